```python
import math
import jax
import jax.numpy as jnp
from jax import lax
import numpy as np

D_MODEL = 4096
BATCH = 1
SEQ = 16384
DEPTH = 1

GRID_W = 64
CTX_LEN = 256
DA_HEADS = 16
DA_QK_DIM = 64
DA_V_DIM = 2 * DA_QK_DIM
DA_WIDTH = DA_HEADS * DA_V_DIM
WA_Q_HEADS = 16
WA_KV_HEADS = 4
WA_GROUP = WA_Q_HEADS // WA_KV_HEADS
WA_HEAD_DIM = 128
WA_WIDTH = WA_Q_HEADS * WA_HEAD_DIM
WINDOW = 128
BLOCK = 128
N_BRANCH = 2
D_FF = 11008
CONV_W = 3
ROPE_BASE = 10000.0
EPS = 1e-6
DA_SCALE = DA_QK_DIM ** -0.5
WA_SCALE = WA_HEAD_DIM ** -0.5
NEG_INF = -1e30
OFF_KA = DA_HEADS * 2 * DA_QK_DIM
OFF_VA = OFF_KA + DA_HEADS * 2 * DA_QK_DIM
OFF_QW = OFF_VA + DA_WIDTH
OFF_KW = OFF_QW + WA_WIDTH
OFF_VW = OFF_KW + WA_KV_HEADS * WA_HEAD_DIM
OFF_GATE = OFF_VW + WA_KV_HEADS * WA_HEAD_DIM
IN_WIDTH = OFF_GATE + N_BRANCH * D_MODEL

kernel_name = "hybrid_diffattn_windowgqa_convffn_dit"


def _rms(x, g):
    xf = x.astype(jnp.float32)
    y = xf * lax.rsqrt(jnp.mean(xf * xf, axis=-1, keepdims=True) + EPS)
    return (y * g.astype(jnp.float32)).astype(x.dtype)


def _adaln(cvec, w, b):
    m = jax.nn.silu(cvec) @ w + b
    return jnp.split(m, 6, axis=-1)


def _modulate(x, shift, scale):
    return x * (1.0 + scale[:, None, :]) + shift[:, None, :]


def _axial_rope_tables(rows, head_dim):
    r = jnp.repeat(jnp.arange(rows, dtype=jnp.float32), GRID_W)
    col = jnp.tile(jnp.arange(GRID_W, dtype=jnp.float32), rows)
    axis_dim = head_dim // 2
    freq = ROPE_BASE ** (-jnp.arange(0, axis_dim, 2, dtype=jnp.float32) / axis_dim)
    ar = r[:, None] * freq
    ac = col[:, None] * freq
    return (jnp.cos(ar), jnp.sin(ar), jnp.cos(ac), jnp.sin(ac))


def _rotate(x, cos, sin):
    x1, x2 = jnp.split(x, 2, axis=-1)
    cos = cos[None, :, None, :].astype(x.dtype)
    sin = sin[None, :, None, :].astype(x.dtype)
    return jnp.concatenate([x1 * cos - x2 * sin, x2 * cos + x1 * sin], axis=-1)


def _rope2d(x, tabs):
    cr, sr, cc, sc = tabs
    half = x.shape[-1] // 2
    return jnp.concatenate([_rotate(x[..., :half], cr, sr), _rotate(x[..., half:], cc, sc)], axis=-1)


def _qk_heads(t, n_heads, dim, g, tabs):
    B, n = t.shape[:2]
    t = _rms(t.reshape(B, n, n_heads, dim), g)
    if tabs is not None:
        t = _rope2d(t, tabs)
    return t


def _diff_attend(q, k, v, lam, lam_init, subln_g):
    s = jnp.einsum('bqhmd,bkhmd->bhmqk', q, k).astype(jnp.float32) * DA_SCALE
    p = jax.nn.softmax(s, axis=-1)
    a = p[:, :, 0] - lam * p[:, :, 1]
    o = jnp.einsum('bhqk,bkhd->bqhd', a.astype(v.dtype), v)
    return _rms(o, subln_g) * (1.0 - lam_init)


def _latent_diff_attention(q, k_all, v_all, lam, lam_init, subln_g):
    B, S = q.shape[:2]
    nb = S // BLOCK
    qb = jnp.moveaxis(q.reshape(B, nb, BLOCK, DA_HEADS, 2, DA_QK_DIM), 1, 0)
    o = lax.map(lambda qq: _diff_attend(qq, k_all, v_all, lam, lam_init, subln_g), qb)
    return jnp.moveaxis(o, 0, 1).reshape(B, S, DA_WIDTH)


def _sink_attend(q, k, v, sink, valid):
    s = jnp.einsum('bqgrd,bkgd->bgrqk', q, k).astype(jnp.float32) * WA_SCALE
    s = jnp.where(valid, s, NEG_INF)
    sink_col = jnp.broadcast_to(
        sink.reshape(WA_KV_HEADS, WA_GROUP)[None, :, :, None, None].astype(jnp.float32),
        s.shape[:-1] + (1,))
    p = jax.nn.softmax(jnp.concatenate([s, sink_col], axis=-1), axis=-1)[..., :-1]
    return jnp.einsum('bgrqk,bkgd->bqgrd', p.astype(v.dtype), v)


def _latent_window_attention(q, k, v, k_ctx, v_ctx, sink):
    B, S = q.shape[:2]
    C = k_ctx.shape[1]
    nb = S // BLOCK
    band = 3 * BLOCK
    qb = jnp.moveaxis(q.reshape(B, nb, BLOCK, WA_KV_HEADS, WA_GROUP, WA_HEAD_DIM), 1, 0)
    pad = ((0, 0), (BLOCK, BLOCK), (0, 0), (0, 0))
    kp = jnp.pad(k, pad)
    vp = jnp.pad(v, pad)
    ctx_ok = jnp.ones((BLOCK, C), dtype=bool)

    def one_block(args):
        i, qq = args
        start = i * BLOCK
        kb = jnp.concatenate([lax.dynamic_slice_in_dim(kp, start, band, axis=1), k_ctx], axis=1)
        vb = jnp.concatenate([lax.dynamic_slice_in_dim(vp, start, band, axis=1), v_ctx], axis=1)
        q_pos = start + jnp.arange(BLOCK)
        k_pos = start - BLOCK + jnp.arange(band)
        near = ((jnp.abs(q_pos[:, None] - k_pos[None, :]) <= WINDOW)
                & (k_pos[None, :] >= 0) & (k_pos[None, :] < S))
        valid = jnp.concatenate([near, ctx_ok], axis=1)
        return _sink_attend(qq, kb, vb, sink, valid)

    o = lax.map(one_block, (jnp.arange(nb), qb))
    return jnp.moveaxis(o, 0, 1).reshape(B, S, WA_WIDTH)


def _merge(y_da, y_wa, gate_logits, b_gate, w_o_da, w_o_wa, w_out):
    g_da, g_wa = jnp.split(jax.nn.sigmoid(gate_logits + b_gate), N_BRANCH, axis=-1)
    return (g_da * (y_da @ w_o_da) + g_wa * (y_wa @ w_o_wa)) @ w_out


def _conv_ffn(h, w_up, conv_w, conv_b, w_down):
    a, u = jnp.split(h @ w_up, 2, axis=-1)
    n = a.shape[1]
    pad = CONV_W // 2
    ap = jnp.pad(a, ((0, 0), (pad, pad), (0, 0)))
    conv = conv_b
    for j in range(CONV_W):
        conv = conv + ap[:, j:j + n] * conv_w[j]
    return (jax.nn.silu(conv) * u) @ w_down


def setup_inputs(seed: int = 0) -> dict:
    key = jax.random.key(seed)
    ks = jax.random.split(key, 27)
    f32 = jnp.float32
    L, D = DEPTH, D_MODEL

    def nrm(k, shape, scale):
        return jax.random.normal(k, shape, f32) * scale

    return {
        "x": nrm(ks[0], (BATCH, SEQ, D), 1.0),
        "c": nrm(ks[1], (BATCH, D), 1.0),
        "ctx": nrm(ks[2], (BATCH, CTX_LEN, D), 1.0),
        "c_ctx": nrm(ks[3], (D,), 1.0),
        "w_ada": nrm(ks[4], (L, D, 6 * D), 0.5 * D ** -0.5),
        "b_ada": nrm(ks[5], (L, 6 * D), 0.02),
        "attn_norm_g": 1.0 + nrm(ks[6], (L, D), 0.02),
        "w_in": nrm(ks[7], (L, D, IN_WIDTH), D ** -0.5),
        "b_gate": nrm(ks[8], (L, N_BRANCH * D), 0.02),
        "da_qn_g": 1.0 + nrm(ks[9], (L, DA_QK_DIM), 0.02),
        "da_kn_g": 1.0 + nrm(ks[10], (L, DA_QK_DIM), 0.02),
        "da_lambda_q1": nrm(ks[11], (L, DA_QK_DIM), 0.1),
        "da_lambda_k1": nrm(ks[12], (L, DA_QK_DIM), 0.1),
        "da_lambda_q2": nrm(ks[13], (L, DA_QK_DIM), 0.1),
        "da_lambda_k2": nrm(ks[14], (L, DA_QK_DIM), 0.1),
        "da_subln_g": 1.0 + nrm(ks[15], (L, DA_V_DIM), 0.02),
        "wa_qn_g": 1.0 + nrm(ks[16], (L, WA_HEAD_DIM), 0.02),
        "wa_kn_g": 1.0 + nrm(ks[17], (L, WA_HEAD_DIM), 0.02),
        "wa_sink": nrm(ks[18], (L, WA_Q_HEADS), 0.5),
        "w_o_da": nrm(ks[19], (L, DA_WIDTH, D), DA_WIDTH ** -0.5),
        "w_o_wa": nrm(ks[20], (L, WA_WIDTH, D), WA_WIDTH ** -0.5),
        "w_out": nrm(ks[21], (L, D, D), D ** -0.5),
        "ffn_norm_g": 1.0 + nrm(ks[22], (L, D), 0.02),
        "w_ffn_up": nrm(ks[23], (L, D, 2 * D_FF), D ** -0.5),
        "ffn_conv_w": nrm(ks[24], (L, CONV_W, D_FF), 0.5),
        "ffn_conv_b": nrm(ks[25], (L, D_FF), 0.02),
        "w_ffn_down": nrm(ks[26], (L, D_FF, D), D_FF ** -0.5),
    }


def reference(x, c, ctx, c_ctx, w_ada, b_ada, attn_norm_g, w_in, b_gate,
              da_qn_g, da_kn_g, da_lambda_q1, da_lambda_k1, da_lambda_q2, da_lambda_k2,
              da_subln_g, wa_qn_g, wa_kn_g, wa_sink, w_o_da, w_o_wa, w_out,
              ffn_norm_g, w_ffn_up, ffn_conv_w, ffn_conv_b, w_ffn_down):
    B, S, _ = x.shape
    C = ctx.shape[1]
    ROWS = S // GRID_W
    tabs_da = _axial_rope_tables(ROWS, DA_QK_DIM)
    tabs_wa = _axial_rope_tables(ROWS, WA_HEAD_DIM)

    for l in range(DEPTH):
        last = l == DEPTH - 1
        wl = w_in[l]
        lam_init = 0.8 - 0.6 * math.exp(-0.3 * l)
        lam = (jnp.exp(jnp.sum(da_lambda_q1[l] * da_lambda_k1[l]).astype(jnp.float32))
               - jnp.exp(jnp.sum(da_lambda_q2[l] * da_lambda_k2[l]).astype(jnp.float32))
               + lam_init)
        sh1, sc1, g1, sh2, sc2, g2 = _adaln(c, w_ada[l], b_ada[l])
        csh1, csc1, cg1, csh2, csc2, cg2 = _adaln(c_ctx[None, :], w_ada[l], b_ada[l])

        h = _modulate(_rms(x, attn_norm_g[l]), sh1, sc1)
        hc = _modulate(_rms(ctx, attn_norm_g[l]), csh1, csc1)

        pr = h @ wl
        qa, ka, va = pr[..., :OFF_KA], pr[..., OFF_KA:OFF_VA], pr[..., OFF_VA:OFF_QW]
        qw, kw, vw = pr[..., OFF_QW:OFF_KW], pr[..., OFF_KW:OFF_VW], pr[..., OFF_VW:OFF_GATE]
        gates = pr[..., OFF_GATE:]
        kva_c = hc @ wl[:, OFF_KA:OFF_QW]
        kvw_c = hc @ wl[:, OFF_KW:OFF_GATE]
        ka_c, va_c = kva_c[..., :OFF_VA - OFF_KA], kva_c[..., OFF_VA - OFF_KA:]
        kw_c, vw_c = kvw_c[..., :OFF_VW - OFF_KW], kvw_c[..., OFF_VW - OFF_KW:]

        q_da = _qk_heads(qa, 2 * DA_HEADS, DA_QK_DIM, da_qn_g[l], tabs_da).reshape(B, S, DA_HEADS, 2, DA_QK_DIM)
        k_da = _qk_heads(ka, 2 * DA_HEADS, DA_QK_DIM, da_kn_g[l], tabs_da).reshape(B, S, DA_HEADS, 2, DA_QK_DIM)
        v_da = va.reshape(B, S, DA_HEADS, DA_V_DIM)
        kc_da = _qk_heads(ka_c, 2 * DA_HEADS, DA_QK_DIM, da_kn_g[l], None).reshape(B, C, DA_HEADS, 2, DA_QK_DIM)
        vc_da = va_c.reshape(B, C, DA_HEADS, DA_V_DIM)
        k_all = jnp.concatenate([k_da, kc_da], axis=1)
        v_all = jnp.concatenate([v_da, vc_da], axis=1)
        y_da = _latent_diff_attention(q_da, k_all, v_all, lam, lam_init, da_subln_g[l])

        q_wa = _qk_heads(qw, WA_Q_HEADS, WA_HEAD_DIM, wa_qn_g[l], tabs_wa).reshape(B, S, WA_KV_HEADS, WA_GROUP, WA_HEAD_DIM)
        k_wa = _qk_heads(kw, WA_KV_HEADS, WA_HEAD_DIM, wa_kn_g[l], tabs_wa)
        v_wa = vw.reshape(B, S, WA_KV_HEADS, WA_HEAD_DIM)
        kc_wa = _qk_heads(kw_c, WA_KV_HEADS, WA_HEAD_DIM, wa_kn_g[l], None)
        vc_wa = vw_c.reshape(B, C, WA_KV_HEADS, WA_HEAD_DIM)
        y_wa = _latent_window_attention(q_wa, k_wa, v_wa, kc_wa, vc_wa, wa_sink[l])

        mix = _merge(y_da, y_wa, gates, b_gate[l], w_o_da[l], w_o_wa[l], w_out[l])

        if not last:
            qc_da = _qk_heads(hc @ wl[:, :OFF_KA], 2 * DA_HEADS, DA_QK_DIM, da_qn_g[l], None).reshape(B, C, DA_HEADS, 2, DA_QK_DIM)
            yc_da = _diff_attend(qc_da, kc_da, vc_da, lam, lam_init, da_subln_g[l]).reshape(B, C, DA_WIDTH)
            qc_wa = _qk_heads(hc @ wl[:, OFF_QW:OFF_KW], WA_Q_HEADS, WA_HEAD_DIM, wa_qn_g[l], None).reshape(B, C, WA_KV_HEADS, WA_GROUP, WA_HEAD_DIM)
            yc_wa = _sink_attend(qc_wa, kc_wa, vc_wa, wa_sink[l], jnp.ones((C, C), dtype=bool)).reshape(B, C, WA_WIDTH)
            mixc = _merge(yc_da, yc_wa, hc @ wl[:, OFF_GATE:], b_gate[l], w_o_da[l], w_o_wa[l], w_out[l])
            ctx = ctx + cg1[:, None, :] * mixc
            hc2 = _modulate(_rms(ctx, ffn_norm_g[l]), csh2, csc2)
            ctx = ctx + cg2[:, None, :] * _conv_ffn(hc2, w_ffn_up[l], ffn_conv_w[l], ffn_conv_b[l], w_ffn_down[l])

        x = x + g1[:, None, :] * mix

        h2 = _modulate(_rms(x, ffn_norm_g[l]), sh2, sc2)
        x = x + g2[:, None, :] * _conv_ffn(h2, w_ffn_up[l], ffn_conv_w[l], ffn_conv_b[l], w_ffn_down[l])

    return x
```

```python
import functools
import math

import jax
import jax.numpy as jnp
from jax import lax
from jax.experimental import pallas as pl
from jax.experimental.pallas import tpu as pltpu

F32 = jnp.float32
BF16 = jnp.bfloat16

GRID_W = 64
DA_QK_DIM = 64
DA_V_DIM = 2 * DA_QK_DIM
WA_HEAD_DIM = 128
WA_GROUP = 4
WINDOW = 128
N_BRANCH = 2
CONV_W = 3
ROPE_BASE = 10000.0
EPS = 1e-6
DA_SCALE = DA_QK_DIM ** -0.5
WA_SCALE = WA_HEAD_DIM ** -0.5
NEG_INF = -1e30
LAM_INIT = 0.8 - 0.6 * math.exp(-0.3 * 0)

LANES = 128
BF16_SUBLANES = 16
V7X_VMEM_CAP_BYTES = 58 * 1024 * 1024
VMEM_SLACK_BYTES = 6 * 1024 * 1024


def _pick(dim, pref, unit):
    best = None
    t = unit
    while t <= min(dim, pref):
        if dim % t == 0:
            best = t
        t += unit
    if best is None:
        raise ValueError(f"no tile for dim={dim} unit={unit}")
    return best


def _params(semantics, block_bytes, scratch_bytes=0):
    est = 2 * block_bytes + scratch_bytes + VMEM_SLACK_BYTES
    return pltpu.CompilerParams(
        dimension_semantics=semantics,
        vmem_limit_bytes=min(max(est, 16 * 1024 * 1024), V7X_VMEM_CAP_BYTES))


def _sigmoid(x):
    return 1.0 / (1.0 + jnp.exp(-x))


def _adaln_kernel(c_ref, w_ref, b_ref, o_ref):
    a = c_ref[...]
    a = a * _sigmoid(a)
    o_ref[...] = jnp.dot(a.astype(BF16), w_ref[...].astype(BF16),
                         preferred_element_type=F32) + b_ref[...]


def _adaln(cc, w, b):
    rows, d = cc.shape
    n = w.shape[1]
    tn = _pick(n, 512, LANES)
    return pl.pallas_call(
        _adaln_kernel,
        grid=(n // tn,),
        in_specs=[pl.BlockSpec((rows, d), lambda j: (0, 0)),
                  pl.BlockSpec((d, tn), lambda j: (0, j)),
                  pl.BlockSpec((1, tn), lambda j: (0, j))],
        out_specs=pl.BlockSpec((rows, tn), lambda j: (0, j)),
        out_shape=jax.ShapeDtypeStruct((rows, n), F32),
        compiler_params=_params(("arbitrary",), d * tn * 4 + d * tn * 2),
        name="adaln",
    )(cc, w, b.reshape(1, n))


def _normmod_rows(x, g, mod):
    ms = jnp.mean(x * x, axis=-1, keepdims=True)
    y = x * lax.rsqrt(ms + EPS) * g
    return (y * (1.0 + mod[1:2, :]) + mod[0:1, :]).astype(BF16)


def _normmod2_kernel(x_ref, c_ref, g_ref, mx_ref, mc_ref, o_ref, *, nx):
    i = pl.program_id(0)

    @pl.when(i < nx)
    def _():
        o_ref[...] = _normmod_rows(x_ref[...], g_ref[...], mx_ref[...])

    @pl.when(i >= nx)
    def _():
        o_ref[...] = _normmod_rows(c_ref[...], g_ref[...], mc_ref[...])


def _normmod1_kernel(x_ref, g_ref, mx_ref, o_ref):
    o_ref[...] = _normmod_rows(x_ref[...], g_ref[...], mx_ref[...])


def _normmod(x, g, mod_x, ctx=None, mod_c=None):
    s, d = x.shape
    g = g.reshape(1, d)
    if ctx is None:
        tm = _pick(s, 256, BF16_SUBLANES)
        return pl.pallas_call(
            _normmod1_kernel,
            grid=(s // tm,),
            in_specs=[pl.BlockSpec((tm, d), lambda i: (i, 0)),
                      pl.BlockSpec((1, d), lambda i: (0, 0)),
                      pl.BlockSpec((2, d), lambda i: (0, 0))],
            out_specs=pl.BlockSpec((tm, d), lambda i: (i, 0)),
            out_shape=jax.ShapeDtypeStruct((s, d), BF16),
            compiler_params=_params(("arbitrary",), tm * d * 6),
            name="normmod",
        )(x, g, mod_x)
    c = ctx.shape[0]
    tm = _pick(math.gcd(s, c), 256, BF16_SUBLANES)
    nx, nc = s // tm, c // tm
    return pl.pallas_call(
        functools.partial(_normmod2_kernel, nx=nx),
        grid=(nx + nc,),
        in_specs=[pl.BlockSpec((tm, d), lambda i: (jnp.minimum(i, nx - 1), 0)),
                  pl.BlockSpec((tm, d), lambda i: (jnp.maximum(i - nx, 0), 0)),
                  pl.BlockSpec((1, d), lambda i: (0, 0)),
                  pl.BlockSpec((2, d), lambda i: (0, 0)),
                  pl.BlockSpec((2, d), lambda i: (0, 0))],
        out_specs=pl.BlockSpec((tm, d), lambda i: (i, 0)),
        out_shape=jax.ShapeDtypeStruct((s + c, d), BF16),
        compiler_params=_params(("arbitrary",), tm * d * 10),
        name="normmod_xc",
    )(x, ctx, g, mod_x, mod_c)


def _mm_kernel(a_ref, w_ref, o_ref):
    o_ref[...] = jnp.dot(a_ref[...], w_ref[...], preferred_element_type=F32)


def _matmul(a, w, tm_pref, tn_pref):
    m, k = a.shape
    n = w.shape[1]
    tm = _pick(m, tm_pref, BF16_SUBLANES)
    tn = _pick(n, tn_pref, LANES)
    return pl.pallas_call(
        _mm_kernel,
        grid=(m // tm, n // tn),
        in_specs=[pl.BlockSpec((tm, k), lambda i, j: (i, 0)),
                  pl.BlockSpec((k, tn), lambda i, j: (0, j))],
        out_specs=pl.BlockSpec((tm, tn), lambda i, j: (i, j)),
        out_shape=jax.ShapeDtypeStruct((m, n), F32),
        compiler_params=_params(("arbitrary", "arbitrary"),
                                tm * k * 2 + k * tn * 2 + tm * tn * 4, scratch_bytes=tm * tn * 4),
        name="in_proj",
    )(a, w)


def _rope_tables(s, c):
    pos = jnp.arange(s, dtype=jnp.int32)
    r = (pos // GRID_W).astype(F32)[:, None]
    col = (pos % GRID_W).astype(F32)[:, None]
    lane = jnp.arange(LANES, dtype=jnp.int32)

    def table(head_dim):
        axis_dim = head_dim // 2
        pair = axis_dim // 2
        within = lane % head_dim
        f = (within % pair).astype(F32)
        freq = ROPE_BASE ** (-(2.0 * f) / axis_dim)
        use_row = (within // axis_dim) == 0
        ang = jnp.where(use_row[None, :], r * freq[None, :], col * freq[None, :])
        first = (within % axis_dim) < pair
        cos = jnp.cos(ang)
        sin = jnp.where(first[None, :], -jnp.sin(ang), jnp.sin(ang))
        cos = jnp.concatenate([cos, jnp.ones((c, LANES), F32)], axis=0)
        sin = jnp.concatenate([sin, jnp.zeros((c, LANES), F32)], axis=0)
        return cos, sin

    cd, sd = table(DA_QK_DIM)
    cw, sw = table(WA_HEAD_DIM)
    return jnp.stack([cd, sd, cw, sw], axis=0)


def _headprep_kernel(pr_ref, tab_ref, gain_ref, o_ref, *, n_qk_da, n_v_da, n_q_wa, n_k_wa, n_v_wa):
    tm = pr_ref.shape[0]
    lane = lax.broadcasted_iota(jnp.int32, (tm, LANES), 1)
    lo = lane < DA_QK_DIM
    first_da = (lane % (DA_QK_DIM // 2)) < (DA_QK_DIM // 4)
    first_wa = (lane % (WA_HEAD_DIM // 2)) < (WA_HEAD_DIM // 4)
    cos_da, sin_da = tab_ref[0], tab_ref[1]
    cos_wa, sin_wa = tab_ref[2], tab_ref[3]

    def rope(y, cos, sin, first, shift):
        partner = jnp.where(first, pltpu.roll(y, LANES - shift, 1), pltpu.roll(y, shift, 1))
        return y * cos + partner * sin

    def da_group(gidx, gain):
        x = pr_ref[:, gidx * LANES:(gidx + 1) * LANES]
        x2 = x * x
        s_lo = jnp.sum(jnp.where(lo, x2, 0.0), axis=-1, keepdims=True)
        s_hi = jnp.sum(jnp.where(lo, 0.0, x2), axis=-1, keepdims=True)
        ms = jnp.where(lo, s_lo, s_hi) * (1.0 / DA_QK_DIM)
        y = x * lax.rsqrt(ms + EPS) * gain
        y = rope(y, cos_da, sin_da, first_da, DA_QK_DIM // 4)
        o_ref[:, gidx * LANES:(gidx + 1) * LANES] = y.astype(BF16)

    def wa_group(gidx, gain):
        x = pr_ref[:, gidx * LANES:(gidx + 1) * LANES]
        ms = jnp.mean(x * x, axis=-1, keepdims=True)
        y = x * lax.rsqrt(ms + EPS) * gain
        y = rope(y, cos_wa, sin_wa, first_wa, WA_HEAD_DIM // 4)
        o_ref[:, gidx * LANES:(gidx + 1) * LANES] = y.astype(BF16)

    def copy_group(gidx):
        o_ref[:, gidx * LANES:(gidx + 1) * LANES] = pr_ref[:, gidx * LANES:(gidx + 1) * LANES].astype(BF16)

    g = 0
    for _ in range(n_qk_da):
        da_group(g, gain_ref[0:1, :]); g += 1
    for _ in range(n_qk_da):
        da_group(g, gain_ref[1:2, :]); g += 1
    for _ in range(n_v_da):
        copy_group(g); g += 1
    for _ in range(n_q_wa):
        wa_group(g, gain_ref[2:3, :]); g += 1
    for _ in range(n_k_wa):
        wa_group(g, gain_ref[3:4, :]); g += 1
    for _ in range(n_v_wa):
        copy_group(g); g += 1


def _headprep(pr, tabs, gains, da_heads, wa_q_heads, tm_pref=128):
    rows = pr.shape[0]
    wa_kv_heads = wa_q_heads // WA_GROUP
    groups = dict(n_qk_da=da_heads, n_v_da=da_heads, n_q_wa=wa_q_heads,
                  n_k_wa=wa_kv_heads, n_v_wa=wa_kv_heads)
    width = (3 * da_heads + wa_q_heads + 2 * wa_kv_heads) * LANES
    tm = _pick(rows, tm_pref, BF16_SUBLANES)
    return pl.pallas_call(
        functools.partial(_headprep_kernel, **groups),
        grid=(rows // tm,),
        in_specs=[pl.BlockSpec((tm, width), lambda i: (i, 0)),
                  pl.BlockSpec((4, tm, LANES), lambda i: (0, i, 0)),
                  pl.BlockSpec((4, LANES), lambda i: (0, 0))],
        out_specs=pl.BlockSpec((tm, width), lambda i: (i, 0)),
        out_shape=jax.ShapeDtypeStruct((rows, width), BF16),
        compiler_params=_params(("arbitrary",), tm * width * 6 + 4 * tm * LANES * 4),
        name="headprep",
    )(pr, tabs, gains)


def _nt_dot(a, b):
    return lax.dot_general(a, b, (((1,), (1,)), ((), ())), preferred_element_type=F32)


def _diffattn_kernel(lam_ref, q_ref, k_ref, v_ref, g_ref, o_ref, *, tk):
    tq = q_ref.shape[0]
    nk = k_ref.shape[0] // tk
    q = q_ref[...]
    lane = lax.broadcasted_iota(jnp.int32, (tq, LANES), 1)
    zero = jnp.zeros_like(q)
    q1 = jnp.where(lane < DA_QK_DIM, q, zero)
    q2 = jnp.where(lane < DA_QK_DIM, zero, q)

    def half(qh, k, v, m, l, acc):
        s = _nt_dot(qh, k)
        m_new = jnp.maximum(m, jnp.max(s, axis=-1, keepdims=True))
        alpha = jnp.exp(m - m_new)
        p = jnp.exp(s - m_new)
        l = alpha * l + jnp.sum(p, axis=-1, keepdims=True)
        acc = alpha * acc + jnp.dot(p.astype(BF16), v, preferred_element_type=F32)
        return m_new, l, acc

    def body(j, carry):
        m1, l1, a1, m2, l2, a2 = carry
        start = pl.multiple_of(j * tk, tk)
        k = k_ref[pl.ds(start, tk), :]
        v = v_ref[pl.ds(start, tk), :]
        m1, l1, a1 = half(q1, k, v, m1, l1, a1)
        m2, l2, a2 = half(q2, k, v, m2, l2, a2)
        return m1, l1, a1, m2, l2, a2

    m0 = jnp.full((tq, 1), NEG_INF, F32)
    l0 = jnp.zeros((tq, 1), F32)
    a0 = jnp.zeros((tq, DA_V_DIM), F32)
    m1, l1, a1, m2, l2, a2 = lax.fori_loop(0, nk, body, (m0, l0, a0, m0, l0, a0))

    lam = (jnp.exp(jnp.sum(lam_ref[0:1, :] * lam_ref[1:2, :], axis=-1, keepdims=True))
           - jnp.exp(jnp.sum(lam_ref[2:3, :] * lam_ref[3:4, :], axis=-1, keepdims=True))
           + LAM_INIT)
    o = a1 / l1 - lam * (a2 / l2)
    ms = jnp.mean(o * o, axis=-1, keepdims=True)
    o = o * lax.rsqrt(ms + EPS) * g_ref[...] * (1.0 - LAM_INIT)
    o_ref[...] = o.astype(BF16)


def _diffattn(qkv, lam_vecs, subln_g, s, da_heads, tq_pref=256, tk_pref=640):
    rows = qkv.shape[0]
    tq = _pick(s, tq_pref, BF16_SUBLANES)
    tk = _pick(rows, tk_pref, LANES)
    kb, vb = da_heads, 2 * da_heads
    return pl.pallas_call(
        functools.partial(_diffattn_kernel, tk=tk),
        grid=(da_heads, s // tq),
        in_specs=[pl.BlockSpec((4, DA_QK_DIM), lambda h, i: (0, 0)),
                  pl.BlockSpec((tq, LANES), lambda h, i: (i, h)),
                  pl.BlockSpec((rows, LANES), lambda h, i: (0, kb + h)),
                  pl.BlockSpec((rows, LANES), lambda h, i: (0, vb + h)),
                  pl.BlockSpec((1, DA_V_DIM), lambda h, i: (0, 0))],
        out_specs=pl.BlockSpec((tq, DA_V_DIM), lambda h, i: (i, h)),
        out_shape=jax.ShapeDtypeStruct((s, da_heads * DA_V_DIM), BF16),
        compiler_params=_params(("arbitrary", "arbitrary"),
                                2 * rows * LANES * 2 + 2 * tq * LANES * 2,
                                scratch_bytes=6 * tq * tk * 4),
        name="diff_attn",
    )(lam_vecs, qkv, qkv, qkv, subln_g.reshape(1, DA_V_DIM))


def _winattn_kernel(sink_ref, q_ref, k_ref, v_ref, o_ref, *, s, c):
    g = pl.program_id(0)
    i = pl.program_id(1)
    tq = q_ref.shape[0]
    band = tq + 2 * WINDOW
    q0 = i * tq
    start = jnp.clip(q0 - WINDOW, 0, s - band)
    start = pl.multiple_of(start, WINDOW)
    kb = k_ref[pl.ds(start, band), :]
    vb = v_ref[pl.ds(start, band), :]
    kc = k_ref[s:s + c, :]
    vc = v_ref[s:s + c, :]
    q_pos = q0 + lax.broadcasted_iota(jnp.int32, (tq, band), 0)
    k_pos = start + lax.broadcasted_iota(jnp.int32, (tq, band), 1)
    valid = jnp.abs(q_pos - k_pos) <= WINDOW
    for r in range(WA_GROUP):
        q = q_ref[:, r * WA_HEAD_DIM:(r + 1) * WA_HEAD_DIM]
        sb = jnp.where(valid, _nt_dot(q, kb) * WA_SCALE, NEG_INF)
        sc = _nt_dot(q, kc) * WA_SCALE
        sink = sink_ref[g * WA_GROUP + r]
        m = jnp.maximum(jnp.max(sb, axis=-1, keepdims=True), jnp.max(sc, axis=-1, keepdims=True))
        m = jnp.maximum(m, sink)
        pb = jnp.exp(sb - m)
        pc = jnp.exp(sc - m)
        l = (jnp.sum(pb, axis=-1, keepdims=True) + jnp.sum(pc, axis=-1, keepdims=True)
             + jnp.exp(sink - m))
        inv = 1.0 / l
        o = (jnp.dot((pb * inv).astype(BF16), vb, preferred_element_type=F32)
             + jnp.dot((pc * inv).astype(BF16), vc, preferred_element_type=F32))
        o_ref[:, r * WA_HEAD_DIM:(r + 1) * WA_HEAD_DIM] = o.astype(BF16)


def _winattn(qkv, sink, s, c, da_heads, wa_q_heads, tq_pref=256):
    rows = qkv.shape[0]
    kvh = wa_q_heads // WA_GROUP
    tq = _pick(s, tq_pref, WINDOW)
    qb = 3 * da_heads // WA_GROUP
    kb = 3 * da_heads + wa_q_heads
    vb = kb + kvh
    gw = WA_GROUP * WA_HEAD_DIM
    return pl.pallas_call(
        functools.partial(_winattn_kernel, s=s, c=c),
        grid=(kvh, s // tq),
        in_specs=[pl.BlockSpec(memory_space=pltpu.SMEM),
                  pl.BlockSpec((tq, gw), lambda g, i: (i, qb + g)),
                  pl.BlockSpec((rows, LANES), lambda g, i: (0, kb + g)),
                  pl.BlockSpec((rows, LANES), lambda g, i: (0, vb + g))],
        out_specs=pl.BlockSpec((tq, gw), lambda g, i: (i, g)),
        out_shape=jax.ShapeDtypeStruct((s, wa_q_heads * WA_HEAD_DIM), BF16),
        compiler_params=_params(("arbitrary", "arbitrary"),
                                2 * rows * LANES * 2 + 2 * tq * gw * 2,
                                scratch_bytes=8 * tq * (tq + 2 * WINDOW + c) * 4),
        name="win_attn",
    )(sink, qkv, qkv, qkv)


def _merge_kernel(yd_ref, yw_ref, wd_ref, ww_ref, gd_ref, gw_ref, bd_ref, bw_ref, o_ref):
    pd = jnp.dot(yd_ref[...], wd_ref[...], preferred_element_type=F32)
    pw = jnp.dot(yw_ref[...], ww_ref[...], preferred_element_type=F32)
    gd = _sigmoid(gd_ref[...] + bd_ref[...])
    gw = _sigmoid(gw_ref[...] + bw_ref[...])
    o_ref[...] = (gd * pd + gw * pw).astype(BF16)


def _merge(y_da, y_wa, w_o_da, w_o_wa, pr, gate_off, b_gate, tm_pref=512, tn_pref=1024):
    s, kd = y_da.shape
    kw = y_wa.shape[1]
    d = w_o_da.shape[1]
    tm = _pick(s, tm_pref, BF16_SUBLANES)
    tn = _pick(math.gcd(d, gate_off), tn_pref, LANES)
    od, ow = gate_off // tn, (gate_off + d) // tn
    nb = d // tn
    bg = b_gate.reshape(1, N_BRANCH * d)
    return pl.pallas_call(
        _merge_kernel,
        grid=(s // tm, nb),
        in_specs=[pl.BlockSpec((tm, kd), lambda i, j: (i, 0)),
                  pl.BlockSpec((tm, kw), lambda i, j: (i, 0)),
                  pl.BlockSpec((kd, tn), lambda i, j: (0, j)),
                  pl.BlockSpec((kw, tn), lambda i, j: (0, j)),
                  pl.BlockSpec((tm, tn), lambda i, j: (i, od + j)),
                  pl.BlockSpec((tm, tn), lambda i, j: (i, ow + j)),
                  pl.BlockSpec((1, tn), lambda i, j: (0, j)),
                  pl.BlockSpec((1, tn), lambda i, j: (0, nb + j))],
        out_specs=pl.BlockSpec((tm, tn), lambda i, j: (i, j)),
        out_shape=jax.ShapeDtypeStruct((s, d), BF16),
        compiler_params=_params(("arbitrary", "arbitrary"),
                                tm * (kd + kw) * 2 + (kd + kw) * tn * 2 + 2 * tm * tn * 4 + tm * tn * 2,
                                scratch_bytes=4 * tm * tn * 4),
        name="merge",
    )(y_da, y_wa, w_o_da, w_o_wa, pr, pr, bg, bg)


def _mm_res_kernel(a_ref, w_ref, x_ref, g_ref, o_ref, acc_ref):
    kk = pl.program_id(2)

    @pl.when(kk == 0)
    def _():
        acc_ref[...] = jnp.zeros_like(acc_ref)

    acc_ref[...] += jnp.dot(a_ref[...], w_ref[...], preferred_element_type=F32)

    @pl.when(kk == pl.num_programs(2) - 1)
    def _():
        o_ref[...] = x_ref[...] + g_ref[...] * acc_ref[...]


def _matmul_residual(a, w, x, gate, tm_pref, tn_pref, tk_pref, name):
    m, k = a.shape
    n = w.shape[1]
    tm = _pick(m, tm_pref, BF16_SUBLANES)
    tn = _pick(n, tn_pref, LANES)
    tk = _pick(k, tk_pref, LANES)
    return pl.pallas_call(
        _mm_res_kernel,
        grid=(m // tm, n // tn, k // tk),
        in_specs=[pl.BlockSpec((tm, tk), lambda i, j, kk: (i, kk)),
                  pl.BlockSpec((tk, tn), lambda i, j, kk: (kk, j)),
                  pl.BlockSpec((tm, tn), lambda i, j, kk: (i, j)),
                  pl.BlockSpec((1, tn), lambda i, j, kk: (0, j))],
        out_specs=pl.BlockSpec((tm, tn), lambda i, j, kk: (i, j)),
        out_shape=jax.ShapeDtypeStruct((m, n), F32),
        scratch_shapes=[pltpu.VMEM((tm, tn), F32)],
        compiler_params=_params(("arbitrary", "arbitrary", "arbitrary"),
                                tm * tk * 2 + tk * tn * 2 + 2 * tm * tn * 4,
                                scratch_bytes=2 * tm * tn * 4),
        name=name,
    )(a, w, x, gate)


HALO = BF16_SUBLANES


def _ffn_up_kernel(h_ref, hp_ref, hn_ref, wa_ref, wu_ref, cw_ref, cb_ref, o_ref, lhs_ref):
    i = pl.program_id(0)
    j = pl.program_id(1)
    tm = h_ref.shape[0]

    @pl.when(j == 0)
    def _():
        prev = jnp.where(i > 0, hp_ref[...], jnp.zeros_like(hp_ref))
        nxt = jnp.where(i < pl.num_programs(0) - 1, hn_ref[...], jnp.zeros_like(hn_ref))
        lhs_ref[0:HALO, :] = prev
        lhs_ref[HALO:HALO + tm, :] = h_ref[...]
        lhs_ref[HALO + tm:, :] = nxt

    a = jnp.dot(lhs_ref[...], wa_ref[...], preferred_element_type=F32)
    u = jnp.dot(lhs_ref[HALO:HALO + tm, :], wu_ref[...], preferred_element_type=F32)
    rows = tm + 2 * HALO
    a_prev = pltpu.roll(a, 1, 0)[HALO:HALO + tm, :]
    a_next = pltpu.roll(a, rows - 1, 0)[HALO:HALO + tm, :]
    a_mid = a[HALO:HALO + tm, :]
    conv = cb_ref[...] + a_prev * cw_ref[0:1, :] + a_mid * cw_ref[1:2, :] + a_next * cw_ref[2:3, :]
    o_ref[...] = (conv * _sigmoid(conv) * u).astype(BF16)


def _ffn_up(h, w_up, conv_w, conv_b, ffp, tm_pref=512, tn_pref=512):
    s, d = h.shape
    tm = _pick(s, tm_pref, HALO)
    tn = _pick(ffp, tn_pref, LANES)
    nf = ffp // tn
    per = tm // HALO
    last = s // HALO - 1
    return pl.pallas_call(
        _ffn_up_kernel,
        grid=(s // tm, nf),
        in_specs=[pl.BlockSpec((tm, d), lambda i, j: (i, 0)),
                  pl.BlockSpec((HALO, d), lambda i, j: (jnp.maximum(i * per - 1, 0), 0)),
                  pl.BlockSpec((HALO, d), lambda i, j: (jnp.minimum((i + 1) * per, last), 0)),
                  pl.BlockSpec((d, tn), lambda i, j: (0, j)),
                  pl.BlockSpec((d, tn), lambda i, j: (0, nf + j)),
                  pl.BlockSpec((CONV_W, tn), lambda i, j: (0, j)),
                  pl.BlockSpec((1, tn), lambda i, j: (0, j))],
        out_specs=pl.BlockSpec((tm, tn), lambda i, j: (i, j)),
        out_shape=jax.ShapeDtypeStruct((s, ffp), BF16),
        scratch_shapes=[pltpu.VMEM((tm + 2 * HALO, d), BF16)],
        compiler_params=_params(("arbitrary", "arbitrary"),
                                (tm + 2 * HALO) * d * 2 + 2 * d * tn * 2 + tm * tn * 2,
                                scratch_bytes=(tm + 2 * HALO) * d * 2 + 6 * (tm + 2 * HALO) * tn * 4),
        name="ffn_up",
    )(h, h, h, w_up, w_up, conv_w, conv_b)


def kernel(x, c, ctx, c_ctx, w_ada, b_ada, attn_norm_g, w_in, b_gate, da_qn_g, da_kn_g, da_lambda_q1, da_lambda_k1, da_lambda_q2, da_lambda_k2, da_subln_g, wa_qn_g, wa_kn_g, wa_sink, w_o_da, w_o_wa, w_out, ffn_norm_g, w_ffn_up, ffn_conv_w, ffn_conv_b, w_ffn_down):
    b, s, d = x.shape
    cl = ctx.shape[1]
    assert b == 1 and w_in.shape[0] == 1, "one batch element, one layer"
    da_heads = w_o_da.shape[1] // DA_V_DIM
    wa_q_heads = w_o_wa.shape[1] // WA_HEAD_DIM
    wa_kv_heads = wa_q_heads // WA_GROUP
    d_ff = ffn_conv_b.shape[1]
    gate_off = (3 * da_heads + wa_q_heads + 2 * wa_kv_heads) * LANES
    assert w_in.shape[2] == gate_off + N_BRANCH * d

    x2 = x.reshape(s, d)
    ctx2 = ctx.reshape(cl, d)

    ffp = -(-d_ff // 1024) * 1024 if d_ff > 1024 else d_ff
    padf = ffp - d_ff
    w_in_b = w_in[0].astype(BF16)
    w_o_da_b = w_o_da[0].astype(BF16)
    w_o_wa_b = w_o_wa[0].astype(BF16)
    w_out_b = w_out[0].astype(BF16)
    wu = w_ffn_up[0]
    w_up_b = jnp.concatenate(
        [jnp.pad(wu[:, :d_ff], ((0, 0), (0, padf))), jnp.pad(wu[:, d_ff:], ((0, 0), (0, padf)))],
        axis=1).astype(BF16)
    w_down_b = jnp.pad(w_ffn_down[0], ((0, padf), (0, 0))).astype(BF16)
    conv_w = jnp.pad(ffn_conv_w[0], ((0, 0), (0, padf)))
    conv_b = jnp.pad(ffn_conv_b[0], (0, padf)).reshape(1, ffp)

    cc = jnp.concatenate([c, c_ctx[None, :], jnp.zeros((6, d), F32)], axis=0)
    mod = _adaln(cc, w_ada[0], b_ada[0])
    sh1, sc1, g1 = mod[0:1, 0:d], mod[0:1, d:2 * d], mod[0:1, 2 * d:3 * d]
    sh2, sc2, g2 = mod[0:1, 3 * d:4 * d], mod[0:1, 4 * d:5 * d], mod[0:1, 5 * d:6 * d]
    mod_x1 = jnp.concatenate([sh1, sc1], axis=0)
    mod_c1 = jnp.concatenate([mod[1:2, 0:d], mod[1:2, d:2 * d]], axis=0)
    mod_x2 = jnp.concatenate([sh2, sc2], axis=0)

    h = _normmod(x2, attn_norm_g[0], mod_x1, ctx2, mod_c1)
    pr = _matmul(h, w_in_b, 1280, 512)
    tabs = _rope_tables(s, cl)
    gains = jnp.stack([jnp.tile(da_qn_g[0] * DA_SCALE, 2), jnp.tile(da_kn_g[0], 2),
                       wa_qn_g[0], wa_kn_g[0]], axis=0)
    qkv = _headprep(pr, tabs, gains, da_heads, wa_q_heads)
    lam_vecs = jnp.stack([da_lambda_q1[0], da_lambda_k1[0], da_lambda_q2[0], da_lambda_k2[0]], axis=0)
    y_da = _diffattn(qkv, lam_vecs, da_subln_g[0], s, da_heads)
    y_wa = _winattn(qkv, wa_sink[0], s, cl, da_heads, wa_q_heads)
    u = _merge(y_da, y_wa, w_o_da_b, w_o_wa_b, pr, gate_off, b_gate[0])
    x1 = _matmul_residual(u, w_out_b, x2, g1, 1024, 1024, d, "out_proj")

    h2 = _normmod(x1, ffn_norm_g[0], mod_x2)
    act = _ffn_up(h2, w_up_b, conv_w, conv_b, ffp)
    out = _matmul_residual(act, w_down_b, x1, g2, 1024, 1024, 2816, "ffn_down")
    return out.reshape(b, s, d)
```

```python
import functools
import math

import jax
import jax.numpy as jnp
from jax import lax
from jax.experimental import pallas as pl
from jax.experimental.pallas import tpu as pltpu

F32 = jnp.float32
BF16 = jnp.bfloat16

GRID_W = 64
DA_QK_DIM = 64
DA_V_DIM = 2 * DA_QK_DIM
WA_HEAD_DIM = 128
WA_GROUP = 4
WINDOW = 128
N_BRANCH = 2
CONV_W = 3
ROPE_BASE = 10000.0
EPS = 1e-6
DA_SCALE = DA_QK_DIM ** -0.5
WA_SCALE = WA_HEAD_DIM ** -0.5
NEG_INF = -1e30
LAM_INIT = 0.8 - 0.6 * math.exp(-0.3 * 0)

LANES = 128
BF16_SUBLANES = 16
V7X_VMEM_CAP_BYTES = 58 * 1024 * 1024
VMEM_SLACK_BYTES = 6 * 1024 * 1024


def _pick(dim, pref, unit):
    best = None
    t = unit
    while t <= min(dim, pref):
        if dim % t == 0:
            best = t
        t += unit
    if best is None:
        raise ValueError(f"no tile for dim={dim} unit={unit}")
    return best


def _params(semantics, block_bytes, scratch_bytes=0):
    est = 2 * block_bytes + scratch_bytes + VMEM_SLACK_BYTES
    return pltpu.CompilerParams(
        dimension_semantics=semantics,
        vmem_limit_bytes=min(max(est, 16 * 1024 * 1024), V7X_VMEM_CAP_BYTES))


def _sigmoid(x):
    return 1.0 / (1.0 + jnp.exp(-x))


def _adaln_kernel(c_ref, w_ref, b_ref, o_ref):
    a = c_ref[...]
    a = a * _sigmoid(a)
    o_ref[...] = jnp.dot(a.astype(BF16), w_ref[...].astype(BF16),
                         preferred_element_type=F32) + b_ref[...]


def _adaln(cc, w, b):
    rows, d = cc.shape
    n = w.shape[1]
    tn = _pick(n, 512, LANES)
    return pl.pallas_call(
        _adaln_kernel,
        grid=(n // tn,),
        in_specs=[pl.BlockSpec((rows, d), lambda j: (0, 0)),
                  pl.BlockSpec((d, tn), lambda j: (0, j)),
                  pl.BlockSpec((1, tn), lambda j: (0, j))],
        out_specs=pl.BlockSpec((rows, tn), lambda j: (0, j)),
        out_shape=jax.ShapeDtypeStruct((rows, n), F32),
        compiler_params=_params(("arbitrary",), d * tn * 4 + d * tn * 2),
        name="adaln",
    )(cc, w, b.reshape(1, n))


def _normmod_rows(x, g, mod):
    ms = jnp.mean(x * x, axis=-1, keepdims=True)
    y = x * lax.rsqrt(ms + EPS) * g
    return (y * (1.0 + mod[1:2, :]) + mod[0:1, :]).astype(BF16)


def _normmod2_kernel(x_ref, c_ref, g_ref, mx_ref, mc_ref, o_ref, *, nx):
    i = pl.program_id(0)

    @pl.when(i < nx)
    def _():
        o_ref[...] = _normmod_rows(x_ref[...], g_ref[...], mx_ref[...])

    @pl.when(i >= nx)
    def _():
        o_ref[...] = _normmod_rows(c_ref[...], g_ref[...], mc_ref[...])


def _normmod1_kernel(x_ref, g_ref, mx_ref, o_ref):
    o_ref[...] = _normmod_rows(x_ref[...], g_ref[...], mx_ref[...])


def _normmod(x, g, mod_x, ctx=None, mod_c=None):
    s, d = x.shape
    g = g.reshape(1, d)
    if ctx is None:
        tm = _pick(s, 256, BF16_SUBLANES)
        return pl.pallas_call(
            _normmod1_kernel,
            grid=(s // tm,),
            in_specs=[pl.BlockSpec((tm, d), lambda i: (i, 0)),
                      pl.BlockSpec((1, d), lambda i: (0, 0)),
                      pl.BlockSpec((2, d), lambda i: (0, 0))],
            out_specs=pl.BlockSpec((tm, d), lambda i: (i, 0)),
            out_shape=jax.ShapeDtypeStruct((s, d), BF16),
            compiler_params=_params(("arbitrary",), tm * d * 6),
            name="normmod",
        )(x, g, mod_x)
    c = ctx.shape[0]
    tm = _pick(math.gcd(s, c), 256, BF16_SUBLANES)
    nx, nc = s // tm, c // tm
    return pl.pallas_call(
        functools.partial(_normmod2_kernel, nx=nx),
        grid=(nx + nc,),
        in_specs=[pl.BlockSpec((tm, d), lambda i: (jnp.minimum(i, nx - 1), 0)),
                  pl.BlockSpec((tm, d), lambda i: (jnp.maximum(i - nx, 0), 0)),
                  pl.BlockSpec((1, d), lambda i: (0, 0)),
                  pl.BlockSpec((2, d), lambda i: (0, 0)),
                  pl.BlockSpec((2, d), lambda i: (0, 0))],
        out_specs=pl.BlockSpec((tm, d), lambda i: (i, 0)),
        out_shape=jax.ShapeDtypeStruct((s + c, d), BF16),
        compiler_params=_params(("arbitrary",), tm * d * 10),
        name="normmod_xc",
    )(x, ctx, g, mod_x, mod_c)


def _mm_kernel(a_ref, w_ref, o_ref):
    o_ref[...] = jnp.dot(a_ref[...], w_ref[...], preferred_element_type=F32)


def _matmul(a, w, tm_pref, tn_pref):
    m, k = a.shape
    n = w.shape[1]
    tm = _pick(m, tm_pref, BF16_SUBLANES)
    tn = _pick(n, tn_pref, LANES)
    return pl.pallas_call(
        _mm_kernel,
        grid=(m // tm, n // tn),
        in_specs=[pl.BlockSpec((tm, k), lambda i, j: (i, 0)),
                  pl.BlockSpec((k, tn), lambda i, j: (0, j))],
        out_specs=pl.BlockSpec((tm, tn), lambda i, j: (i, j)),
        out_shape=jax.ShapeDtypeStruct((m, n), F32),
        compiler_params=_params(("arbitrary", "arbitrary"),
                                tm * k * 2 + k * tn * 2 + tm * tn * 4, scratch_bytes=tm * tn * 4),
        name="in_proj",
    )(a, w)


def _rope_tables(s, c):
    pos = jnp.arange(s, dtype=jnp.int32)
    r = (pos // GRID_W).astype(F32)[:, None]
    col = (pos % GRID_W).astype(F32)[:, None]
    lane = jnp.arange(LANES, dtype=jnp.int32)

    def table(head_dim):
        axis_dim = head_dim // 2
        pair = axis_dim // 2
        within = lane % head_dim
        f = (within % pair).astype(F32)
        freq = ROPE_BASE ** (-(2.0 * f) / axis_dim)
        use_row = (within // axis_dim) == 0
        ang = jnp.where(use_row[None, :], r * freq[None, :], col * freq[None, :])
        first = (within % axis_dim) < pair
        cos = jnp.cos(ang)
        sin = jnp.where(first[None, :], -jnp.sin(ang), jnp.sin(ang))
        cos = jnp.concatenate([cos, jnp.ones((c, LANES), F32)], axis=0)
        sin = jnp.concatenate([sin, jnp.zeros((c, LANES), F32)], axis=0)
        return cos, sin

    cd, sd = table(DA_QK_DIM)
    cw, sw = table(WA_HEAD_DIM)
    return jnp.stack([cd, sd, cw, sw], axis=0)


def _headprep_kernel(pr_ref, tab_ref, gain_ref, o_ref, *, n_qk_da, n_v_da, n_q_wa, n_k_wa, n_v_wa):
    tm = pr_ref.shape[0]
    lane = lax.broadcasted_iota(jnp.int32, (tm, LANES), 1)
    lo = lane < DA_QK_DIM
    first_da = (lane % (DA_QK_DIM // 2)) < (DA_QK_DIM // 4)
    first_wa = (lane % (WA_HEAD_DIM // 2)) < (WA_HEAD_DIM // 4)
    cos_da, sin_da = tab_ref[0], tab_ref[1]
    cos_wa, sin_wa = tab_ref[2], tab_ref[3]

    def rope(y, cos, sin, first, shift):
        partner = jnp.where(first, pltpu.roll(y, LANES - shift, 1), pltpu.roll(y, shift, 1))
        return y * cos + partner * sin

    def da_group(gidx, gain):
        x = pr_ref[:, gidx * LANES:(gidx + 1) * LANES]
        x2 = x * x
        s_lo = jnp.sum(jnp.where(lo, x2, 0.0), axis=-1, keepdims=True)
        s_hi = jnp.sum(jnp.where(lo, 0.0, x2), axis=-1, keepdims=True)
        ms = jnp.where(lo, s_lo, s_hi) * (1.0 / DA_QK_DIM)
        y = x * lax.rsqrt(ms + EPS) * gain
        y = rope(y, cos_da, sin_da, first_da, DA_QK_DIM // 4)
        o_ref[:, gidx * LANES:(gidx + 1) * LANES] = y.astype(BF16)

    def wa_group(gidx, gain):
        x = pr_ref[:, gidx * LANES:(gidx + 1) * LANES]
        ms = jnp.mean(x * x, axis=-1, keepdims=True)
        y = x * lax.rsqrt(ms + EPS) * gain
        y = rope(y, cos_wa, sin_wa, first_wa, WA_HEAD_DIM // 4)
        o_ref[:, gidx * LANES:(gidx + 1) * LANES] = y.astype(BF16)

    def copy_group(gidx):
        o_ref[:, gidx * LANES:(gidx + 1) * LANES] = pr_ref[:, gidx * LANES:(gidx + 1) * LANES].astype(BF16)

    g = 0
    for _ in range(n_qk_da):
        da_group(g, gain_ref[0:1, :]); g += 1
    for _ in range(n_qk_da):
        da_group(g, gain_ref[1:2, :]); g += 1
    for _ in range(n_v_da):
        copy_group(g); g += 1
    for _ in range(n_q_wa):
        wa_group(g, gain_ref[2:3, :]); g += 1
    for _ in range(n_k_wa):
        wa_group(g, gain_ref[3:4, :]); g += 1
    for _ in range(n_v_wa):
        copy_group(g); g += 1


def _headprep(pr, tabs, gains, da_heads, wa_q_heads, tm_pref=128):
    rows = pr.shape[0]
    wa_kv_heads = wa_q_heads // WA_GROUP
    groups = dict(n_qk_da=da_heads, n_v_da=da_heads, n_q_wa=wa_q_heads,
                  n_k_wa=wa_kv_heads, n_v_wa=wa_kv_heads)
    width = (3 * da_heads + wa_q_heads + 2 * wa_kv_heads) * LANES
    tm = _pick(rows, tm_pref, BF16_SUBLANES)
    return pl.pallas_call(
        functools.partial(_headprep_kernel, **groups),
        grid=(rows // tm,),
        in_specs=[pl.BlockSpec((tm, width), lambda i: (i, 0)),
                  pl.BlockSpec((4, tm, LANES), lambda i: (0, i, 0)),
                  pl.BlockSpec((4, LANES), lambda i: (0, 0))],
        out_specs=pl.BlockSpec((tm, width), lambda i: (i, 0)),
        out_shape=jax.ShapeDtypeStruct((rows, width), BF16),
        compiler_params=_params(("arbitrary",), tm * width * 6 + 4 * tm * LANES * 4),
        name="headprep",
    )(pr, tabs, gains)


def _nt_dot(a, b):
    return lax.dot_general(a, b, (((1,), (1,)), ((), ())), preferred_element_type=F32)


STEPS_PER_TRIP = 4
LOG2_E = math.log2(math.e)


def _diffattn_kernel(lam_ref, q_ref, k_ref, v_ref, g_ref, o_ref, s_ref, p_ref, *, tk, rb):
    tq = q_ref.shape[0]
    nk = k_ref.shape[0] // tk
    q = q_ref[...]
    lane = lax.broadcasted_iota(jnp.int32, (tq, LANES), 1)
    zero = jnp.zeros_like(q)
    qs = (jnp.where(lane < DA_QK_DIM, q, zero), jnp.where(lane < DA_QK_DIM, zero, q))
    ones = jnp.ones((tk, LANES), BF16)

    def scores(c, slot):
        k = k_ref[pl.ds(pl.multiple_of(c * tk, tk), tk), :]
        for h in range(2):
            s_ref[slot, h] = _nt_dot(qs[h], k)

    def softmax(slot, ms):
        new_ms, alphas = [], []
        for h in range(2):
            m_blocks, a_blocks = [], []
            for r in range(0, tq, rb):
                s = s_ref[slot, h, r:r + rb, :]
                m_old = ms[h][r:r + rb]
                m_new = jnp.maximum(m_old, jnp.max(s, axis=-1, keepdims=True))
                p_ref[slot, h, r:r + rb, :] = jnp.exp2(s - m_new).astype(BF16)
                m_blocks.append(m_new)
                a_blocks.append(jnp.exp2(m_old - m_new))
            new_ms.append(jnp.concatenate(m_blocks, axis=0))
            alphas.append(jnp.concatenate(a_blocks, axis=0))
        return tuple(new_ms), tuple(alphas)

    def accumulate(c, slot, alphas, accs):
        v = v_ref[pl.ds(pl.multiple_of(c * tk, tk), tk), :]
        v1 = jnp.concatenate([v, ones], axis=1)
        return tuple(alphas[h] * accs[h] + jnp.dot(p_ref[slot, h], v1, preferred_element_type=F32)
                     for h in range(2))

    def step(c, slot, carry, with_softmax=True, with_scores=True):
        ms, alphas, accs = carry
        accs = accumulate(c, slot, alphas, accs)
        if with_softmax:
            ms, alphas = softmax(1 - slot, ms)
        if with_scores:
            scores(c + 2, slot)
        return ms, alphas, accs

    m0 = jnp.full((tq, 1), NEG_INF, F32)
    a0 = jnp.zeros((tq, 2 * DA_V_DIM), F32)
    scores(0, 0)
    scores(1, 1)
    ms, alphas = softmax(0, (m0, m0))
    carry = (ms, alphas, (a0, a0))

    def group(i, carry):
        for t in range(STEPS_PER_TRIP):
            carry = step(STEPS_PER_TRIP * i + t, t % 2, carry)
        return carry

    n_full = nk - 2
    carry = lax.fori_loop(0, n_full // STEPS_PER_TRIP, group, carry)
    c = STEPS_PER_TRIP * (n_full // STEPS_PER_TRIP)
    while c < n_full:
        carry = step(c, c % 2, carry)
        c += 1
    carry = step(c, c % 2, carry, with_scores=False)
    _, _, (a1, a2) = step(c + 1, (c + 1) % 2, carry, with_softmax=False, with_scores=False)

    lam = (jnp.exp(jnp.sum(lam_ref[0:1, :] * lam_ref[1:2, :], axis=-1, keepdims=True))
           - jnp.exp(jnp.sum(lam_ref[2:3, :] * lam_ref[3:4, :], axis=-1, keepdims=True))
           + LAM_INIT)
    o = (a1[:, :DA_V_DIM] / a1[:, DA_V_DIM:]) - lam * (a2[:, :DA_V_DIM] / a2[:, DA_V_DIM:])
    ms = jnp.mean(o * o, axis=-1, keepdims=True)
    o = o * lax.rsqrt(ms + EPS) * g_ref[...] * (1.0 - LAM_INIT)
    o_ref[...] = o.astype(BF16)


def _diffattn(qkv, lam_vecs, subln_g, s, da_heads, tq_pref=512, tk_pref=640):
    rows = qkv.shape[0]
    tq = _pick(s, tq_pref, BF16_SUBLANES)
    tk = _pick(rows, tk_pref, LANES)
    kb, vb = da_heads, 2 * da_heads
    assert rows // tk >= 2
    return pl.pallas_call(
        functools.partial(_diffattn_kernel, tk=tk, rb=BF16_SUBLANES),
        grid=(da_heads, s // tq),
        in_specs=[pl.BlockSpec((4, DA_QK_DIM), lambda h, i: (0, 0)),
                  pl.BlockSpec((tq, LANES), lambda h, i: (i, h)),
                  pl.BlockSpec((rows, LANES), lambda h, i: (0, kb + h)),
                  pl.BlockSpec((rows, LANES), lambda h, i: (0, vb + h)),
                  pl.BlockSpec((1, DA_V_DIM), lambda h, i: (0, 0))],
        out_specs=pl.BlockSpec((tq, DA_V_DIM), lambda h, i: (i, h)),
        out_shape=jax.ShapeDtypeStruct((s, da_heads * DA_V_DIM), BF16),
        scratch_shapes=[pltpu.VMEM((2, 2, tq, tk), F32), pltpu.VMEM((2, 2, tq, tk), BF16)],
        compiler_params=_params(("arbitrary", "arbitrary"),
                                2 * rows * LANES * 2 + 2 * tq * LANES * 2,
                                scratch_bytes=4 * tq * tk * 6 + 4 * tq * tk * 4),
        name="diff_attn",
    )(lam_vecs, qkv, qkv, qkv, subln_g.reshape(1, DA_V_DIM))


def _winattn_kernel(sink_ref, q_ref, k_ref, v_ref, o_ref, *, s, c):
    g = pl.program_id(0)
    i = pl.program_id(1)
    tq = q_ref.shape[0]
    band = tq + 2 * WINDOW
    q0 = i * tq
    start = jnp.clip(q0 - WINDOW, 0, s - band)
    start = pl.multiple_of(start, WINDOW)
    kb = k_ref[pl.ds(start, band), :]
    vb = v_ref[pl.ds(start, band), :]
    kc = k_ref[s:s + c, :]
    vc = v_ref[s:s + c, :]
    q_pos = q0 + lax.broadcasted_iota(jnp.int32, (tq, band), 0)
    k_pos = start + lax.broadcasted_iota(jnp.int32, (tq, band), 1)
    valid = jnp.abs(q_pos - k_pos) <= WINDOW
    for r in range(WA_GROUP):
        q = q_ref[:, r * WA_HEAD_DIM:(r + 1) * WA_HEAD_DIM]
        sb = jnp.where(valid, _nt_dot(q, kb) * WA_SCALE, NEG_INF)
        sc = _nt_dot(q, kc) * WA_SCALE
        sink = sink_ref[g * WA_GROUP + r]
        m = jnp.maximum(jnp.max(sb, axis=-1, keepdims=True), jnp.max(sc, axis=-1, keepdims=True))
        m = jnp.maximum(m, sink)
        pb = jnp.exp(sb - m)
        pc = jnp.exp(sc - m)
        l = (jnp.sum(pb, axis=-1, keepdims=True) + jnp.sum(pc, axis=-1, keepdims=True)
             + jnp.exp(sink - m))
        inv = 1.0 / l
        o = (jnp.dot((pb * inv).astype(BF16), vb, preferred_element_type=F32)
             + jnp.dot((pc * inv).astype(BF16), vc, preferred_element_type=F32))
        o_ref[:, r * WA_HEAD_DIM:(r + 1) * WA_HEAD_DIM] = o.astype(BF16)


def _winattn(qkv, sink, s, c, da_heads, wa_q_heads, tq_pref=512):
    rows = qkv.shape[0]
    kvh = wa_q_heads // WA_GROUP
    tq = _pick(s, tq_pref, WINDOW)
    qb = 3 * da_heads // WA_GROUP
    kb = 3 * da_heads + wa_q_heads
    vb = kb + kvh
    gw = WA_GROUP * WA_HEAD_DIM
    return pl.pallas_call(
        functools.partial(_winattn_kernel, s=s, c=c),
        grid=(kvh, s // tq),
        in_specs=[pl.BlockSpec(memory_space=pltpu.SMEM),
                  pl.BlockSpec((tq, gw), lambda g, i: (i, qb + g)),
                  pl.BlockSpec((rows, LANES), lambda g, i: (0, kb + g)),
                  pl.BlockSpec((rows, LANES), lambda g, i: (0, vb + g))],
        out_specs=pl.BlockSpec((tq, gw), lambda g, i: (i, g)),
        out_shape=jax.ShapeDtypeStruct((s, wa_q_heads * WA_HEAD_DIM), BF16),
        compiler_params=_params(("arbitrary", "arbitrary"),
                                2 * rows * LANES * 2 + 2 * tq * gw * 2,
                                scratch_bytes=8 * tq * (tq + 2 * WINDOW + c) * 4),
        name="win_attn",
    )(sink, qkv, qkv, qkv)


def _merge_kernel(yd_ref, yw_ref, wd_ref, ww_ref, gd_ref, gw_ref, bd_ref, bw_ref, o_ref):
    pd = jnp.dot(yd_ref[...], wd_ref[...], preferred_element_type=F32)
    pw = jnp.dot(yw_ref[...], ww_ref[...], preferred_element_type=F32)
    gd = _sigmoid(gd_ref[...] + bd_ref[...])
    gw = _sigmoid(gw_ref[...] + bw_ref[...])
    o_ref[...] = (gd * pd + gw * pw).astype(BF16)


def _merge(y_da, y_wa, w_o_da, w_o_wa, pr, gate_off, b_gate, tm_pref=512, tn_pref=1024):
    s, kd = y_da.shape
    kw = y_wa.shape[1]
    d = w_o_da.shape[1]
    tm = _pick(s, tm_pref, BF16_SUBLANES)
    tn = _pick(math.gcd(d, gate_off), tn_pref, LANES)
    od, ow = gate_off // tn, (gate_off + d) // tn
    nb = d // tn
    bg = b_gate.reshape(1, N_BRANCH * d)
    return pl.pallas_call(
        _merge_kernel,
        grid=(s // tm, nb),
        in_specs=[pl.BlockSpec((tm, kd), lambda i, j: (i, 0)),
                  pl.BlockSpec((tm, kw), lambda i, j: (i, 0)),
                  pl.BlockSpec((kd, tn), lambda i, j: (0, j)),
                  pl.BlockSpec((kw, tn), lambda i, j: (0, j)),
                  pl.BlockSpec((tm, tn), lambda i, j: (i, od + j)),
                  pl.BlockSpec((tm, tn), lambda i, j: (i, ow + j)),
                  pl.BlockSpec((1, tn), lambda i, j: (0, j)),
                  pl.BlockSpec((1, tn), lambda i, j: (0, nb + j))],
        out_specs=pl.BlockSpec((tm, tn), lambda i, j: (i, j)),
        out_shape=jax.ShapeDtypeStruct((s, d), BF16),
        compiler_params=_params(("arbitrary", "arbitrary"),
                                tm * (kd + kw) * 2 + (kd + kw) * tn * 2 + 2 * tm * tn * 4 + tm * tn * 2,
                                scratch_bytes=4 * tm * tn * 4),
        name="merge",
    )(y_da, y_wa, w_o_da, w_o_wa, pr, pr, bg, bg)


def _mm_res_kernel(a_ref, w_ref, x_ref, g_ref, o_ref, acc_ref):
    kk = pl.program_id(2)

    @pl.when(kk == 0)
    def _():
        acc_ref[...] = jnp.zeros_like(acc_ref)

    acc_ref[...] += jnp.dot(a_ref[...], w_ref[...], preferred_element_type=F32)

    @pl.when(kk == pl.num_programs(2) - 1)
    def _():
        o_ref[...] = x_ref[...] + g_ref[...] * acc_ref[...]


def _matmul_residual(a, w, x, gate, tm_pref, tn_pref, tk_pref, name):
    m, k = a.shape
    n = w.shape[1]
    tm = _pick(m, tm_pref, BF16_SUBLANES)
    tn = _pick(n, tn_pref, LANES)
    tk = _pick(k, tk_pref, LANES)
    return pl.pallas_call(
        _mm_res_kernel,
        grid=(m // tm, n // tn, k // tk),
        in_specs=[pl.BlockSpec((tm, tk), lambda i, j, kk: (i, kk)),
                  pl.BlockSpec((tk, tn), lambda i, j, kk: (kk, j)),
                  pl.BlockSpec((tm, tn), lambda i, j, kk: (i, j)),
                  pl.BlockSpec((1, tn), lambda i, j, kk: (0, j))],
        out_specs=pl.BlockSpec((tm, tn), lambda i, j, kk: (i, j)),
        out_shape=jax.ShapeDtypeStruct((m, n), F32),
        scratch_shapes=[pltpu.VMEM((tm, tn), F32)],
        compiler_params=_params(("arbitrary", "arbitrary", "arbitrary"),
                                tm * tk * 2 + tk * tn * 2 + 2 * tm * tn * 4,
                                scratch_bytes=2 * tm * tn * 4),
        name=name,
    )(a, w, x, gate)


HALO = BF16_SUBLANES


def _ffn_up_kernel(h_ref, hp_ref, hn_ref, wa_ref, wu_ref, cw_ref, cb_ref, o_ref, lhs_ref):
    i = pl.program_id(0)
    j = pl.program_id(1)
    tm = h_ref.shape[0]

    @pl.when(j == 0)
    def _():
        prev = jnp.where(i > 0, hp_ref[...], jnp.zeros_like(hp_ref))
        nxt = jnp.where(i < pl.num_programs(0) - 1, hn_ref[...], jnp.zeros_like(hn_ref))
        lhs_ref[0:HALO, :] = prev
        lhs_ref[HALO:HALO + tm, :] = h_ref[...]
        lhs_ref[HALO + tm:, :] = nxt

    a = jnp.dot(lhs_ref[...], wa_ref[...], preferred_element_type=F32)
    u = jnp.dot(lhs_ref[HALO:HALO + tm, :], wu_ref[...], preferred_element_type=F32)
    rows = tm + 2 * HALO
    a_prev = pltpu.roll(a, 1, 0)[HALO:HALO + tm, :]
    a_next = pltpu.roll(a, rows - 1, 0)[HALO:HALO + tm, :]
    a_mid = a[HALO:HALO + tm, :]
    conv = cb_ref[...] + a_prev * cw_ref[0:1, :] + a_mid * cw_ref[1:2, :] + a_next * cw_ref[2:3, :]
    o_ref[...] = (conv * _sigmoid(conv) * u).astype(BF16)


def _ffn_up(h, w_up, conv_w, conv_b, ffp, tm_pref=1024, tn_pref=512):
    s, d = h.shape
    tm = _pick(s, tm_pref, HALO)
    tn = _pick(ffp, tn_pref, LANES)
    nf = ffp // tn
    per = tm // HALO
    last = s // HALO - 1
    return pl.pallas_call(
        _ffn_up_kernel,
        grid=(s // tm, nf),
        in_specs=[pl.BlockSpec((tm, d), lambda i, j: (i, 0)),
                  pl.BlockSpec((HALO, d), lambda i, j: (jnp.maximum(i * per - 1, 0), 0)),
                  pl.BlockSpec((HALO, d), lambda i, j: (jnp.minimum((i + 1) * per, last), 0)),
                  pl.BlockSpec((d, tn), lambda i, j: (0, j)),
                  pl.BlockSpec((d, tn), lambda i, j: (0, nf + j)),
                  pl.BlockSpec((CONV_W, tn), lambda i, j: (0, j)),
                  pl.BlockSpec((1, tn), lambda i, j: (0, j))],
        out_specs=pl.BlockSpec((tm, tn), lambda i, j: (i, j)),
        out_shape=jax.ShapeDtypeStruct((s, ffp), BF16),
        scratch_shapes=[pltpu.VMEM((tm + 2 * HALO, d), BF16)],
        compiler_params=_params(("arbitrary", "arbitrary"),
                                (tm + 2 * HALO) * d * 2 + 2 * d * tn * 2 + tm * tn * 2,
                                scratch_bytes=(tm + 2 * HALO) * d * 2 + 6 * (tm + 2 * HALO) * tn * 4),
        name="ffn_up",
    )(h, h, h, w_up, w_up, conv_w, conv_b)


def kernel(x, c, ctx, c_ctx, w_ada, b_ada, attn_norm_g, w_in, b_gate, da_qn_g, da_kn_g, da_lambda_q1, da_lambda_k1, da_lambda_q2, da_lambda_k2, da_subln_g, wa_qn_g, wa_kn_g, wa_sink, w_o_da, w_o_wa, w_out, ffn_norm_g, w_ffn_up, ffn_conv_w, ffn_conv_b, w_ffn_down):
    b, s, d = x.shape
    cl = ctx.shape[1]
    assert b == 1 and w_in.shape[0] == 1, "one batch element, one layer"
    da_heads = w_o_da.shape[1] // DA_V_DIM
    wa_q_heads = w_o_wa.shape[1] // WA_HEAD_DIM
    wa_kv_heads = wa_q_heads // WA_GROUP
    d_ff = ffn_conv_b.shape[1]
    gate_off = (3 * da_heads + wa_q_heads + 2 * wa_kv_heads) * LANES
    assert w_in.shape[2] == gate_off + N_BRANCH * d

    x2 = x.reshape(s, d)
    ctx2 = ctx.reshape(cl, d)

    ffp = -(-d_ff // 1024) * 1024 if d_ff > 1024 else d_ff
    padf = ffp - d_ff
    w_in_b = w_in[0].astype(BF16)
    w_o_da_b = w_o_da[0].astype(BF16)
    w_o_wa_b = w_o_wa[0].astype(BF16)
    w_out_b = w_out[0].astype(BF16)
    wu = w_ffn_up[0]
    w_up_b = jnp.concatenate(
        [jnp.pad(wu[:, :d_ff], ((0, 0), (0, padf))), jnp.pad(wu[:, d_ff:], ((0, 0), (0, padf)))],
        axis=1).astype(BF16)
    w_down_b = jnp.pad(w_ffn_down[0], ((0, padf), (0, 0))).astype(BF16)
    conv_w = jnp.pad(ffn_conv_w[0], ((0, 0), (0, padf)))
    conv_b = jnp.pad(ffn_conv_b[0], (0, padf)).reshape(1, ffp)

    cc = jnp.concatenate([c, c_ctx[None, :], jnp.zeros((6, d), F32)], axis=0)
    mod = _adaln(cc, w_ada[0], b_ada[0])
    sh1, sc1, g1 = mod[0:1, 0:d], mod[0:1, d:2 * d], mod[0:1, 2 * d:3 * d]
    sh2, sc2, g2 = mod[0:1, 3 * d:4 * d], mod[0:1, 4 * d:5 * d], mod[0:1, 5 * d:6 * d]
    mod_x1 = jnp.concatenate([sh1, sc1], axis=0)
    mod_c1 = jnp.concatenate([mod[1:2, 0:d], mod[1:2, d:2 * d]], axis=0)
    mod_x2 = jnp.concatenate([sh2, sc2], axis=0)

    h = _normmod(x2, attn_norm_g[0], mod_x1, ctx2, mod_c1)
    pr = _matmul(h, w_in_b, 1280, 512)
    tabs = _rope_tables(s, cl)
    gains = jnp.stack([jnp.tile(da_qn_g[0] * (DA_SCALE * LOG2_E), 2), jnp.tile(da_kn_g[0], 2),
                       wa_qn_g[0], wa_kn_g[0]], axis=0)
    qkv = _headprep(pr, tabs, gains, da_heads, wa_q_heads)
    lam_vecs = jnp.stack([da_lambda_q1[0], da_lambda_k1[0], da_lambda_q2[0], da_lambda_k2[0]], axis=0)
    y_da = _diffattn(qkv, lam_vecs, da_subln_g[0], s, da_heads)
    y_wa = _winattn(qkv, wa_sink[0], s, cl, da_heads, wa_q_heads)
    u = _merge(y_da, y_wa, w_o_da_b, w_o_wa_b, pr, gate_off, b_gate[0])
    x1 = _matmul_residual(u, w_out_b, x2, g1, 1024, 1024, d, "out_proj")

    h2 = _normmod(x1, ffn_norm_g[0], mod_x2)
    act = _ffn_up(h2, w_up_b, conv_w, conv_b, ffp)
    out = _matmul_residual(act, w_down_b, x1, g2, 1024, 1024, 2816, "ffn_down")
    return out.reshape(b, s, d)
```

```python
import functools
import math

import jax
import jax.numpy as jnp
from jax import lax
from jax.experimental import pallas as pl
from jax.experimental.pallas import tpu as pltpu

F32 = jnp.float32
BF16 = jnp.bfloat16

GRID_W = 64
DA_QK_DIM = 64
DA_V_DIM = 2 * DA_QK_DIM
WA_HEAD_DIM = 128
WA_GROUP = 4
WINDOW = 128
N_BRANCH = 2
CONV_W = 3
ROPE_BASE = 10000.0
EPS = 1e-6
DA_SCALE = DA_QK_DIM ** -0.5
WA_SCALE = WA_HEAD_DIM ** -0.5
NEG_INF = -1e30
LAM_INIT = 0.8 - 0.6 * math.exp(-0.3 * 0)

LANES = 128
BF16_SUBLANES = 16
V7X_VMEM_CAP_BYTES = 58 * 1024 * 1024
VMEM_SLACK_BYTES = 6 * 1024 * 1024


def _pick(dim, pref, unit):
    best = None
    t = unit
    while t <= min(dim, pref):
        if dim % t == 0:
            best = t
        t += unit
    if best is None:
        raise ValueError(f"no tile for dim={dim} unit={unit}")
    return best


def _params(semantics, block_bytes, scratch_bytes=0):
    est = 2 * block_bytes + scratch_bytes + VMEM_SLACK_BYTES
    return pltpu.CompilerParams(
        dimension_semantics=semantics,
        vmem_limit_bytes=min(max(est, 16 * 1024 * 1024), V7X_VMEM_CAP_BYTES))


def _sigmoid(x):
    return 1.0 / (1.0 + jnp.exp(-x))


def _adaln_kernel(c_ref, w_ref, b_ref, o_ref):
    a = c_ref[...]
    a = a * _sigmoid(a)
    o_ref[...] = jnp.dot(a.astype(BF16), w_ref[...].astype(BF16),
                         preferred_element_type=F32) + b_ref[...]


def _adaln(cc, w, b):
    rows, d = cc.shape
    n = w.shape[1]
    tn = _pick(n, 512, LANES)
    return pl.pallas_call(
        _adaln_kernel,
        grid=(n // tn,),
        in_specs=[pl.BlockSpec((rows, d), lambda j: (0, 0)),
                  pl.BlockSpec((d, tn), lambda j: (0, j)),
                  pl.BlockSpec((1, tn), lambda j: (0, j))],
        out_specs=pl.BlockSpec((rows, tn), lambda j: (0, j)),
        out_shape=jax.ShapeDtypeStruct((rows, n), F32),
        compiler_params=_params(("arbitrary",), d * tn * 4 + d * tn * 2),
        name="adaln",
    )(cc, w, b.reshape(1, n))


def _normmod_rows(x, g, mod):
    ms = jnp.mean(x * x, axis=-1, keepdims=True)
    y = x * lax.rsqrt(ms + EPS) * g
    return (y * (1.0 + mod[1:2, :]) + mod[0:1, :]).astype(BF16)


def _normmod2_kernel(x_ref, c_ref, g_ref, mx_ref, mc_ref, o_ref, *, nx):
    i = pl.program_id(0)

    @pl.when(i < nx)
    def _():
        o_ref[...] = _normmod_rows(x_ref[...], g_ref[...], mx_ref[...])

    @pl.when(i >= nx)
    def _():
        o_ref[...] = _normmod_rows(c_ref[...], g_ref[...], mc_ref[...])


def _normmod1_kernel(x_ref, g_ref, mx_ref, o_ref):
    o_ref[...] = _normmod_rows(x_ref[...], g_ref[...], mx_ref[...])


def _normmod(x, g, mod_x, ctx=None, mod_c=None):
    s, d = x.shape
    g = g.reshape(1, d)
    if ctx is None:
        tm = _pick(s, 256, BF16_SUBLANES)
        return pl.pallas_call(
            _normmod1_kernel,
            grid=(s // tm,),
            in_specs=[pl.BlockSpec((tm, d), lambda i: (i, 0)),
                      pl.BlockSpec((1, d), lambda i: (0, 0)),
                      pl.BlockSpec((2, d), lambda i: (0, 0))],
            out_specs=pl.BlockSpec((tm, d), lambda i: (i, 0)),
            out_shape=jax.ShapeDtypeStruct((s, d), BF16),
            compiler_params=_params(("arbitrary",), tm * d * 6),
            name="normmod",
        )(x, g, mod_x)
    c = ctx.shape[0]
    tm = _pick(math.gcd(s, c), 256, BF16_SUBLANES)
    nx, nc = s // tm, c // tm
    return pl.pallas_call(
        functools.partial(_normmod2_kernel, nx=nx),
        grid=(nx + nc,),
        in_specs=[pl.BlockSpec((tm, d), lambda i: (jnp.minimum(i, nx - 1), 0)),
                  pl.BlockSpec((tm, d), lambda i: (jnp.maximum(i - nx, 0), 0)),
                  pl.BlockSpec((1, d), lambda i: (0, 0)),
                  pl.BlockSpec((2, d), lambda i: (0, 0)),
                  pl.BlockSpec((2, d), lambda i: (0, 0))],
        out_specs=pl.BlockSpec((tm, d), lambda i: (i, 0)),
        out_shape=jax.ShapeDtypeStruct((s + c, d), BF16),
        compiler_params=_params(("arbitrary",), tm * d * 10),
        name="normmod_xc",
    )(x, ctx, g, mod_x, mod_c)


def _mm_kernel(a_ref, w_ref, o_ref):
    o_ref[...] = jnp.dot(a_ref[...], w_ref[...], preferred_element_type=F32)


def _matmul(a, w, tm_pref, tn_pref):
    m, k = a.shape
    n = w.shape[1]
    tm = _pick(m, tm_pref, BF16_SUBLANES)
    tn = _pick(n, tn_pref, LANES)
    return pl.pallas_call(
        _mm_kernel,
        grid=(m // tm, n // tn),
        in_specs=[pl.BlockSpec((tm, k), lambda i, j: (i, 0)),
                  pl.BlockSpec((k, tn), lambda i, j: (0, j))],
        out_specs=pl.BlockSpec((tm, tn), lambda i, j: (i, j)),
        out_shape=jax.ShapeDtypeStruct((m, n), F32),
        compiler_params=_params(("arbitrary", "arbitrary"),
                                tm * k * 2 + k * tn * 2 + tm * tn * 4, scratch_bytes=tm * tn * 4),
        name="in_proj",
    )(a, w)


def _rope_tables(s, c):
    pos = jnp.arange(s, dtype=jnp.int32)
    r = (pos // GRID_W).astype(F32)[:, None]
    col = (pos % GRID_W).astype(F32)[:, None]
    lane = jnp.arange(LANES, dtype=jnp.int32)

    def table(head_dim):
        axis_dim = head_dim // 2
        pair = axis_dim // 2
        within = lane % head_dim
        f = (within % pair).astype(F32)
        freq = ROPE_BASE ** (-(2.0 * f) / axis_dim)
        use_row = (within // axis_dim) == 0
        ang = jnp.where(use_row[None, :], r * freq[None, :], col * freq[None, :])
        first = (within % axis_dim) < pair
        cos = jnp.cos(ang)
        sin = jnp.where(first[None, :], -jnp.sin(ang), jnp.sin(ang))
        cos = jnp.concatenate([cos, jnp.ones((c, LANES), F32)], axis=0)
        sin = jnp.concatenate([sin, jnp.zeros((c, LANES), F32)], axis=0)
        return cos, sin

    cd, sd = table(DA_QK_DIM)
    cw, sw = table(WA_HEAD_DIM)
    return jnp.stack([cd, sd, cw, sw], axis=0)


def _headprep_kernel(pr_ref, tab_ref, gain_ref, ones_da_ref, ones_wa_ref, o_ref,
                     *, n_qk_da, n_v_da, n_q_wa, n_k_wa, n_v_wa):
    tm = pr_ref.shape[0]
    lane = lax.broadcasted_iota(jnp.int32, (tm, LANES), 1)
    first_da = (lane % (DA_QK_DIM // 2)) < (DA_QK_DIM // 4)
    first_wa = (lane % (WA_HEAD_DIM // 2)) < (WA_HEAD_DIM // 4)
    cos_da, sin_da = tab_ref[0], tab_ref[1]
    cos_wa, sin_wa = tab_ref[2], tab_ref[3]

    def rope(y, cos, sin, first, shift):
        partner = jnp.where(first, pltpu.roll(y, LANES - shift, 1), pltpu.roll(y, shift, 1))
        return y * cos + partner * sin

    def head_sum_sq(x, ones_ref):
        x2 = x * x
        hi = x2.astype(BF16)
        mid = (x2 - hi.astype(F32)).astype(BF16)
        return jnp.dot(jnp.concatenate([hi, mid], axis=1), ones_ref[...], preferred_element_type=F32)

    def da_group(gidx, gain):
        x = pr_ref[:, gidx * LANES:(gidx + 1) * LANES]
        ms = head_sum_sq(x, ones_da_ref) * (1.0 / DA_QK_DIM)
        y = x * lax.rsqrt(ms + EPS) * gain
        y = rope(y, cos_da, sin_da, first_da, DA_QK_DIM // 4)
        o_ref[:, gidx * LANES:(gidx + 1) * LANES] = y.astype(BF16)

    def wa_group(gidx, gain):
        x = pr_ref[:, gidx * LANES:(gidx + 1) * LANES]
        ms = head_sum_sq(x, ones_wa_ref) * (1.0 / WA_HEAD_DIM)
        y = x * lax.rsqrt(ms + EPS) * gain
        y = rope(y, cos_wa, sin_wa, first_wa, WA_HEAD_DIM // 4)
        o_ref[:, gidx * LANES:(gidx + 1) * LANES] = y.astype(BF16)

    def copy_group(gidx):
        o_ref[:, gidx * LANES:(gidx + 1) * LANES] = pr_ref[:, gidx * LANES:(gidx + 1) * LANES].astype(BF16)

    g = 0
    for _ in range(n_qk_da):
        da_group(g, gain_ref[0:1, :]); g += 1
    for _ in range(n_qk_da):
        da_group(g, gain_ref[1:2, :]); g += 1
    for _ in range(n_v_da):
        copy_group(g); g += 1
    for _ in range(n_q_wa):
        wa_group(g, gain_ref[2:3, :]); g += 1
    for _ in range(n_k_wa):
        wa_group(g, gain_ref[3:4, :]); g += 1
    for _ in range(n_v_wa):
        copy_group(g); g += 1


def _headprep(pr, tabs, gains, da_heads, wa_q_heads, tm_pref=128):
    rows = pr.shape[0]
    wa_kv_heads = wa_q_heads // WA_GROUP
    groups = dict(n_qk_da=da_heads, n_v_da=da_heads, n_q_wa=wa_q_heads,
                  n_k_wa=wa_kv_heads, n_v_wa=wa_kv_heads)
    width = (3 * da_heads + wa_q_heads + 2 * wa_kv_heads) * LANES
    tm = _pick(rows, tm_pref, BF16_SUBLANES)
    lane = jnp.arange(LANES)
    same_da_head = (lane[:, None] // DA_QK_DIM) == (lane[None, :] // DA_QK_DIM)
    ones_da = jnp.tile(same_da_head.astype(BF16), (2, 1))
    ones_wa = jnp.ones((2 * LANES, LANES), BF16)
    return pl.pallas_call(
        functools.partial(_headprep_kernel, **groups),
        grid=(rows // tm,),
        in_specs=[pl.BlockSpec((tm, width), lambda i: (i, 0)),
                  pl.BlockSpec((4, tm, LANES), lambda i: (0, i, 0)),
                  pl.BlockSpec((4, LANES), lambda i: (0, 0)),
                  pl.BlockSpec((2 * LANES, LANES), lambda i: (0, 0)),
                  pl.BlockSpec((2 * LANES, LANES), lambda i: (0, 0))],
        out_specs=pl.BlockSpec((tm, width), lambda i: (i, 0)),
        out_shape=jax.ShapeDtypeStruct((rows, width), BF16),
        compiler_params=_params(("arbitrary",), tm * width * 6 + 4 * tm * LANES * 4),
        name="headprep",
    )(pr, tabs, gains, ones_da, ones_wa)


def _nt_dot(a, b):
    return lax.dot_general(a, b, (((1,), (1,)), ((), ())), preferred_element_type=F32)


STEPS_PER_TRIP = 4
LOG2_E = math.log2(math.e)


def _diffattn_kernel(lam_ref, q_ref, k_ref, v_ref, g_ref, o_ref,
                     s_ref, p_ref, m_ref, al_ref, acc_ref, *, tk, rb):
    tq = q_ref.shape[0]
    nk = k_ref.shape[0] // tk
    q = q_ref[...]
    lane = lax.broadcasted_iota(jnp.int32, (tq, LANES), 1)
    zero = jnp.zeros_like(q)
    qs = (jnp.where(lane < DA_QK_DIM, q, zero), jnp.where(lane < DA_QK_DIM, zero, q))
    ones = jnp.ones((tk, LANES), BF16)

    def scores(c, slot):
        k = k_ref[pl.ds(pl.multiple_of(c * tk, tk), tk), :]
        for h in range(2):
            s_ref[slot, h] = _nt_dot(qs[h], k).astype(BF16)

    def softmax(slot):
        for h in range(2):
            for r in range(0, tq, rb):
                s = s_ref[slot, h, r:r + rb, :]
                m_old = m_ref[h, r:r + rb, :]
                m_new = jnp.maximum(m_old, jnp.max(s, axis=-1, keepdims=True).astype(F32))
                p_ref[slot, h, r:r + rb, :] = jnp.exp2(s - m_new.astype(BF16))
                m_ref[h, r:r + rb, :] = m_new
                al_ref[slot, h, r:r + rb, :] = jnp.exp2(m_old - m_new)

    def accumulate(c, slot):
        v = v_ref[pl.ds(pl.multiple_of(c * tk, tk), tk), :]
        v1 = jnp.concatenate([v, ones], axis=1)
        for h in range(2):
            acc_ref[h] = (al_ref[slot, h] * acc_ref[h]
                          + jnp.dot(p_ref[slot, h], v1, preferred_element_type=F32))

    def step(c, slot, with_softmax=True, with_scores=True):
        accumulate(c, slot)
        if with_softmax:
            softmax(1 - slot)
        if with_scores:
            scores(c + 2, slot)

    m_ref[...] = jnp.full(m_ref.shape, NEG_INF, F32)
    acc_ref[...] = jnp.zeros(acc_ref.shape, F32)
    scores(0, 0)
    scores(1, 1)
    softmax(0)

    def group(i, carry):
        for t in range(STEPS_PER_TRIP):
            step(STEPS_PER_TRIP * i + t, t % 2)
        return carry

    n_full = nk - 2
    lax.fori_loop(0, n_full // STEPS_PER_TRIP, group, 0)
    c = STEPS_PER_TRIP * (n_full // STEPS_PER_TRIP)
    while c < n_full:
        step(c, c % 2)
        c += 1
    step(c, c % 2, with_scores=False)
    step(c + 1, (c + 1) % 2, with_softmax=False, with_scores=False)
    a1, a2 = acc_ref[0], acc_ref[1]

    lam = (jnp.exp(jnp.sum(lam_ref[0:1, :] * lam_ref[1:2, :], axis=-1, keepdims=True))
           - jnp.exp(jnp.sum(lam_ref[2:3, :] * lam_ref[3:4, :], axis=-1, keepdims=True))
           + LAM_INIT)
    o = (a1[:, :DA_V_DIM] / a1[:, DA_V_DIM:]) - lam * (a2[:, :DA_V_DIM] / a2[:, DA_V_DIM:])
    ms = jnp.mean(o * o, axis=-1, keepdims=True)
    o = o * lax.rsqrt(ms + EPS) * g_ref[...] * (1.0 - LAM_INIT)
    o_ref[...] = o.astype(BF16)


def _diffattn(qkv, lam_vecs, subln_g, s, da_heads, tq_pref=256, tk_pref=1280):
    rows = qkv.shape[0]
    tq = _pick(s, tq_pref, BF16_SUBLANES)
    tk = _pick(rows, min(tk_pref, rows // 2), LANES)
    kb, vb = da_heads, 2 * da_heads
    return pl.pallas_call(
        functools.partial(_diffattn_kernel, tk=tk, rb=BF16_SUBLANES),
        grid=(da_heads, s // tq),
        in_specs=[pl.BlockSpec((4, DA_QK_DIM), lambda h, i: (0, 0)),
                  pl.BlockSpec((tq, LANES), lambda h, i: (i, h)),
                  pl.BlockSpec((rows, LANES), lambda h, i: (0, kb + h)),
                  pl.BlockSpec((rows, LANES), lambda h, i: (0, vb + h)),
                  pl.BlockSpec((1, DA_V_DIM), lambda h, i: (0, 0))],
        out_specs=pl.BlockSpec((tq, DA_V_DIM), lambda h, i: (i, h)),
        out_shape=jax.ShapeDtypeStruct((s, da_heads * DA_V_DIM), BF16),
        scratch_shapes=[pltpu.VMEM((2, 2, tq, tk), BF16),
                        pltpu.VMEM((2, 2, tq, tk), BF16),
                        pltpu.VMEM((2, tq, 1), F32),
                        pltpu.VMEM((2, 2, tq, 1), F32),
                        pltpu.VMEM((2, tq, 2 * DA_V_DIM), F32)],
        compiler_params=_params(("arbitrary", "arbitrary"),
                                2 * rows * LANES * 2 + 2 * tq * LANES * 2,
                                scratch_bytes=4 * tq * tk * 6 + 4 * tq * tk * 4),
        name="diff_attn",
    )(lam_vecs, qkv, qkv, qkv, subln_g.reshape(1, DA_V_DIM))


def _winattn_kernel(sink_ref, q_ref, k_ref, v_ref, o_ref, *, s, c):
    g = pl.program_id(0)
    i = pl.program_id(1)
    tq = q_ref.shape[0]
    band = 3 * WINDOW
    rows = WA_GROUP * WINDOW
    kc = k_ref[s:s + c, :]
    vc = v_ref[s:s + c, :]
    row = lax.broadcasted_iota(jnp.int32, (rows, band), 0)
    col = lax.broadcasted_iota(jnp.int32, (rows, band), 1)
    head = lax.broadcasted_iota(jnp.int32, (rows, 1), 0) // WINDOW
    sink = jnp.zeros((rows, 1), F32)
    for r in range(WA_GROUP):
        sink = jnp.where(head == r, sink_ref[g * WA_GROUP + r] * LOG2_E, sink)
    for b in range(tq // WINDOW):
        q0 = i * tq + b * WINDOW
        start = pl.multiple_of(jnp.clip(q0 - WINDOW, 0, s - band), WINDOW)
        kb = k_ref[pl.ds(start, band), :]
        vb = v_ref[pl.ds(start, band), :]
        valid = jnp.abs(q0 + row % WINDOW - (start + col)) <= WINDOW
        q = jnp.concatenate(
            [q_ref[b * WINDOW:(b + 1) * WINDOW, r * WA_HEAD_DIM:(r + 1) * WA_HEAD_DIM]
             for r in range(WA_GROUP)], axis=0)
        sb = jnp.where(valid, _nt_dot(q, kb), NEG_INF)
        sc = _nt_dot(q, kc)
        m = jnp.maximum(jnp.max(sb, axis=-1, keepdims=True), jnp.max(sc, axis=-1, keepdims=True))
        m = jnp.maximum(m, sink)
        pb = jnp.exp2(sb - m)
        pc = jnp.exp2(sc - m)
        l = (jnp.sum(pb, axis=-1, keepdims=True) + jnp.sum(pc, axis=-1, keepdims=True)
             + jnp.exp2(sink - m))
        o = (jnp.dot(pb.astype(BF16), vb, preferred_element_type=F32)
             + jnp.dot(pc.astype(BF16), vc, preferred_element_type=F32)) * (1.0 / l)
        for r in range(WA_GROUP):
            o_ref[b * WINDOW:(b + 1) * WINDOW, r * WA_HEAD_DIM:(r + 1) * WA_HEAD_DIM] = (
                o[r * WINDOW:(r + 1) * WINDOW].astype(BF16))


def _winattn(qkv, sink, s, c, da_heads, wa_q_heads, tq_pref=512):
    rows = qkv.shape[0]
    kvh = wa_q_heads // WA_GROUP
    tq = _pick(s, tq_pref, WINDOW)
    qb = 3 * da_heads // WA_GROUP
    kb = 3 * da_heads + wa_q_heads
    vb = kb + kvh
    gw = WA_GROUP * WA_HEAD_DIM
    return pl.pallas_call(
        functools.partial(_winattn_kernel, s=s, c=c),
        grid=(kvh, s // tq),
        in_specs=[pl.BlockSpec(memory_space=pltpu.SMEM),
                  pl.BlockSpec((tq, gw), lambda g, i: (i, qb + g)),
                  pl.BlockSpec((rows, LANES), lambda g, i: (0, kb + g)),
                  pl.BlockSpec((rows, LANES), lambda g, i: (0, vb + g))],
        out_specs=pl.BlockSpec((tq, gw), lambda g, i: (i, g)),
        out_shape=jax.ShapeDtypeStruct((s, wa_q_heads * WA_HEAD_DIM), BF16),
        compiler_params=_params(("arbitrary", "arbitrary"),
                                2 * rows * LANES * 2 + 2 * tq * gw * 2,
                                scratch_bytes=8 * tq * (tq + 2 * WINDOW + c) * 4),
        name="win_attn",
    )(sink, qkv, qkv, qkv)


def _merge_kernel(yd_ref, yw_ref, wd_ref, ww_ref, gd_ref, gw_ref, bd_ref, bw_ref, o_ref):
    pd = jnp.dot(yd_ref[...], wd_ref[...], preferred_element_type=F32)
    pw = jnp.dot(yw_ref[...], ww_ref[...], preferred_element_type=F32)
    gd = _sigmoid(gd_ref[...] + bd_ref[...])
    gw = _sigmoid(gw_ref[...] + bw_ref[...])
    o_ref[...] = (gd * pd + gw * pw).astype(BF16)


def _merge(y_da, y_wa, w_o_da, w_o_wa, pr, gate_off, b_gate, tm_pref=512, tn_pref=1024):
    s, kd = y_da.shape
    kw = y_wa.shape[1]
    d = w_o_da.shape[1]
    tm = _pick(s, tm_pref, BF16_SUBLANES)
    tn = _pick(math.gcd(d, gate_off), tn_pref, LANES)
    od, ow = gate_off // tn, (gate_off + d) // tn
    nb = d // tn
    bg = b_gate.reshape(1, N_BRANCH * d)
    return pl.pallas_call(
        _merge_kernel,
        grid=(s // tm, nb),
        in_specs=[pl.BlockSpec((tm, kd), lambda i, j: (i, 0)),
                  pl.BlockSpec((tm, kw), lambda i, j: (i, 0)),
                  pl.BlockSpec((kd, tn), lambda i, j: (0, j)),
                  pl.BlockSpec((kw, tn), lambda i, j: (0, j)),
                  pl.BlockSpec((tm, tn), lambda i, j: (i, od + j)),
                  pl.BlockSpec((tm, tn), lambda i, j: (i, ow + j)),
                  pl.BlockSpec((1, tn), lambda i, j: (0, j)),
                  pl.BlockSpec((1, tn), lambda i, j: (0, nb + j))],
        out_specs=pl.BlockSpec((tm, tn), lambda i, j: (i, j)),
        out_shape=jax.ShapeDtypeStruct((s, d), BF16),
        compiler_params=_params(("arbitrary", "arbitrary"),
                                tm * (kd + kw) * 2 + (kd + kw) * tn * 2 + 2 * tm * tn * 4 + tm * tn * 2,
                                scratch_bytes=4 * tm * tn * 4),
        name="merge",
    )(y_da, y_wa, w_o_da, w_o_wa, pr, pr, bg, bg)


def _mm_res_kernel(a_ref, w_ref, x_ref, g_ref, o_ref, acc_ref):
    kk = pl.program_id(2)

    @pl.when(kk == 0)
    def _():
        acc_ref[...] = jnp.zeros_like(acc_ref)

    acc_ref[...] += jnp.dot(a_ref[...], w_ref[...], preferred_element_type=F32)

    @pl.when(kk == pl.num_programs(2) - 1)
    def _():
        o_ref[...] = x_ref[...] + g_ref[...] * acc_ref[...]


def _matmul_residual(a, w, x, gate, tm_pref, tn_pref, tk_pref, name):
    m, k = a.shape
    n = w.shape[1]
    tm = _pick(m, tm_pref, BF16_SUBLANES)
    tn = _pick(n, tn_pref, LANES)
    tk = _pick(k, tk_pref, LANES)
    return pl.pallas_call(
        _mm_res_kernel,
        grid=(m // tm, n // tn, k // tk),
        in_specs=[pl.BlockSpec((tm, tk), lambda i, j, kk: (i, kk)),
                  pl.BlockSpec((tk, tn), lambda i, j, kk: (kk, j)),
                  pl.BlockSpec((tm, tn), lambda i, j, kk: (i, j)),
                  pl.BlockSpec((1, tn), lambda i, j, kk: (0, j))],
        out_specs=pl.BlockSpec((tm, tn), lambda i, j, kk: (i, j)),
        out_shape=jax.ShapeDtypeStruct((m, n), F32),
        scratch_shapes=[pltpu.VMEM((tm, tn), F32)],
        compiler_params=_params(("arbitrary", "arbitrary", "arbitrary"),
                                tm * tk * 2 + tk * tn * 2 + 2 * tm * tn * 4,
                                scratch_bytes=2 * tm * tn * 4),
        name=name,
    )(a, w, x, gate)


HALO = BF16_SUBLANES


def _ffn_up_kernel(h_ref, hp_ref, hn_ref, wa_ref, wu_ref, cw_ref, cb_ref, o_ref, lhs_ref):
    i = pl.program_id(0)
    j = pl.program_id(1)
    tm = h_ref.shape[0]

    @pl.when(j == 0)
    def _():
        prev = jnp.where(i > 0, hp_ref[...], jnp.zeros_like(hp_ref))
        nxt = jnp.where(i < pl.num_programs(0) - 1, hn_ref[...], jnp.zeros_like(hn_ref))
        lhs_ref[0:HALO, :] = prev
        lhs_ref[HALO:HALO + tm, :] = h_ref[...]
        lhs_ref[HALO + tm:, :] = nxt

    a = jnp.dot(lhs_ref[...], wa_ref[...], preferred_element_type=F32)
    u = jnp.dot(lhs_ref[HALO:HALO + tm, :], wu_ref[...], preferred_element_type=F32)
    rows = tm + 2 * HALO
    a_prev = pltpu.roll(a, 1, 0)[HALO:HALO + tm, :]
    a_next = pltpu.roll(a, rows - 1, 0)[HALO:HALO + tm, :]
    a_mid = a[HALO:HALO + tm, :]
    conv = cb_ref[...] + a_prev * cw_ref[0:1, :] + a_mid * cw_ref[1:2, :] + a_next * cw_ref[2:3, :]
    o_ref[...] = (conv * _sigmoid(conv) * u).astype(BF16)


def _ffn_up(h, w_up, conv_w, conv_b, ffp, tm_pref=1024, tn_pref=512):
    s, d = h.shape
    tm = _pick(s, tm_pref, HALO)
    tn = _pick(ffp, tn_pref, LANES)
    nf = ffp // tn
    per = tm // HALO
    last = s // HALO - 1
    return pl.pallas_call(
        _ffn_up_kernel,
        grid=(s // tm, nf),
        in_specs=[pl.BlockSpec((tm, d), lambda i, j: (i, 0)),
                  pl.BlockSpec((HALO, d), lambda i, j: (jnp.maximum(i * per - 1, 0), 0)),
                  pl.BlockSpec((HALO, d), lambda i, j: (jnp.minimum((i + 1) * per, last), 0)),
                  pl.BlockSpec((d, tn), lambda i, j: (0, j)),
                  pl.BlockSpec((d, tn), lambda i, j: (0, nf + j)),
                  pl.BlockSpec((CONV_W, tn), lambda i, j: (0, j)),
                  pl.BlockSpec((1, tn), lambda i, j: (0, j))],
        out_specs=pl.BlockSpec((tm, tn), lambda i, j: (i, j)),
        out_shape=jax.ShapeDtypeStruct((s, ffp), BF16),
        scratch_shapes=[pltpu.VMEM((tm + 2 * HALO, d), BF16)],
        compiler_params=_params(("arbitrary", "arbitrary"),
                                (tm + 2 * HALO) * d * 2 + 2 * d * tn * 2 + tm * tn * 2,
                                scratch_bytes=(tm + 2 * HALO) * d * 2 + 6 * (tm + 2 * HALO) * tn * 4),
        name="ffn_up",
    )(h, h, h, w_up, w_up, conv_w, conv_b)


def kernel(x, c, ctx, c_ctx, w_ada, b_ada, attn_norm_g, w_in, b_gate, da_qn_g, da_kn_g, da_lambda_q1, da_lambda_k1, da_lambda_q2, da_lambda_k2, da_subln_g, wa_qn_g, wa_kn_g, wa_sink, w_o_da, w_o_wa, w_out, ffn_norm_g, w_ffn_up, ffn_conv_w, ffn_conv_b, w_ffn_down):
    b, s, d = x.shape
    cl = ctx.shape[1]
    assert b == 1 and w_in.shape[0] == 1, "one batch element, one layer"
    da_heads = w_o_da.shape[1] // DA_V_DIM
    wa_q_heads = w_o_wa.shape[1] // WA_HEAD_DIM
    wa_kv_heads = wa_q_heads // WA_GROUP
    d_ff = ffn_conv_b.shape[1]
    gate_off = (3 * da_heads + wa_q_heads + 2 * wa_kv_heads) * LANES
    assert w_in.shape[2] == gate_off + N_BRANCH * d

    x2 = x.reshape(s, d)
    ctx2 = ctx.reshape(cl, d)

    ffp = -(-d_ff // 1024) * 1024 if d_ff > 1024 else d_ff
    padf = ffp - d_ff
    w_in_b = w_in[0].astype(BF16)
    w_o_da_b = w_o_da[0].astype(BF16)
    w_o_wa_b = w_o_wa[0].astype(BF16)
    w_out_b = w_out[0].astype(BF16)
    wu = w_ffn_up[0].astype(BF16)
    w_up_b = jnp.concatenate(
        [jnp.pad(wu[:, :d_ff], ((0, 0), (0, padf))), jnp.pad(wu[:, d_ff:], ((0, 0), (0, padf)))],
        axis=1)
    w_down_b = jnp.pad(w_ffn_down[0].astype(BF16), ((0, padf), (0, 0)))
    conv_w = jnp.pad(ffn_conv_w[0], ((0, 0), (0, padf)))
    conv_b = jnp.pad(ffn_conv_b[0], (0, padf)).reshape(1, ffp)

    cc = jnp.concatenate([c, c_ctx[None, :], jnp.zeros((6, d), F32)], axis=0)
    mod = _adaln(cc, w_ada[0], b_ada[0])
    sh1, sc1, g1 = mod[0:1, 0:d], mod[0:1, d:2 * d], mod[0:1, 2 * d:3 * d]
    sh2, sc2, g2 = mod[0:1, 3 * d:4 * d], mod[0:1, 4 * d:5 * d], mod[0:1, 5 * d:6 * d]
    mod_x1 = jnp.concatenate([sh1, sc1], axis=0)
    mod_c1 = jnp.concatenate([mod[1:2, 0:d], mod[1:2, d:2 * d]], axis=0)
    mod_x2 = jnp.concatenate([sh2, sc2], axis=0)

    h = _normmod(x2, attn_norm_g[0], mod_x1, ctx2, mod_c1)
    pr = _matmul(h, w_in_b, 1280, 512)
    tabs = _rope_tables(s, cl)
    gains = jnp.stack([jnp.tile(da_qn_g[0] * (DA_SCALE * LOG2_E), 2), jnp.tile(da_kn_g[0], 2),
                       wa_qn_g[0] * (WA_SCALE * LOG2_E), wa_kn_g[0]], axis=0)
    qkv = _headprep(pr, tabs, gains, da_heads, wa_q_heads)
    lam_vecs = jnp.stack([da_lambda_q1[0], da_lambda_k1[0], da_lambda_q2[0], da_lambda_k2[0]], axis=0)
    y_da = _diffattn(qkv, lam_vecs, da_subln_g[0], s, da_heads)
    y_wa = _winattn(qkv, wa_sink[0], s, cl, da_heads, wa_q_heads)
    u = _merge(y_da, y_wa, w_o_da_b, w_o_wa_b, pr, gate_off, b_gate[0])
    x1 = _matmul_residual(u, w_out_b, x2, g1, 1024, 1024, d, "out_proj")

    h2 = _normmod(x1, ffn_norm_g[0], mod_x2)
    act = _ffn_up(h2, w_up_b, conv_w, conv_b, ffp)
    out = _matmul_residual(act, w_down_b, x1, g2, 1024, 1024, 2816, "ffn_down")
    return out.reshape(b, s, d)
```

```python
import functools
import math

import jax
import jax.numpy as jnp
from jax import lax
from jax.experimental import pallas as pl
from jax.experimental.pallas import tpu as pltpu

F32 = jnp.float32
BF16 = jnp.bfloat16

GRID_W = 64
DA_QK_DIM = 64
DA_V_DIM = 2 * DA_QK_DIM
WA_HEAD_DIM = 128
WA_GROUP = 4
WINDOW = 128
N_BRANCH = 2
CONV_W = 3
ROPE_BASE = 10000.0
EPS = 1e-6
DA_SCALE = DA_QK_DIM ** -0.5
WA_SCALE = WA_HEAD_DIM ** -0.5
NEG_INF = -1e30
LAM_INIT = 0.8 - 0.6 * math.exp(-0.3 * 0)

LANES = 128
BF16_SUBLANES = 16
V7X_VMEM_CAP_BYTES = 58 * 1024 * 1024
VMEM_SLACK_BYTES = 6 * 1024 * 1024


def _pick(dim, pref, unit):
    best = None
    t = unit
    while t <= min(dim, pref):
        if dim % t == 0:
            best = t
        t += unit
    if best is None:
        raise ValueError(f"no tile for dim={dim} unit={unit}")
    return best


def _params(semantics, block_bytes, scratch_bytes=0):
    est = 2 * block_bytes + scratch_bytes + VMEM_SLACK_BYTES
    return pltpu.CompilerParams(
        dimension_semantics=semantics,
        vmem_limit_bytes=min(max(est, 16 * 1024 * 1024), V7X_VMEM_CAP_BYTES))


def _sigmoid(x):
    return 1.0 / (1.0 + jnp.exp(-x))


def _adaln_kernel(c_ref, w_ref, b_ref, o_ref):
    a = c_ref[...]
    a = a * _sigmoid(a)
    o_ref[...] = jnp.dot(a.astype(BF16), w_ref[...].astype(BF16),
                         preferred_element_type=F32) + b_ref[...]


def _adaln(cc, w, b):
    rows, d = cc.shape
    n = w.shape[1]
    tn = _pick(n, 512, LANES)
    return pl.pallas_call(
        _adaln_kernel,
        grid=(n // tn,),
        in_specs=[pl.BlockSpec((rows, d), lambda j: (0, 0)),
                  pl.BlockSpec((d, tn), lambda j: (0, j)),
                  pl.BlockSpec((1, tn), lambda j: (0, j))],
        out_specs=pl.BlockSpec((rows, tn), lambda j: (0, j)),
        out_shape=jax.ShapeDtypeStruct((rows, n), F32),
        compiler_params=_params(("arbitrary",), d * tn * 4 + d * tn * 2),
        name="adaln",
    )(cc, w, b.reshape(1, n))


def _normmod_rows(x, g, mod):
    ms = jnp.mean(x * x, axis=-1, keepdims=True)
    y = x * lax.rsqrt(ms + EPS) * g
    return (y * (1.0 + mod[1:2, :]) + mod[0:1, :]).astype(BF16)


def _normmod2_kernel(x_ref, c_ref, g_ref, mx_ref, mc_ref, o_ref, *, nx):
    i = pl.program_id(0)

    @pl.when(i < nx)
    def _():
        o_ref[...] = _normmod_rows(x_ref[...], g_ref[...], mx_ref[...])

    @pl.when(i >= nx)
    def _():
        o_ref[...] = _normmod_rows(c_ref[...], g_ref[...], mc_ref[...])


def _normmod1_kernel(x_ref, g_ref, mx_ref, o_ref):
    o_ref[...] = _normmod_rows(x_ref[...], g_ref[...], mx_ref[...])


def _normmod(x, g, mod_x, ctx=None, mod_c=None):
    s, d = x.shape
    g = g.reshape(1, d)
    if ctx is None:
        tm = _pick(s, 256, BF16_SUBLANES)
        return pl.pallas_call(
            _normmod1_kernel,
            grid=(s // tm,),
            in_specs=[pl.BlockSpec((tm, d), lambda i: (i, 0)),
                      pl.BlockSpec((1, d), lambda i: (0, 0)),
                      pl.BlockSpec((2, d), lambda i: (0, 0))],
            out_specs=pl.BlockSpec((tm, d), lambda i: (i, 0)),
            out_shape=jax.ShapeDtypeStruct((s, d), BF16),
            compiler_params=_params(("arbitrary",), tm * d * 6),
            name="normmod",
        )(x, g, mod_x)
    c = ctx.shape[0]
    tm = _pick(math.gcd(s, c), 256, BF16_SUBLANES)
    nx, nc = s // tm, c // tm
    return pl.pallas_call(
        functools.partial(_normmod2_kernel, nx=nx),
        grid=(nx + nc,),
        in_specs=[pl.BlockSpec((tm, d), lambda i: (jnp.minimum(i, nx - 1), 0)),
                  pl.BlockSpec((tm, d), lambda i: (jnp.maximum(i - nx, 0), 0)),
                  pl.BlockSpec((1, d), lambda i: (0, 0)),
                  pl.BlockSpec((2, d), lambda i: (0, 0)),
                  pl.BlockSpec((2, d), lambda i: (0, 0))],
        out_specs=pl.BlockSpec((tm, d), lambda i: (i, 0)),
        out_shape=jax.ShapeDtypeStruct((s + c, d), BF16),
        compiler_params=_params(("arbitrary",), tm * d * 10),
        name="normmod_xc",
    )(x, ctx, g, mod_x, mod_c)


def _mm_kernel(a_ref, w_ref, o_ref):
    o_ref[...] = jnp.dot(a_ref[...], w_ref[...], preferred_element_type=F32)


def _matmul(a, w, tm_pref, tn_pref):
    m, k = a.shape
    n = w.shape[1]
    tm = _pick(m, tm_pref, BF16_SUBLANES)
    tn = _pick(n, tn_pref, LANES)
    return pl.pallas_call(
        _mm_kernel,
        grid=(m // tm, n // tn),
        in_specs=[pl.BlockSpec((tm, k), lambda i, j: (i, 0)),
                  pl.BlockSpec((k, tn), lambda i, j: (0, j))],
        out_specs=pl.BlockSpec((tm, tn), lambda i, j: (i, j)),
        out_shape=jax.ShapeDtypeStruct((m, n), F32),
        compiler_params=_params(("arbitrary", "arbitrary"),
                                tm * k * 2 + k * tn * 2 + tm * tn * 4, scratch_bytes=tm * tn * 4),
        name="in_proj",
    )(a, w)


def _rope_tables(s, c):
    pos = jnp.arange(s, dtype=jnp.int32)
    r = (pos // GRID_W).astype(F32)[:, None]
    col = (pos % GRID_W).astype(F32)[:, None]
    lane = jnp.arange(LANES, dtype=jnp.int32)

    def table(head_dim):
        axis_dim = head_dim // 2
        pair = axis_dim // 2
        within = lane % head_dim
        f = (within % pair).astype(F32)
        freq = ROPE_BASE ** (-(2.0 * f) / axis_dim)
        use_row = (within // axis_dim) == 0
        ang = jnp.where(use_row[None, :], r * freq[None, :], col * freq[None, :])
        first = (within % axis_dim) < pair
        cos = jnp.cos(ang)
        sin = jnp.where(first[None, :], -jnp.sin(ang), jnp.sin(ang))
        cos = jnp.concatenate([cos, jnp.ones((c, LANES), F32)], axis=0)
        sin = jnp.concatenate([sin, jnp.zeros((c, LANES), F32)], axis=0)
        return cos, sin

    cd, sd = table(DA_QK_DIM)
    cw, sw = table(WA_HEAD_DIM)
    return jnp.stack([cd, sd, cw, sw], axis=0)


def _headprep_kernel(pr_ref, tab_ref, gain_ref, ones_da_ref, ones_wa_ref, o_ref,
                     *, n_qk_da, n_v_da, n_q_wa, n_k_wa, n_v_wa):
    tm = pr_ref.shape[0]
    lane = lax.broadcasted_iota(jnp.int32, (tm, LANES), 1)
    first_da = (lane % (DA_QK_DIM // 2)) < (DA_QK_DIM // 4)
    first_wa = (lane % (WA_HEAD_DIM // 2)) < (WA_HEAD_DIM // 4)
    cos_da, sin_da = tab_ref[0], tab_ref[1]
    cos_wa, sin_wa = tab_ref[2], tab_ref[3]

    def rope(y, cos, sin, first, shift):
        partner = jnp.where(first, pltpu.roll(y, LANES - shift, 1), pltpu.roll(y, shift, 1))
        return y * cos + partner * sin

    def head_sum_sq(x, ones_ref):
        x2 = x * x
        hi = x2.astype(BF16)
        mid = (x2 - hi.astype(F32)).astype(BF16)
        return jnp.dot(jnp.concatenate([hi, mid], axis=1), ones_ref[...], preferred_element_type=F32)

    def da_group(gidx, gain):
        x = pr_ref[:, gidx * LANES:(gidx + 1) * LANES]
        ms = head_sum_sq(x, ones_da_ref) * (1.0 / DA_QK_DIM)
        y = x * lax.rsqrt(ms + EPS) * gain
        y = rope(y, cos_da, sin_da, first_da, DA_QK_DIM // 4)
        o_ref[:, gidx * LANES:(gidx + 1) * LANES] = y.astype(BF16)

    def wa_group(gidx, gain):
        x = pr_ref[:, gidx * LANES:(gidx + 1) * LANES]
        ms = head_sum_sq(x, ones_wa_ref) * (1.0 / WA_HEAD_DIM)
        y = x * lax.rsqrt(ms + EPS) * gain
        y = rope(y, cos_wa, sin_wa, first_wa, WA_HEAD_DIM // 4)
        o_ref[:, gidx * LANES:(gidx + 1) * LANES] = y.astype(BF16)

    def copy_group(gidx):
        o_ref[:, gidx * LANES:(gidx + 1) * LANES] = pr_ref[:, gidx * LANES:(gidx + 1) * LANES].astype(BF16)

    g = 0
    for _ in range(n_qk_da):
        da_group(g, gain_ref[0:1, :]); g += 1
    for _ in range(n_qk_da):
        da_group(g, gain_ref[1:2, :]); g += 1
    for _ in range(n_v_da):
        copy_group(g); g += 1
    for _ in range(n_q_wa):
        wa_group(g, gain_ref[2:3, :]); g += 1
    for _ in range(n_k_wa):
        wa_group(g, gain_ref[3:4, :]); g += 1
    for _ in range(n_v_wa):
        copy_group(g); g += 1


def _headprep(pr, tabs, gains, da_heads, wa_q_heads, tm_pref=128):
    rows = pr.shape[0]
    wa_kv_heads = wa_q_heads // WA_GROUP
    groups = dict(n_qk_da=da_heads, n_v_da=da_heads, n_q_wa=wa_q_heads,
                  n_k_wa=wa_kv_heads, n_v_wa=wa_kv_heads)
    width = (3 * da_heads + wa_q_heads + 2 * wa_kv_heads) * LANES
    tm = _pick(rows, tm_pref, BF16_SUBLANES)
    lane = jnp.arange(LANES)
    same_da_head = (lane[:, None] // DA_QK_DIM) == (lane[None, :] // DA_QK_DIM)
    ones_da = jnp.tile(same_da_head.astype(BF16), (2, 1))
    ones_wa = jnp.ones((2 * LANES, LANES), BF16)
    return pl.pallas_call(
        functools.partial(_headprep_kernel, **groups),
        grid=(rows // tm,),
        in_specs=[pl.BlockSpec((tm, width), lambda i: (i, 0)),
                  pl.BlockSpec((4, tm, LANES), lambda i: (0, i, 0)),
                  pl.BlockSpec((4, LANES), lambda i: (0, 0)),
                  pl.BlockSpec((2 * LANES, LANES), lambda i: (0, 0)),
                  pl.BlockSpec((2 * LANES, LANES), lambda i: (0, 0))],
        out_specs=pl.BlockSpec((tm, width), lambda i: (i, 0)),
        out_shape=jax.ShapeDtypeStruct((rows, width), BF16),
        compiler_params=_params(("arbitrary",), tm * width * 6 + 4 * tm * LANES * 4),
        name="headprep",
    )(pr, tabs, gains, ones_da, ones_wa)


def _nt_dot(a, b):
    return lax.dot_general(a, b, (((1,), (1,)), ((), ())), preferred_element_type=F32)


STEPS_PER_TRIP = 4
LOG2_E = math.log2(math.e)


def _diffattn_kernel(lam_ref, q_ref, k_ref, v_ref, g_ref, o_ref,
                     s_ref, p_ref, m_ref, mb_ref, al_ref, acc_ref, *, tk, rb):
    tq = q_ref.shape[0]
    nk = k_ref.shape[0] // tk
    q = q_ref[...]
    lane = lax.broadcasted_iota(jnp.int32, (tq, LANES), 1)
    zero = jnp.zeros_like(q)
    qs = (jnp.where(lane < DA_QK_DIM, q, zero), jnp.where(lane < DA_QK_DIM, zero, q))
    ones = jnp.ones((tk, LANES), BF16)

    def scores(c, slot):
        k = k_ref[pl.ds(pl.multiple_of(c * tk, tk), tk), :]
        for h in range(2):
            s = _nt_dot(qs[h], k).astype(BF16)
            s_ref[slot, h] = s
            m_old = m_ref[h]
            m_new = jnp.maximum(m_old, jnp.max(s, axis=-1, keepdims=True).astype(F32))
            m_ref[h] = m_new
            mb_ref[slot, h] = jnp.broadcast_to(m_new, (tq, LANES)).astype(BF16)
            al_ref[slot, h] = jnp.exp2(m_old - m_new)

    def softmax(slot):
        for h in range(2):
            for r in range(0, tq, rb):
                mb = mb_ref[slot, h, r:r + rb, :]
                for t in range(0, tk, LANES):
                    p_ref[slot, h, r:r + rb, t:t + LANES] = jnp.exp2(
                        s_ref[slot, h, r:r + rb, t:t + LANES] - mb)

    def accumulate(c, slot):
        v = v_ref[pl.ds(pl.multiple_of(c * tk, tk), tk), :]
        v1 = jnp.concatenate([v, ones], axis=1)
        for h in range(2):
            acc_ref[h] = (al_ref[slot, h] * acc_ref[h]
                          + jnp.dot(p_ref[slot, h], v1, preferred_element_type=F32))

    def step(c, slot, with_softmax=True, with_scores=True):
        accumulate(c, slot)
        if with_softmax:
            softmax(1 - slot)
        if with_scores:
            scores(c + 2, slot)

    m_ref[...] = jnp.full(m_ref.shape, NEG_INF, F32)
    acc_ref[...] = jnp.zeros(acc_ref.shape, F32)
    scores(0, 0)
    scores(1, 1)
    softmax(0)

    def group(i, carry):
        for t in range(STEPS_PER_TRIP):
            step(STEPS_PER_TRIP * i + t, t % 2)
        return carry

    n_full = nk - 2
    lax.fori_loop(0, n_full // STEPS_PER_TRIP, group, 0)
    c = STEPS_PER_TRIP * (n_full // STEPS_PER_TRIP)
    while c < n_full:
        step(c, c % 2)
        c += 1
    step(c, c % 2, with_scores=False)
    step(c + 1, (c + 1) % 2, with_softmax=False, with_scores=False)
    a1, a2 = acc_ref[0], acc_ref[1]

    lam = (jnp.exp(jnp.sum(lam_ref[0:1, :] * lam_ref[1:2, :], axis=-1, keepdims=True))
           - jnp.exp(jnp.sum(lam_ref[2:3, :] * lam_ref[3:4, :], axis=-1, keepdims=True))
           + LAM_INIT)
    o = (a1[:, :DA_V_DIM] / a1[:, DA_V_DIM:]) - lam * (a2[:, :DA_V_DIM] / a2[:, DA_V_DIM:])
    ms = jnp.mean(o * o, axis=-1, keepdims=True)
    o = o * lax.rsqrt(ms + EPS) * g_ref[...] * (1.0 - LAM_INIT)
    o_ref[...] = o.astype(BF16)


def _diffattn(qkv, lam_vecs, subln_g, s, da_heads, tq_pref=512, tk_pref=1280):
    rows = qkv.shape[0]
    tq = _pick(s, tq_pref, BF16_SUBLANES)
    tk = _pick(rows, min(tk_pref, rows // 2), LANES)
    kb, vb = da_heads, 2 * da_heads
    return pl.pallas_call(
        functools.partial(_diffattn_kernel, tk=tk, rb=BF16_SUBLANES),
        grid=(da_heads, s // tq),
        in_specs=[pl.BlockSpec((4, DA_QK_DIM), lambda h, i: (0, 0)),
                  pl.BlockSpec((tq, LANES), lambda h, i: (i, h)),
                  pl.BlockSpec((rows, LANES), lambda h, i: (0, kb + h)),
                  pl.BlockSpec((rows, LANES), lambda h, i: (0, vb + h)),
                  pl.BlockSpec((1, DA_V_DIM), lambda h, i: (0, 0))],
        out_specs=pl.BlockSpec((tq, DA_V_DIM), lambda h, i: (i, h)),
        out_shape=jax.ShapeDtypeStruct((s, da_heads * DA_V_DIM), BF16),
        scratch_shapes=[pltpu.VMEM((2, 2, tq, tk), BF16),
                        pltpu.VMEM((2, 2, tq, tk), BF16),
                        pltpu.VMEM((2, tq, 1), F32),
                        pltpu.VMEM((2, 2, tq, LANES), BF16),
                        pltpu.VMEM((2, 2, tq, 1), F32),
                        pltpu.VMEM((2, tq, 2 * DA_V_DIM), F32)],
        compiler_params=_params(("arbitrary", "arbitrary"),
                                2 * rows * LANES * 2 + 2 * tq * LANES * 2,
                                scratch_bytes=4 * tq * tk * 6 + 4 * tq * tk * 4),
        name="diff_attn",
    )(lam_vecs, qkv, qkv, qkv, subln_g.reshape(1, DA_V_DIM))


def _winattn_kernel(sink_ref, q_ref, k_ref, v_ref, o_ref, *, s, c):
    g = pl.program_id(0)
    i = pl.program_id(1)
    tq = q_ref.shape[0]
    band = 3 * WINDOW
    rows = WA_GROUP * WINDOW
    kc = k_ref[s:s + c, :]
    vc = v_ref[s:s + c, :]
    row = lax.broadcasted_iota(jnp.int32, (rows, band), 0)
    col = lax.broadcasted_iota(jnp.int32, (rows, band), 1)
    head = lax.broadcasted_iota(jnp.int32, (rows, 1), 0) // WINDOW
    sink = jnp.zeros((rows, 1), F32)
    for r in range(WA_GROUP):
        sink = jnp.where(head == r, sink_ref[g * WA_GROUP + r] * LOG2_E, sink)
    for b in range(tq // WINDOW):
        q0 = i * tq + b * WINDOW
        start = pl.multiple_of(jnp.clip(q0 - WINDOW, 0, s - band), WINDOW)
        kb = k_ref[pl.ds(start, band), :]
        vb = v_ref[pl.ds(start, band), :]
        valid = jnp.abs(q0 + row % WINDOW - (start + col)) <= WINDOW
        q = jnp.concatenate(
            [q_ref[b * WINDOW:(b + 1) * WINDOW, r * WA_HEAD_DIM:(r + 1) * WA_HEAD_DIM]
             for r in range(WA_GROUP)], axis=0)
        sb = jnp.where(valid, _nt_dot(q, kb), NEG_INF)
        sc = _nt_dot(q, kc)
        m = jnp.maximum(jnp.max(sb, axis=-1, keepdims=True), jnp.max(sc, axis=-1, keepdims=True))
        m = jnp.maximum(m, sink)
        pb = jnp.exp2(sb - m)
        pc = jnp.exp2(sc - m)
        l = (jnp.sum(pb, axis=-1, keepdims=True) + jnp.sum(pc, axis=-1, keepdims=True)
             + jnp.exp2(sink - m))
        o = (jnp.dot(pb.astype(BF16), vb, preferred_element_type=F32)
             + jnp.dot(pc.astype(BF16), vc, preferred_element_type=F32)) * (1.0 / l)
        for r in range(WA_GROUP):
            o_ref[b * WINDOW:(b + 1) * WINDOW, r * WA_HEAD_DIM:(r + 1) * WA_HEAD_DIM] = (
                o[r * WINDOW:(r + 1) * WINDOW].astype(BF16))


def _winattn(qkv, sink, s, c, da_heads, wa_q_heads, tq_pref=512):
    rows = qkv.shape[0]
    kvh = wa_q_heads // WA_GROUP
    tq = _pick(s, tq_pref, WINDOW)
    qb = 3 * da_heads // WA_GROUP
    kb = 3 * da_heads + wa_q_heads
    vb = kb + kvh
    gw = WA_GROUP * WA_HEAD_DIM
    return pl.pallas_call(
        functools.partial(_winattn_kernel, s=s, c=c),
        grid=(kvh, s // tq),
        in_specs=[pl.BlockSpec(memory_space=pltpu.SMEM),
                  pl.BlockSpec((tq, gw), lambda g, i: (i, qb + g)),
                  pl.BlockSpec((rows, LANES), lambda g, i: (0, kb + g)),
                  pl.BlockSpec((rows, LANES), lambda g, i: (0, vb + g))],
        out_specs=pl.BlockSpec((tq, gw), lambda g, i: (i, g)),
        out_shape=jax.ShapeDtypeStruct((s, wa_q_heads * WA_HEAD_DIM), BF16),
        compiler_params=_params(("arbitrary", "arbitrary"),
                                2 * rows * LANES * 2 + 2 * tq * gw * 2,
                                scratch_bytes=8 * tq * (tq + 2 * WINDOW + c) * 4),
        name="win_attn",
    )(sink, qkv, qkv, qkv)


def _merge_kernel(yd_ref, yw_ref, wd_ref, ww_ref, gd_ref, gw_ref, bd_ref, bw_ref, o_ref):
    pd = jnp.dot(yd_ref[...], wd_ref[...], preferred_element_type=F32)
    pw = jnp.dot(yw_ref[...], ww_ref[...], preferred_element_type=F32)
    gd = _sigmoid(gd_ref[...] + bd_ref[...])
    gw = _sigmoid(gw_ref[...] + bw_ref[...])
    o_ref[...] = (gd * pd + gw * pw).astype(BF16)


def _merge(y_da, y_wa, w_o_da, w_o_wa, pr, gate_off, b_gate, tm_pref=512, tn_pref=1024):
    s, kd = y_da.shape
    kw = y_wa.shape[1]
    d = w_o_da.shape[1]
    tm = _pick(s, tm_pref, BF16_SUBLANES)
    tn = _pick(math.gcd(d, gate_off), tn_pref, LANES)
    od, ow = gate_off // tn, (gate_off + d) // tn
    nb = d // tn
    bg = b_gate.reshape(1, N_BRANCH * d)
    return pl.pallas_call(
        _merge_kernel,
        grid=(s // tm, nb),
        in_specs=[pl.BlockSpec((tm, kd), lambda i, j: (i, 0)),
                  pl.BlockSpec((tm, kw), lambda i, j: (i, 0)),
                  pl.BlockSpec((kd, tn), lambda i, j: (0, j)),
                  pl.BlockSpec((kw, tn), lambda i, j: (0, j)),
                  pl.BlockSpec((tm, tn), lambda i, j: (i, od + j)),
                  pl.BlockSpec((tm, tn), lambda i, j: (i, ow + j)),
                  pl.BlockSpec((1, tn), lambda i, j: (0, j)),
                  pl.BlockSpec((1, tn), lambda i, j: (0, nb + j))],
        out_specs=pl.BlockSpec((tm, tn), lambda i, j: (i, j)),
        out_shape=jax.ShapeDtypeStruct((s, d), BF16),
        compiler_params=_params(("arbitrary", "arbitrary"),
                                tm * (kd + kw) * 2 + (kd + kw) * tn * 2 + 2 * tm * tn * 4 + tm * tn * 2,
                                scratch_bytes=4 * tm * tn * 4),
        name="merge",
    )(y_da, y_wa, w_o_da, w_o_wa, pr, pr, bg, bg)


def _mm_res_kernel(a_ref, w_ref, x_ref, g_ref, o_ref, acc_ref):
    kk = pl.program_id(2)

    @pl.when(kk == 0)
    def _():
        acc_ref[...] = jnp.zeros_like(acc_ref)

    acc_ref[...] += jnp.dot(a_ref[...], w_ref[...], preferred_element_type=F32)

    @pl.when(kk == pl.num_programs(2) - 1)
    def _():
        o_ref[...] = x_ref[...] + g_ref[...] * acc_ref[...]


def _matmul_residual(a, w, x, gate, tm_pref, tn_pref, tk_pref, name):
    m, k = a.shape
    n = w.shape[1]
    tm = _pick(m, tm_pref, BF16_SUBLANES)
    tn = _pick(n, tn_pref, LANES)
    tk = _pick(k, tk_pref, LANES)
    return pl.pallas_call(
        _mm_res_kernel,
        grid=(m // tm, n // tn, k // tk),
        in_specs=[pl.BlockSpec((tm, tk), lambda i, j, kk: (i, kk)),
                  pl.BlockSpec((tk, tn), lambda i, j, kk: (kk, j)),
                  pl.BlockSpec((tm, tn), lambda i, j, kk: (i, j)),
                  pl.BlockSpec((1, tn), lambda i, j, kk: (0, j))],
        out_specs=pl.BlockSpec((tm, tn), lambda i, j, kk: (i, j)),
        out_shape=jax.ShapeDtypeStruct((m, n), F32),
        scratch_shapes=[pltpu.VMEM((tm, tn), F32)],
        compiler_params=_params(("arbitrary", "arbitrary", "arbitrary"),
                                tm * tk * 2 + tk * tn * 2 + 2 * tm * tn * 4,
                                scratch_bytes=2 * tm * tn * 4),
        name=name,
    )(a, w, x, gate)


HALO = BF16_SUBLANES


def _ffn_up_kernel(h_ref, hp_ref, hn_ref, wa_ref, wu_ref, cw_ref, cb_ref, o_ref, lhs_ref):
    i = pl.program_id(0)
    j = pl.program_id(1)
    tm = h_ref.shape[0]

    @pl.when(j == 0)
    def _():
        prev = jnp.where(i > 0, hp_ref[...], jnp.zeros_like(hp_ref))
        nxt = jnp.where(i < pl.num_programs(0) - 1, hn_ref[...], jnp.zeros_like(hn_ref))
        lhs_ref[0:HALO, :] = prev
        lhs_ref[HALO:HALO + tm, :] = h_ref[...]
        lhs_ref[HALO + tm:, :] = nxt

    a = jnp.dot(lhs_ref[...], wa_ref[...], preferred_element_type=F32)
    u = jnp.dot(lhs_ref[HALO:HALO + tm, :], wu_ref[...], preferred_element_type=F32)
    rows = tm + 2 * HALO
    a_prev = pltpu.roll(a, 1, 0)[HALO:HALO + tm, :]
    a_next = pltpu.roll(a, rows - 1, 0)[HALO:HALO + tm, :]
    a_mid = a[HALO:HALO + tm, :]
    conv = cb_ref[...] + a_prev * cw_ref[0:1, :] + a_mid * cw_ref[1:2, :] + a_next * cw_ref[2:3, :]
    o_ref[...] = (conv * _sigmoid(conv) * u).astype(BF16)


def _ffn_up(h, w_gate, w_lin, conv_w, conv_b, tm_pref=1024, tn_pref=512):
    s, d = h.shape
    ffp = w_gate.shape[1]
    tm = _pick(s, tm_pref, HALO)
    tn = _pick(ffp, tn_pref, LANES)
    nf = ffp // tn
    per = tm // HALO
    last = s // HALO - 1
    return pl.pallas_call(
        _ffn_up_kernel,
        grid=(s // tm, nf),
        in_specs=[pl.BlockSpec((tm, d), lambda i, j: (i, 0)),
                  pl.BlockSpec((HALO, d), lambda i, j: (jnp.maximum(i * per - 1, 0), 0)),
                  pl.BlockSpec((HALO, d), lambda i, j: (jnp.minimum((i + 1) * per, last), 0)),
                  pl.BlockSpec((d, tn), lambda i, j: (0, j)),
                  pl.BlockSpec((d, tn), lambda i, j: (0, j)),
                  pl.BlockSpec((CONV_W, tn), lambda i, j: (0, j)),
                  pl.BlockSpec((1, tn), lambda i, j: (0, j))],
        out_specs=pl.BlockSpec((tm, tn), lambda i, j: (i, j)),
        out_shape=jax.ShapeDtypeStruct((s, ffp), BF16),
        scratch_shapes=[pltpu.VMEM((tm + 2 * HALO, d), BF16)],
        compiler_params=_params(("arbitrary", "arbitrary"),
                                (tm + 2 * HALO) * d * 2 + 2 * d * tn * 2 + tm * tn * 2,
                                scratch_bytes=(tm + 2 * HALO) * d * 2 + 6 * (tm + 2 * HALO) * tn * 4),
        name="ffn_up",
    )(h, h, h, w_gate, w_lin, conv_w, conv_b)


def kernel(x, c, ctx, c_ctx, w_ada, b_ada, attn_norm_g, w_in, b_gate, da_qn_g, da_kn_g, da_lambda_q1, da_lambda_k1, da_lambda_q2, da_lambda_k2, da_subln_g, wa_qn_g, wa_kn_g, wa_sink, w_o_da, w_o_wa, w_out, ffn_norm_g, w_ffn_up, ffn_conv_w, ffn_conv_b, w_ffn_down):
    b, s, d = x.shape
    cl = ctx.shape[1]
    assert b == 1 and w_in.shape[0] == 1, "one batch element, one layer"
    da_heads = w_o_da.shape[1] // DA_V_DIM
    wa_q_heads = w_o_wa.shape[1] // WA_HEAD_DIM
    wa_kv_heads = wa_q_heads // WA_GROUP
    d_ff = ffn_conv_b.shape[1]
    gate_off = (3 * da_heads + wa_q_heads + 2 * wa_kv_heads) * LANES
    assert w_in.shape[2] == gate_off + N_BRANCH * d

    x2 = x.reshape(s, d)
    ctx2 = ctx.reshape(cl, d)

    ffp = -(-d_ff // 1024) * 1024 if d_ff > 1024 else d_ff
    padf = ffp - d_ff
    w_in_b = w_in[0].astype(BF16)
    w_o_da_b = w_o_da[0].astype(BF16)
    w_o_wa_b = w_o_wa[0].astype(BF16)
    w_out_b = w_out[0].astype(BF16)
    w_gate_b = jnp.pad(w_ffn_up[0][:, :d_ff].astype(BF16), ((0, 0), (0, padf)))
    w_lin_b = jnp.pad(w_ffn_up[0][:, d_ff:].astype(BF16), ((0, 0), (0, padf)))
    w_down_b = jnp.pad(w_ffn_down[0].astype(BF16), ((0, padf), (0, 0)))
    conv_w = jnp.pad(ffn_conv_w[0], ((0, 0), (0, padf)))
    conv_b = jnp.pad(ffn_conv_b[0], (0, padf)).reshape(1, ffp)

    cc = jnp.concatenate([c, c_ctx[None, :], jnp.zeros((6, d), F32)], axis=0)
    mod = _adaln(cc, w_ada[0], b_ada[0])
    sh1, sc1, g1 = mod[0:1, 0:d], mod[0:1, d:2 * d], mod[0:1, 2 * d:3 * d]
    sh2, sc2, g2 = mod[0:1, 3 * d:4 * d], mod[0:1, 4 * d:5 * d], mod[0:1, 5 * d:6 * d]
    mod_x1 = jnp.concatenate([sh1, sc1], axis=0)
    mod_c1 = jnp.concatenate([mod[1:2, 0:d], mod[1:2, d:2 * d]], axis=0)
    mod_x2 = jnp.concatenate([sh2, sc2], axis=0)

    h = _normmod(x2, attn_norm_g[0], mod_x1, ctx2, mod_c1)
    pr = _matmul(h, w_in_b, 1280, 512)
    tabs = _rope_tables(s, cl)
    gains = jnp.stack([jnp.tile(da_qn_g[0] * (DA_SCALE * LOG2_E), 2), jnp.tile(da_kn_g[0], 2),
                       wa_qn_g[0] * (WA_SCALE * LOG2_E), wa_kn_g[0]], axis=0)
    qkv = _headprep(pr, tabs, gains, da_heads, wa_q_heads)
    lam_vecs = jnp.stack([da_lambda_q1[0], da_lambda_k1[0], da_lambda_q2[0], da_lambda_k2[0]], axis=0)
    y_da = _diffattn(qkv, lam_vecs, da_subln_g[0], s, da_heads)
    y_wa = _winattn(qkv, wa_sink[0], s, cl, da_heads, wa_q_heads)
    u = _merge(y_da, y_wa, w_o_da_b, w_o_wa_b, pr, gate_off, b_gate[0])
    x1 = _matmul_residual(u, w_out_b, x2, g1, 1024, 1024, d, "out_proj")

    h2 = _normmod(x1, ffn_norm_g[0], mod_x2)
    act = _ffn_up(h2, w_gate_b, w_lin_b, conv_w, conv_b)
    out = _matmul_residual(act, w_down_b, x1, g2, 1024, 1024, 2816, "ffn_down")
    return out.reshape(b, s, d)
```

```python
import functools
import math

import jax
import jax.numpy as jnp
from jax import lax
from jax.experimental import pallas as pl
from jax.experimental.pallas import tpu as pltpu

F32 = jnp.float32
BF16 = jnp.bfloat16

GRID_W = 64
DA_QK_DIM = 64
DA_V_DIM = 2 * DA_QK_DIM
WA_HEAD_DIM = 128
WA_GROUP = 4
WINDOW = 128
N_BRANCH = 2
CONV_W = 3
ROPE_BASE = 10000.0
EPS = 1e-6
DA_SCALE = DA_QK_DIM ** -0.5
WA_SCALE = WA_HEAD_DIM ** -0.5
NEG_INF = -1e30
LAM_INIT = 0.8 - 0.6 * math.exp(-0.3 * 0)

LANES = 128
BF16_SUBLANES = 16
V7X_VMEM_CAP_BYTES = 58 * 1024 * 1024
VMEM_SLACK_BYTES = 6 * 1024 * 1024


def _pick(dim, pref, unit):
    best = None
    t = unit
    while t <= min(dim, pref):
        if dim % t == 0:
            best = t
        t += unit
    if best is None:
        raise ValueError(f"no tile for dim={dim} unit={unit}")
    return best


def _params(semantics, block_bytes, scratch_bytes=0, flags=None):
    est = 2 * block_bytes + scratch_bytes + VMEM_SLACK_BYTES
    return pltpu.CompilerParams(
        dimension_semantics=semantics,
        vmem_limit_bytes=min(max(est, 16 * 1024 * 1024), V7X_VMEM_CAP_BYTES),
        flags=flags)


def _sigmoid(x):
    return 1.0 / (1.0 + jnp.exp(-x))


def _adaln_kernel(c_ref, w_ref, b_ref, o_ref):
    a = c_ref[...]
    a = a * _sigmoid(a)
    o_ref[...] = jnp.dot(a.astype(BF16), w_ref[...].astype(BF16),
                         preferred_element_type=F32) + b_ref[...]


def _adaln(cc, w, b):
    rows, d = cc.shape
    n = w.shape[1]
    tn = _pick(n, 512, LANES)
    return pl.pallas_call(
        _adaln_kernel,
        grid=(n // tn,),
        in_specs=[pl.BlockSpec((rows, d), lambda j: (0, 0)),
                  pl.BlockSpec((d, tn), lambda j: (0, j)),
                  pl.BlockSpec((1, tn), lambda j: (0, j))],
        out_specs=pl.BlockSpec((rows, tn), lambda j: (0, j)),
        out_shape=jax.ShapeDtypeStruct((rows, n), F32),
        compiler_params=_params(("arbitrary",), d * tn * 4 + d * tn * 2),
        name="adaln",
    )(cc, w, b.reshape(1, n))


def _normmod_rows(x, g, mod):
    ms = jnp.mean(x * x, axis=-1, keepdims=True)
    y = x * lax.rsqrt(ms + EPS) * g
    return (y * (1.0 + mod[1:2, :]) + mod[0:1, :]).astype(BF16)


def _normmod2_kernel(x_ref, c_ref, g_ref, mx_ref, mc_ref, o_ref, *, nx):
    i = pl.program_id(0)

    @pl.when(i < nx)
    def _():
        o_ref[...] = _normmod_rows(x_ref[...], g_ref[...], mx_ref[...])

    @pl.when(i >= nx)
    def _():
        o_ref[...] = _normmod_rows(c_ref[...], g_ref[...], mc_ref[...])


def _normmod1_kernel(x_ref, g_ref, mx_ref, o_ref):
    o_ref[...] = _normmod_rows(x_ref[...], g_ref[...], mx_ref[...])


def _normmod(x, g, mod_x, ctx=None, mod_c=None):
    s, d = x.shape
    g = g.reshape(1, d)
    if ctx is None:
        tm = _pick(s, 256, BF16_SUBLANES)
        return pl.pallas_call(
            _normmod1_kernel,
            grid=(s // tm,),
            in_specs=[pl.BlockSpec((tm, d), lambda i: (i, 0)),
                      pl.BlockSpec((1, d), lambda i: (0, 0)),
                      pl.BlockSpec((2, d), lambda i: (0, 0))],
            out_specs=pl.BlockSpec((tm, d), lambda i: (i, 0)),
            out_shape=jax.ShapeDtypeStruct((s, d), BF16),
            compiler_params=_params(("arbitrary",), tm * d * 6),
            name="normmod",
        )(x, g, mod_x)
    c = ctx.shape[0]
    tm = _pick(math.gcd(s, c), 256, BF16_SUBLANES)
    nx, nc = s // tm, c // tm
    return pl.pallas_call(
        functools.partial(_normmod2_kernel, nx=nx),
        grid=(nx + nc,),
        in_specs=[pl.BlockSpec((tm, d), lambda i: (jnp.minimum(i, nx - 1), 0)),
                  pl.BlockSpec((tm, d), lambda i: (jnp.maximum(i - nx, 0), 0)),
                  pl.BlockSpec((1, d), lambda i: (0, 0)),
                  pl.BlockSpec((2, d), lambda i: (0, 0)),
                  pl.BlockSpec((2, d), lambda i: (0, 0))],
        out_specs=pl.BlockSpec((tm, d), lambda i: (i, 0)),
        out_shape=jax.ShapeDtypeStruct((s + c, d), BF16),
        compiler_params=_params(("arbitrary",), tm * d * 10),
        name="normmod_xc",
    )(x, ctx, g, mod_x, mod_c)


def _mm_kernel(a_ref, w_ref, o_ref):
    o_ref[...] = jnp.dot(a_ref[...], w_ref[...], preferred_element_type=F32)


def _matmul(a, w, tm_pref, tn_pref):
    m, k = a.shape
    n = w.shape[1]
    tm = _pick(m, tm_pref, BF16_SUBLANES)
    tn = _pick(n, tn_pref, LANES)
    return pl.pallas_call(
        _mm_kernel,
        grid=(m // tm, n // tn),
        in_specs=[pl.BlockSpec((tm, k), lambda i, j: (i, 0)),
                  pl.BlockSpec((k, tn), lambda i, j: (0, j))],
        out_specs=pl.BlockSpec((tm, tn), lambda i, j: (i, j)),
        out_shape=jax.ShapeDtypeStruct((m, n), F32),
        compiler_params=_params(("arbitrary", "arbitrary"),
                                tm * k * 2 + k * tn * 2 + tm * tn * 4, scratch_bytes=tm * tn * 4),
        name="in_proj",
    )(a, w)


def _rope_tables(s, c):
    pos = jnp.arange(s, dtype=jnp.int32)
    r = (pos // GRID_W).astype(F32)[:, None]
    col = (pos % GRID_W).astype(F32)[:, None]
    lane = jnp.arange(LANES, dtype=jnp.int32)

    def table(head_dim):
        axis_dim = head_dim // 2
        pair = axis_dim // 2
        within = lane % head_dim
        f = (within % pair).astype(F32)
        freq = ROPE_BASE ** (-(2.0 * f) / axis_dim)
        use_row = (within // axis_dim) == 0
        ang = jnp.where(use_row[None, :], r * freq[None, :], col * freq[None, :])
        first = (within % axis_dim) < pair
        cos = jnp.cos(ang)
        sin = jnp.where(first[None, :], -jnp.sin(ang), jnp.sin(ang))
        cos = jnp.concatenate([cos, jnp.ones((c, LANES), F32)], axis=0)
        sin = jnp.concatenate([sin, jnp.zeros((c, LANES), F32)], axis=0)
        return cos, sin

    cd, sd = table(DA_QK_DIM)
    cw, sw = table(WA_HEAD_DIM)
    return jnp.stack([cd, sd, cw, sw], axis=0)


def _headprep_kernel(pr_ref, tab_ref, gain_ref, ones_da_ref, ones_wa_ref, o_ref,
                     *, n_qk_da, n_v_da, n_q_wa, n_k_wa, n_v_wa):
    tm = pr_ref.shape[0]
    lane = lax.broadcasted_iota(jnp.int32, (tm, LANES), 1)
    first_da = (lane % (DA_QK_DIM // 2)) < (DA_QK_DIM // 4)
    first_wa = (lane % (WA_HEAD_DIM // 2)) < (WA_HEAD_DIM // 4)
    cos_da, sin_da = tab_ref[0], tab_ref[1]
    cos_wa, sin_wa = tab_ref[2], tab_ref[3]

    def rope(y, cos, sin, first, shift):
        partner = jnp.where(first, pltpu.roll(y, LANES - shift, 1), pltpu.roll(y, shift, 1))
        return y * cos + partner * sin

    def head_sum_sq(x, ones_ref):
        x2 = x * x
        hi = x2.astype(BF16)
        mid = (x2 - hi.astype(F32)).astype(BF16)
        return jnp.dot(jnp.concatenate([hi, mid], axis=1), ones_ref[...], preferred_element_type=F32)

    def da_group(gidx, gain):
        x = pr_ref[:, gidx * LANES:(gidx + 1) * LANES]
        ms = head_sum_sq(x, ones_da_ref) * (1.0 / DA_QK_DIM)
        y = x * lax.rsqrt(ms + EPS) * gain
        y = rope(y, cos_da, sin_da, first_da, DA_QK_DIM // 4)
        o_ref[:, gidx * LANES:(gidx + 1) * LANES] = y.astype(BF16)

    def wa_group(gidx, gain):
        x = pr_ref[:, gidx * LANES:(gidx + 1) * LANES]
        ms = head_sum_sq(x, ones_wa_ref) * (1.0 / WA_HEAD_DIM)
        y = x * lax.rsqrt(ms + EPS) * gain
        y = rope(y, cos_wa, sin_wa, first_wa, WA_HEAD_DIM // 4)
        o_ref[:, gidx * LANES:(gidx + 1) * LANES] = y.astype(BF16)

    def copy_group(gidx):
        o_ref[:, gidx * LANES:(gidx + 1) * LANES] = pr_ref[:, gidx * LANES:(gidx + 1) * LANES].astype(BF16)

    g = 0
    for _ in range(n_qk_da):
        da_group(g, gain_ref[0:1, :]); g += 1
    for _ in range(n_qk_da):
        da_group(g, gain_ref[1:2, :]); g += 1
    for _ in range(n_v_da):
        copy_group(g); g += 1
    for _ in range(n_q_wa):
        wa_group(g, gain_ref[2:3, :]); g += 1
    for _ in range(n_k_wa):
        wa_group(g, gain_ref[3:4, :]); g += 1
    for _ in range(n_v_wa):
        copy_group(g); g += 1


def _headprep(pr, tabs, gains, da_heads, wa_q_heads, tm_pref=128):
    rows = pr.shape[0]
    wa_kv_heads = wa_q_heads // WA_GROUP
    groups = dict(n_qk_da=da_heads, n_v_da=da_heads, n_q_wa=wa_q_heads,
                  n_k_wa=wa_kv_heads, n_v_wa=wa_kv_heads)
    width = (3 * da_heads + wa_q_heads + 2 * wa_kv_heads) * LANES
    tm = _pick(rows, tm_pref, BF16_SUBLANES)
    lane = jnp.arange(LANES)
    same_da_head = (lane[:, None] // DA_QK_DIM) == (lane[None, :] // DA_QK_DIM)
    ones_da = jnp.tile(same_da_head.astype(BF16), (2, 1))
    ones_wa = jnp.ones((2 * LANES, LANES), BF16)
    return pl.pallas_call(
        functools.partial(_headprep_kernel, **groups),
        grid=(rows // tm,),
        in_specs=[pl.BlockSpec((tm, width), lambda i: (i, 0)),
                  pl.BlockSpec((4, tm, LANES), lambda i: (0, i, 0)),
                  pl.BlockSpec((4, LANES), lambda i: (0, 0)),
                  pl.BlockSpec((2 * LANES, LANES), lambda i: (0, 0)),
                  pl.BlockSpec((2 * LANES, LANES), lambda i: (0, 0))],
        out_specs=pl.BlockSpec((tm, width), lambda i: (i, 0)),
        out_shape=jax.ShapeDtypeStruct((rows, width), BF16),
        compiler_params=_params(("arbitrary",), tm * width * 6 + 4 * tm * LANES * 4),
        name="headprep",
    )(pr, tabs, gains, ones_da, ones_wa)


def _nt_dot(a, b):
    return lax.dot_general(a, b, (((1,), (1,)), ((), ())), preferred_element_type=F32)


STEPS_PER_TRIP = 4
LOG2_E = math.log2(math.e)


def _diffattn_kernel(lam_ref, q_ref, k_ref, v_ref, g_ref, o_ref,
                     s_ref, p_ref, m_ref, mb_ref, al_ref, acc_ref, *, tk, rb):
    tq = q_ref.shape[0]
    nk = k_ref.shape[0] // tk
    q = q_ref[...]
    lane = lax.broadcasted_iota(jnp.int32, (tq, LANES), 1)
    zero = jnp.zeros_like(q)
    qs = (jnp.where(lane < DA_QK_DIM, q, zero), jnp.where(lane < DA_QK_DIM, zero, q))
    ones = jnp.ones((tk, LANES), BF16)

    def scores(c, slot):
        k = k_ref[pl.ds(pl.multiple_of(c * tk, tk), tk), :]
        for h in range(2):
            s_ref[slot, h] = _nt_dot(qs[h], k).astype(BF16)

    def softmax(slot):
        for h in range(2):
            for r in range(0, tq, rb):
                m_old = m_ref[h, r:r + rb, :]
                mx = jnp.max(s_ref[slot, h, r:r + rb, :], axis=-1, keepdims=True).astype(F32)
                m_new = jnp.maximum(m_old, mx)
                m_ref[h, r:r + rb, :] = m_new
                mb_ref[h, r:r + rb, :] = jnp.broadcast_to(m_new, (rb, LANES)).astype(BF16)
                al_ref[slot, h, r:r + rb, :] = jnp.exp2(m_old - m_new)
        for h in range(2):
            for r in range(0, tq, rb):
                mb = mb_ref[h, r:r + rb, :]
                for t in range(0, tk, LANES):
                    p_ref[slot, h, r:r + rb, t:t + LANES] = jnp.exp2(
                        s_ref[slot, h, r:r + rb, t:t + LANES] - mb)

    def accumulate(c, slot):
        v = v_ref[pl.ds(pl.multiple_of(c * tk, tk), tk), :]
        v1 = jnp.concatenate([v, ones], axis=1)
        for h in range(2):
            acc_ref[h] = (al_ref[slot, h] * acc_ref[h]
                          + jnp.dot(p_ref[slot, h], v1, preferred_element_type=F32))

    def step(c, slot, with_softmax=True, with_scores=True):
        accumulate(c, slot)
        if with_softmax:
            softmax(1 - slot)
        if with_scores:
            scores(c + 2, slot)

    m_ref[...] = jnp.full(m_ref.shape, NEG_INF, F32)
    acc_ref[...] = jnp.zeros(acc_ref.shape, F32)
    scores(0, 0)
    scores(1, 1)
    softmax(0)

    def group(i, carry):
        for t in range(STEPS_PER_TRIP):
            step(STEPS_PER_TRIP * i + t, t % 2)
        return carry

    n_full = nk - 2
    lax.fori_loop(0, n_full // STEPS_PER_TRIP, group, 0)
    c = STEPS_PER_TRIP * (n_full // STEPS_PER_TRIP)
    while c < n_full:
        step(c, c % 2)
        c += 1
    step(c, c % 2, with_scores=False)
    step(c + 1, (c + 1) % 2, with_softmax=False, with_scores=False)
    a1, a2 = acc_ref[0], acc_ref[1]

    lam = (jnp.exp(jnp.sum(lam_ref[0:1, :] * lam_ref[1:2, :], axis=-1, keepdims=True))
           - jnp.exp(jnp.sum(lam_ref[2:3, :] * lam_ref[3:4, :], axis=-1, keepdims=True))
           + LAM_INIT)
    o = (a1[:, :DA_V_DIM] / a1[:, DA_V_DIM:]) - lam * (a2[:, :DA_V_DIM] / a2[:, DA_V_DIM:])
    ms = jnp.mean(o * o, axis=-1, keepdims=True)
    o = o * lax.rsqrt(ms + EPS) * g_ref[...] * (1.0 - LAM_INIT)
    o_ref[...] = o.astype(BF16)


def _diffattn(qkv, lam_vecs, subln_g, s, da_heads, tq_pref=256, tk_pref=1280):
    rows = qkv.shape[0]
    tq = _pick(s, tq_pref, BF16_SUBLANES)
    tk = _pick(rows, min(tk_pref, rows // 2), LANES)
    kb, vb = da_heads, 2 * da_heads
    return pl.pallas_call(
        functools.partial(_diffattn_kernel, tk=tk, rb=BF16_SUBLANES),
        grid=(da_heads, s // tq),
        in_specs=[pl.BlockSpec((4, DA_QK_DIM), lambda h, i: (0, 0)),
                  pl.BlockSpec((tq, LANES), lambda h, i: (i, h)),
                  pl.BlockSpec((rows, LANES), lambda h, i: (0, kb + h)),
                  pl.BlockSpec((rows, LANES), lambda h, i: (0, vb + h)),
                  pl.BlockSpec((1, DA_V_DIM), lambda h, i: (0, 0))],
        out_specs=pl.BlockSpec((tq, DA_V_DIM), lambda h, i: (i, h)),
        out_shape=jax.ShapeDtypeStruct((s, da_heads * DA_V_DIM), BF16),
        scratch_shapes=[pltpu.VMEM((2, 2, tq, tk), BF16),
                        pltpu.VMEM((2, 2, tq, tk), BF16),
                        pltpu.VMEM((2, tq, 1), F32),
                        pltpu.VMEM((2, tq, LANES), BF16),
                        pltpu.VMEM((2, 2, tq, 1), F32),
                        pltpu.VMEM((2, tq, 2 * DA_V_DIM), F32)],
        compiler_params=_params(("arbitrary", "arbitrary"),
                                2 * rows * LANES * 2 + 2 * tq * LANES * 2,
                                scratch_bytes=4 * tq * tk * 6 + 4 * tq * tk * 4),
        name="diff_attn",
    )(lam_vecs, qkv, qkv, qkv, subln_g.reshape(1, DA_V_DIM))


def _winattn_kernel(sink_ref, q_ref, k_ref, v_ref, o_ref, *, s, c):
    g = pl.program_id(0)
    i = pl.program_id(1)
    tq = q_ref.shape[0]
    band = 3 * WINDOW
    rows = WA_GROUP * WINDOW
    kc = k_ref[s:s + c, :]
    vc = v_ref[s:s + c, :]
    row = lax.broadcasted_iota(jnp.int32, (rows, band), 0)
    col = lax.broadcasted_iota(jnp.int32, (rows, band), 1)
    head = lax.broadcasted_iota(jnp.int32, (rows, 1), 0) // WINDOW
    sink = jnp.zeros((rows, 1), F32)
    for r in range(WA_GROUP):
        sink = jnp.where(head == r, sink_ref[g * WA_GROUP + r] * LOG2_E, sink)
    for b in range(tq // WINDOW):
        q0 = i * tq + b * WINDOW
        start = pl.multiple_of(jnp.clip(q0 - WINDOW, 0, s - band), WINDOW)
        kb = k_ref[pl.ds(start, band), :]
        vb = v_ref[pl.ds(start, band), :]
        valid = jnp.abs(q0 + row % WINDOW - (start + col)) <= WINDOW
        q = jnp.concatenate(
            [q_ref[b * WINDOW:(b + 1) * WINDOW, r * WA_HEAD_DIM:(r + 1) * WA_HEAD_DIM]
             for r in range(WA_GROUP)], axis=0)
        sb = jnp.where(valid, _nt_dot(q, kb), NEG_INF)
        sc = _nt_dot(q, kc)
        m = jnp.maximum(jnp.max(sb, axis=-1, keepdims=True), jnp.max(sc, axis=-1, keepdims=True))
        m = jnp.maximum(m, sink)
        pb = jnp.exp2(sb - m)
        pc = jnp.exp2(sc - m)
        l = (jnp.sum(pb, axis=-1, keepdims=True) + jnp.sum(pc, axis=-1, keepdims=True)
             + jnp.exp2(sink - m))
        o = (jnp.dot(pb.astype(BF16), vb, preferred_element_type=F32)
             + jnp.dot(pc.astype(BF16), vc, preferred_element_type=F32)) * (1.0 / l)
        for r in range(WA_GROUP):
            o_ref[b * WINDOW:(b + 1) * WINDOW, r * WA_HEAD_DIM:(r + 1) * WA_HEAD_DIM] = (
                o[r * WINDOW:(r + 1) * WINDOW].astype(BF16))


def _winattn(qkv, sink, s, c, da_heads, wa_q_heads, tq_pref=512):
    rows = qkv.shape[0]
    kvh = wa_q_heads // WA_GROUP
    tq = _pick(s, tq_pref, WINDOW)
    qb = 3 * da_heads // WA_GROUP
    kb = 3 * da_heads + wa_q_heads
    vb = kb + kvh
    gw = WA_GROUP * WA_HEAD_DIM
    return pl.pallas_call(
        functools.partial(_winattn_kernel, s=s, c=c),
        grid=(kvh, s // tq),
        in_specs=[pl.BlockSpec(memory_space=pltpu.SMEM),
                  pl.BlockSpec((tq, gw), lambda g, i: (i, qb + g)),
                  pl.BlockSpec((rows, LANES), lambda g, i: (0, kb + g)),
                  pl.BlockSpec((rows, LANES), lambda g, i: (0, vb + g))],
        out_specs=pl.BlockSpec((tq, gw), lambda g, i: (i, g)),
        out_shape=jax.ShapeDtypeStruct((s, wa_q_heads * WA_HEAD_DIM), BF16),
        compiler_params=_params(("arbitrary", "arbitrary"),
                                2 * rows * LANES * 2 + 2 * tq * gw * 2,
                                scratch_bytes=8 * tq * (tq + 2 * WINDOW + c) * 4),
        name="win_attn",
    )(sink, qkv, qkv, qkv)


def _merge_kernel(yd_ref, yw_ref, wd_ref, ww_ref, gd_ref, gw_ref, bd_ref, bw_ref, o_ref):
    pd = jnp.dot(yd_ref[...], wd_ref[...], preferred_element_type=F32)
    pw = jnp.dot(yw_ref[...], ww_ref[...], preferred_element_type=F32)
    gd = _sigmoid(gd_ref[...] + bd_ref[...])
    gw = _sigmoid(gw_ref[...] + bw_ref[...])
    o_ref[...] = (gd * pd + gw * pw).astype(BF16)


def _merge(y_da, y_wa, w_o_da, w_o_wa, pr, gate_off, b_gate, tm_pref=512, tn_pref=1024):
    s, kd = y_da.shape
    kw = y_wa.shape[1]
    d = w_o_da.shape[1]
    tm = _pick(s, tm_pref, BF16_SUBLANES)
    tn = _pick(math.gcd(d, gate_off), tn_pref, LANES)
    od, ow = gate_off // tn, (gate_off + d) // tn
    nb = d // tn
    bg = b_gate.reshape(1, N_BRANCH * d)
    return pl.pallas_call(
        _merge_kernel,
        grid=(s // tm, nb),
        in_specs=[pl.BlockSpec((tm, kd), lambda i, j: (i, 0)),
                  pl.BlockSpec((tm, kw), lambda i, j: (i, 0)),
                  pl.BlockSpec((kd, tn), lambda i, j: (0, j)),
                  pl.BlockSpec((kw, tn), lambda i, j: (0, j)),
                  pl.BlockSpec((tm, tn), lambda i, j: (i, od + j)),
                  pl.BlockSpec((tm, tn), lambda i, j: (i, ow + j)),
                  pl.BlockSpec((1, tn), lambda i, j: (0, j)),
                  pl.BlockSpec((1, tn), lambda i, j: (0, nb + j))],
        out_specs=pl.BlockSpec((tm, tn), lambda i, j: (i, j)),
        out_shape=jax.ShapeDtypeStruct((s, d), BF16),
        compiler_params=_params(("arbitrary", "arbitrary"),
                                tm * (kd + kw) * 2 + (kd + kw) * tn * 2 + 2 * tm * tn * 4 + tm * tn * 2,
                                scratch_bytes=4 * tm * tn * 4),
        name="merge",
    )(y_da, y_wa, w_o_da, w_o_wa, pr, pr, bg, bg)


def _mm_res_kernel(a_ref, w_ref, x_ref, g_ref, o_ref, acc_ref):
    kk = pl.program_id(2)

    @pl.when(kk == 0)
    def _():
        acc_ref[...] = jnp.zeros_like(acc_ref)

    acc_ref[...] += jnp.dot(a_ref[...], w_ref[...], preferred_element_type=F32)

    @pl.when(kk == pl.num_programs(2) - 1)
    def _():
        o_ref[...] = x_ref[...] + g_ref[...] * acc_ref[...]


def _matmul_residual(a, w, x, gate, tm_pref, tn_pref, tk_pref, name):
    m, k = a.shape
    n = w.shape[1]
    tm = _pick(m, tm_pref, BF16_SUBLANES)
    tn = _pick(n, tn_pref, LANES)
    tk = _pick(k, tk_pref, LANES)
    return pl.pallas_call(
        _mm_res_kernel,
        grid=(m // tm, n // tn, k // tk),
        in_specs=[pl.BlockSpec((tm, tk), lambda i, j, kk: (i, kk)),
                  pl.BlockSpec((tk, tn), lambda i, j, kk: (kk, j)),
                  pl.BlockSpec((tm, tn), lambda i, j, kk: (i, j)),
                  pl.BlockSpec((1, tn), lambda i, j, kk: (0, j))],
        out_specs=pl.BlockSpec((tm, tn), lambda i, j, kk: (i, j)),
        out_shape=jax.ShapeDtypeStruct((m, n), F32),
        scratch_shapes=[pltpu.VMEM((tm, tn), F32)],
        compiler_params=_params(("arbitrary", "arbitrary", "arbitrary"),
                                tm * tk * 2 + tk * tn * 2 + 2 * tm * tn * 4,
                                scratch_bytes=2 * tm * tn * 4),
        name=name,
    )(a, w, x, gate)


HALO = BF16_SUBLANES


def _ffn_up_kernel(h_ref, hp_ref, hn_ref, wa_ref, wu_ref, cw_ref, cb_ref, o_ref, lhs_ref):
    i = pl.program_id(0)
    j = pl.program_id(1)
    tm = h_ref.shape[0]

    @pl.when(j == 0)
    def _():
        prev = jnp.where(i > 0, hp_ref[...], jnp.zeros_like(hp_ref))
        nxt = jnp.where(i < pl.num_programs(0) - 1, hn_ref[...], jnp.zeros_like(hn_ref))
        lhs_ref[0:HALO, :] = prev
        lhs_ref[HALO:HALO + tm, :] = h_ref[...]
        lhs_ref[HALO + tm:, :] = nxt

    a = jnp.dot(lhs_ref[...], wa_ref[...], preferred_element_type=F32)
    u = jnp.dot(lhs_ref[HALO:HALO + tm, :], wu_ref[...], preferred_element_type=F32)
    rows = tm + 2 * HALO
    a_prev = pltpu.roll(a, 1, 0)[HALO:HALO + tm, :]
    a_next = pltpu.roll(a, rows - 1, 0)[HALO:HALO + tm, :]
    a_mid = a[HALO:HALO + tm, :]
    conv = cb_ref[...] + a_prev * cw_ref[0:1, :] + a_mid * cw_ref[1:2, :] + a_next * cw_ref[2:3, :]
    o_ref[...] = (conv * _sigmoid(conv) * u).astype(BF16)


def _ffn_up(h, w_gate, w_lin, conv_w, conv_b, tm_pref=1024, tn_pref=512):
    s, d = h.shape
    ffp = w_gate.shape[1]
    tm = _pick(s, tm_pref, HALO)
    tn = _pick(ffp, tn_pref, LANES)
    nf = ffp // tn
    per = tm // HALO
    last = s // HALO - 1
    return pl.pallas_call(
        _ffn_up_kernel,
        grid=(s // tm, nf),
        in_specs=[pl.BlockSpec((tm, d), lambda i, j: (i, 0)),
                  pl.BlockSpec((HALO, d), lambda i, j: (jnp.maximum(i * per - 1, 0), 0)),
                  pl.BlockSpec((HALO, d), lambda i, j: (jnp.minimum((i + 1) * per, last), 0)),
                  pl.BlockSpec((d, tn), lambda i, j: (0, j)),
                  pl.BlockSpec((d, tn), lambda i, j: (0, j)),
                  pl.BlockSpec((CONV_W, tn), lambda i, j: (0, j)),
                  pl.BlockSpec((1, tn), lambda i, j: (0, j))],
        out_specs=pl.BlockSpec((tm, tn), lambda i, j: (i, j)),
        out_shape=jax.ShapeDtypeStruct((s, ffp), BF16),
        scratch_shapes=[pltpu.VMEM((tm + 2 * HALO, d), BF16)],
        compiler_params=_params(("arbitrary", "arbitrary"),
                                (tm + 2 * HALO) * d * 2 + 2 * d * tn * 2 + tm * tn * 2,
                                scratch_bytes=(tm + 2 * HALO) * d * 2 + 6 * (tm + 2 * HALO) * tn * 4),
        name="ffn_up",
    )(h, h, h, w_gate, w_lin, conv_w, conv_b)


def kernel(x, c, ctx, c_ctx, w_ada, b_ada, attn_norm_g, w_in, b_gate, da_qn_g, da_kn_g, da_lambda_q1, da_lambda_k1, da_lambda_q2, da_lambda_k2, da_subln_g, wa_qn_g, wa_kn_g, wa_sink, w_o_da, w_o_wa, w_out, ffn_norm_g, w_ffn_up, ffn_conv_w, ffn_conv_b, w_ffn_down):
    b, s, d = x.shape
    cl = ctx.shape[1]
    assert b == 1 and w_in.shape[0] == 1, "one batch element, one layer"
    da_heads = w_o_da.shape[1] // DA_V_DIM
    wa_q_heads = w_o_wa.shape[1] // WA_HEAD_DIM
    wa_kv_heads = wa_q_heads // WA_GROUP
    d_ff = ffn_conv_b.shape[1]
    gate_off = (3 * da_heads + wa_q_heads + 2 * wa_kv_heads) * LANES
    assert w_in.shape[2] == gate_off + N_BRANCH * d

    x2 = x.reshape(s, d)
    ctx2 = ctx.reshape(cl, d)

    ffp = -(-d_ff // 1024) * 1024 if d_ff > 1024 else d_ff
    padf = ffp - d_ff
    w_in_b = w_in[0].astype(BF16)
    w_o_da_b = w_o_da[0].astype(BF16)
    w_o_wa_b = w_o_wa[0].astype(BF16)
    w_out_b = w_out[0].astype(BF16)
    w_gate_b = jnp.pad(w_ffn_up[0][:, :d_ff].astype(BF16), ((0, 0), (0, padf)))
    w_lin_b = jnp.pad(w_ffn_up[0][:, d_ff:].astype(BF16), ((0, 0), (0, padf)))
    w_down_b = jnp.pad(w_ffn_down[0].astype(BF16), ((0, padf), (0, 0)))
    conv_w = jnp.pad(ffn_conv_w[0], ((0, 0), (0, padf)))
    conv_b = jnp.pad(ffn_conv_b[0], (0, padf)).reshape(1, ffp)

    cc = jnp.concatenate([c, c_ctx[None, :], jnp.zeros((6, d), F32)], axis=0)
    mod = _adaln(cc, w_ada[0], b_ada[0])
    sh1, sc1, g1 = mod[0:1, 0:d], mod[0:1, d:2 * d], mod[0:1, 2 * d:3 * d]
    sh2, sc2, g2 = mod[0:1, 3 * d:4 * d], mod[0:1, 4 * d:5 * d], mod[0:1, 5 * d:6 * d]
    mod_x1 = jnp.concatenate([sh1, sc1], axis=0)
    mod_c1 = jnp.concatenate([mod[1:2, 0:d], mod[1:2, d:2 * d]], axis=0)
    mod_x2 = jnp.concatenate([sh2, sc2], axis=0)

    h = _normmod(x2, attn_norm_g[0], mod_x1, ctx2, mod_c1)
    pr = _matmul(h, w_in_b, 1280, 512)
    tabs = _rope_tables(s, cl)
    gains = jnp.stack([jnp.tile(da_qn_g[0] * (DA_SCALE * LOG2_E), 2), jnp.tile(da_kn_g[0], 2),
                       wa_qn_g[0] * (WA_SCALE * LOG2_E), wa_kn_g[0]], axis=0)
    qkv = _headprep(pr, tabs, gains, da_heads, wa_q_heads)
    lam_vecs = jnp.stack([da_lambda_q1[0], da_lambda_k1[0], da_lambda_q2[0], da_lambda_k2[0]], axis=0)
    y_da = _diffattn(qkv, lam_vecs, da_subln_g[0], s, da_heads)
    y_wa = _winattn(qkv, wa_sink[0], s, cl, da_heads, wa_q_heads)
    u = _merge(y_da, y_wa, w_o_da_b, w_o_wa_b, pr, gate_off, b_gate[0])
    x1 = _matmul_residual(u, w_out_b, x2, g1, 1024, 1024, d, "out_proj")

    h2 = _normmod(x1, ffn_norm_g[0], mod_x2)
    act = _ffn_up(h2, w_gate_b, w_lin_b, conv_w, conv_b)
    out = _matmul_residual(act, w_down_b, x1, g2, 1024, 1024, 2816, "ffn_down")
    return out.reshape(b, s, d)
```

```python
import functools
import math

import jax
import jax.numpy as jnp
from jax import lax
from jax.experimental import pallas as pl
from jax.experimental.pallas import tpu as pltpu

F32 = jnp.float32
BF16 = jnp.bfloat16

GRID_W = 64
DA_QK_DIM = 64
DA_V_DIM = 2 * DA_QK_DIM
WA_HEAD_DIM = 128
WA_GROUP = 4
WINDOW = 128
N_BRANCH = 2
CONV_W = 3
ROPE_BASE = 10000.0
EPS = 1e-6
DA_SCALE = DA_QK_DIM ** -0.5
WA_SCALE = WA_HEAD_DIM ** -0.5
NEG_INF = -1e30
LAM_INIT = 0.8 - 0.6 * math.exp(-0.3 * 0)

LANES = 128
BF16_SUBLANES = 16
V7X_VMEM_CAP_BYTES = 58 * 1024 * 1024
VMEM_SLACK_BYTES = 6 * 1024 * 1024


def _pick(dim, pref, unit):
    best = None
    t = unit
    while t <= min(dim, pref):
        if dim % t == 0:
            best = t
        t += unit
    if best is None:
        raise ValueError(f"no tile for dim={dim} unit={unit}")
    return best


def _params(semantics, block_bytes, scratch_bytes=0):
    est = 2 * block_bytes + scratch_bytes + VMEM_SLACK_BYTES
    return pltpu.CompilerParams(
        dimension_semantics=semantics,
        vmem_limit_bytes=min(max(est, 16 * 1024 * 1024), V7X_VMEM_CAP_BYTES))


def _sigmoid(x):
    return 1.0 / (1.0 + jnp.exp(-x))


def _adaln_kernel(c_ref, w_ref, b_ref, o_ref):
    a = c_ref[...]
    a = a * _sigmoid(a)
    o_ref[...] = jnp.dot(a.astype(BF16), w_ref[...].astype(BF16),
                         preferred_element_type=F32) + b_ref[...]


def _adaln(cc, w, b):
    rows, d = cc.shape
    n = w.shape[1]
    tn = _pick(n, 512, LANES)
    return pl.pallas_call(
        _adaln_kernel,
        grid=(n // tn,),
        in_specs=[pl.BlockSpec((rows, d), lambda j: (0, 0)),
                  pl.BlockSpec((d, tn), lambda j: (0, j)),
                  pl.BlockSpec((1, tn), lambda j: (0, j))],
        out_specs=pl.BlockSpec((rows, tn), lambda j: (0, j)),
        out_shape=jax.ShapeDtypeStruct((rows, n), F32),
        compiler_params=_params(("arbitrary",), d * tn * 4 + d * tn * 2),
        name="adaln",
    )(cc, w, b.reshape(1, n))


def _normmod_rows(x, g, mod):
    ms = jnp.mean(x * x, axis=-1, keepdims=True)
    y = x * lax.rsqrt(ms + EPS) * g
    return (y * (1.0 + mod[1:2, :]) + mod[0:1, :]).astype(BF16)


def _normmod2_kernel(x_ref, c_ref, g_ref, mx_ref, mc_ref, o_ref, *, nx):
    i = pl.program_id(0)

    @pl.when(i < nx)
    def _():
        o_ref[...] = _normmod_rows(x_ref[...], g_ref[...], mx_ref[...])

    @pl.when(i >= nx)
    def _():
        o_ref[...] = _normmod_rows(c_ref[...], g_ref[...], mc_ref[...])


def _normmod1_kernel(x_ref, g_ref, mx_ref, o_ref):
    o_ref[...] = _normmod_rows(x_ref[...], g_ref[...], mx_ref[...])


def _normmod(x, g, mod_x, ctx=None, mod_c=None):
    s, d = x.shape
    g = g.reshape(1, d)
    if ctx is None:
        tm = _pick(s, 256, BF16_SUBLANES)
        return pl.pallas_call(
            _normmod1_kernel,
            grid=(s // tm,),
            in_specs=[pl.BlockSpec((tm, d), lambda i: (i, 0)),
                      pl.BlockSpec((1, d), lambda i: (0, 0)),
                      pl.BlockSpec((2, d), lambda i: (0, 0))],
            out_specs=pl.BlockSpec((tm, d), lambda i: (i, 0)),
            out_shape=jax.ShapeDtypeStruct((s, d), BF16),
            compiler_params=_params(("arbitrary",), tm * d * 6),
            name="normmod",
        )(x, g, mod_x)
    c = ctx.shape[0]
    tm = _pick(math.gcd(s, c), 256, BF16_SUBLANES)
    nx, nc = s // tm, c // tm
    return pl.pallas_call(
        functools.partial(_normmod2_kernel, nx=nx),
        grid=(nx + nc,),
        in_specs=[pl.BlockSpec((tm, d), lambda i: (jnp.minimum(i, nx - 1), 0)),
                  pl.BlockSpec((tm, d), lambda i: (jnp.maximum(i - nx, 0), 0)),
                  pl.BlockSpec((1, d), lambda i: (0, 0)),
                  pl.BlockSpec((2, d), lambda i: (0, 0)),
                  pl.BlockSpec((2, d), lambda i: (0, 0))],
        out_specs=pl.BlockSpec((tm, d), lambda i: (i, 0)),
        out_shape=jax.ShapeDtypeStruct((s + c, d), BF16),
        compiler_params=_params(("arbitrary",), tm * d * 10),
        name="normmod_xc",
    )(x, ctx, g, mod_x, mod_c)


def _mm_kernel(a_ref, w_ref, o_ref):
    o_ref[...] = jnp.dot(a_ref[...], w_ref[...], preferred_element_type=F32)


def _matmul(a, w, tm_pref, tn_pref):
    m, k = a.shape
    n = w.shape[1]
    tm = _pick(m, tm_pref, BF16_SUBLANES)
    tn = _pick(n, tn_pref, LANES)
    return pl.pallas_call(
        _mm_kernel,
        grid=(m // tm, n // tn),
        in_specs=[pl.BlockSpec((tm, k), lambda i, j: (i, 0)),
                  pl.BlockSpec((k, tn), lambda i, j: (0, j))],
        out_specs=pl.BlockSpec((tm, tn), lambda i, j: (i, j)),
        out_shape=jax.ShapeDtypeStruct((m, n), F32),
        compiler_params=_params(("arbitrary", "arbitrary"),
                                tm * k * 2 + k * tn * 2 + tm * tn * 4, scratch_bytes=tm * tn * 4),
        name="in_proj",
    )(a, w)


def _rope_tables(s, c):
    pos = jnp.arange(s, dtype=jnp.int32)
    r = (pos // GRID_W).astype(F32)[:, None]
    col = (pos % GRID_W).astype(F32)[:, None]
    lane = jnp.arange(LANES, dtype=jnp.int32)

    def table(head_dim):
        axis_dim = head_dim // 2
        pair = axis_dim // 2
        within = lane % head_dim
        f = (within % pair).astype(F32)
        freq = ROPE_BASE ** (-(2.0 * f) / axis_dim)
        use_row = (within // axis_dim) == 0
        ang = jnp.where(use_row[None, :], r * freq[None, :], col * freq[None, :])
        first = (within % axis_dim) < pair
        cos = jnp.cos(ang)
        sin = jnp.where(first[None, :], -jnp.sin(ang), jnp.sin(ang))
        cos = jnp.concatenate([cos, jnp.ones((c, LANES), F32)], axis=0)
        sin = jnp.concatenate([sin, jnp.zeros((c, LANES), F32)], axis=0)
        return cos, sin

    cd, sd = table(DA_QK_DIM)
    cw, sw = table(WA_HEAD_DIM)
    return jnp.stack([cd, sd, cw, sw], axis=0)


def _headprep_kernel(pr_ref, tab_ref, gain_ref, ones_da_ref, ones_wa_ref, o_ref, vt_ref,
                     *, n_qk_da, n_v_da, n_q_wa, n_k_wa, n_v_wa):
    tm = pr_ref.shape[0]
    lane = lax.broadcasted_iota(jnp.int32, (tm, LANES), 1)
    first_da = (lane % (DA_QK_DIM // 2)) < (DA_QK_DIM // 4)
    first_wa = (lane % (WA_HEAD_DIM // 2)) < (WA_HEAD_DIM // 4)
    cos_da, sin_da = tab_ref[0], tab_ref[1]
    cos_wa, sin_wa = tab_ref[2], tab_ref[3]

    def rope(y, cos, sin, first, shift):
        partner = jnp.where(first, pltpu.roll(y, LANES - shift, 1), pltpu.roll(y, shift, 1))
        return y * cos + partner * sin

    def head_sum_sq(x, ones_ref):
        x2 = x * x
        hi = x2.astype(BF16)
        mid = (x2 - hi.astype(F32)).astype(BF16)
        return jnp.dot(jnp.concatenate([hi, mid], axis=1), ones_ref[...], preferred_element_type=F32)

    def da_group(gidx, gain):
        x = pr_ref[:, gidx * LANES:(gidx + 1) * LANES]
        ms = head_sum_sq(x, ones_da_ref) * (1.0 / DA_QK_DIM)
        y = x * lax.rsqrt(ms + EPS) * gain
        y = rope(y, cos_da, sin_da, first_da, DA_QK_DIM // 4)
        o_ref[:, gidx * LANES:(gidx + 1) * LANES] = y.astype(BF16)

    def wa_group(gidx, gain):
        x = pr_ref[:, gidx * LANES:(gidx + 1) * LANES]
        ms = head_sum_sq(x, ones_wa_ref) * (1.0 / WA_HEAD_DIM)
        y = x * lax.rsqrt(ms + EPS) * gain
        y = rope(y, cos_wa, sin_wa, first_wa, WA_HEAD_DIM // 4)
        o_ref[:, gidx * LANES:(gidx + 1) * LANES] = y.astype(BF16)

    def copy_group(gidx):
        o_ref[:, gidx * LANES:(gidx + 1) * LANES] = pr_ref[:, gidx * LANES:(gidx + 1) * LANES].astype(BF16)

    g = 0
    for _ in range(n_qk_da):
        da_group(g, gain_ref[0:1, :]); g += 1
    for _ in range(n_qk_da):
        da_group(g, gain_ref[1:2, :]); g += 1
    for hd in range(n_v_da):
        vt_ref[hd * LANES:(hd + 1) * LANES, :] = pr_ref[:, g * LANES:(g + 1) * LANES].T.astype(BF16)
        copy_group(g); g += 1
    for _ in range(n_q_wa):
        wa_group(g, gain_ref[2:3, :]); g += 1
    for _ in range(n_k_wa):
        wa_group(g, gain_ref[3:4, :]); g += 1
    for _ in range(n_v_wa):
        copy_group(g); g += 1


def _headprep(pr, tabs, gains, da_heads, wa_q_heads, tm_pref=128):
    rows = pr.shape[0]
    wa_kv_heads = wa_q_heads // WA_GROUP
    groups = dict(n_qk_da=da_heads, n_v_da=da_heads, n_q_wa=wa_q_heads,
                  n_k_wa=wa_kv_heads, n_v_wa=wa_kv_heads)
    width = (3 * da_heads + wa_q_heads + 2 * wa_kv_heads) * LANES
    tm = _pick(rows, tm_pref, BF16_SUBLANES)
    lane = jnp.arange(LANES)
    same_da_head = (lane[:, None] // DA_QK_DIM) == (lane[None, :] // DA_QK_DIM)
    ones_da = jnp.tile(same_da_head.astype(BF16), (2, 1))
    ones_wa = jnp.ones((2 * LANES, LANES), BF16)
    return pl.pallas_call(
        functools.partial(_headprep_kernel, **groups),
        grid=(rows // tm,),
        in_specs=[pl.BlockSpec((tm, width), lambda i: (i, 0)),
                  pl.BlockSpec((4, tm, LANES), lambda i: (0, i, 0)),
                  pl.BlockSpec((4, LANES), lambda i: (0, 0)),
                  pl.BlockSpec((2 * LANES, LANES), lambda i: (0, 0)),
                  pl.BlockSpec((2 * LANES, LANES), lambda i: (0, 0))],
        out_specs=[pl.BlockSpec((tm, width), lambda i: (i, 0)),
                   pl.BlockSpec((da_heads * LANES, tm), lambda i: (0, i))],
        out_shape=[jax.ShapeDtypeStruct((rows, width), BF16),
                   jax.ShapeDtypeStruct((da_heads * LANES, rows), BF16)],
        compiler_params=_params(("arbitrary",),
                                tm * width * 6 + 4 * tm * LANES * 4 + da_heads * LANES * tm * 2),
        name="headprep",
    )(pr, tabs, gains, ones_da, ones_wa)


def _nt_dot(a, b):
    return lax.dot_general(a, b, (((1,), (1,)), ((), ())), preferred_element_type=F32)


STEPS_PER_TRIP = 4
LOG2_E = math.log2(math.e)


def _diffattn_kernel(lam_ref, q_ref, k_ref, vt_ref, g_ref, o_ref,
                     s_ref, p_ref, m_ref, al_ref, acc_ref, *, tk, rb):
    tq = q_ref.shape[0]
    nk = k_ref.shape[0] // tk
    q = q_ref[...]
    lane = lax.broadcasted_iota(jnp.int32, (tq, LANES), 1)
    zero = jnp.zeros_like(q)
    qs = (jnp.where(lane < DA_QK_DIM, q, zero), jnp.where(lane < DA_QK_DIM, zero, q))
    ones = jnp.ones((BF16_SUBLANES, tk), BF16)

    def scores(c, slot):
        k = k_ref[pl.ds(pl.multiple_of(c * tk, tk), tk), :]
        for h in range(2):
            s_ref[slot, h] = _nt_dot(k, qs[h]).astype(BF16)

    def softmax(slot):
        for h in range(2):
            for j in range(0, tq, LANES):
                m_old = m_ref[h, :, j:j + LANES]
                mx = s_ref[slot, h, 0:rb, j:j + LANES]
                for r in range(rb, tk, rb):
                    mx = jnp.maximum(mx, s_ref[slot, h, r:r + rb, j:j + LANES])
                m_new = jnp.maximum(m_old, jnp.max(mx.astype(F32), axis=0, keepdims=True))
                m_ref[h, :, j:j + LANES] = m_new
                al_ref[slot, h, :, j:j + LANES] = jnp.exp2(m_old - m_new)
                mb = m_new.astype(BF16)
                for r in range(0, tk, rb):
                    p_ref[slot, h, r:r + rb, j:j + LANES] = jnp.exp2(
                        s_ref[slot, h, r:r + rb, j:j + LANES] - mb)

    def accumulate(c, slot):
        vt = vt_ref[:, pl.ds(pl.multiple_of(c * tk, tk), tk)]
        v1 = jnp.concatenate([vt, ones], axis=0)
        for h in range(2):
            acc_ref[h] = (al_ref[slot, h] * acc_ref[h]
                          + jnp.dot(v1, p_ref[slot, h], preferred_element_type=F32))

    def step(c, slot, with_softmax=True, with_scores=True):
        accumulate(c, slot)
        if with_softmax:
            softmax(1 - slot)
        if with_scores:
            scores(c + 2, slot)

    m_ref[...] = jnp.full(m_ref.shape, NEG_INF, F32)
    acc_ref[...] = jnp.zeros(acc_ref.shape, F32)
    scores(0, 0)
    scores(1, 1)
    softmax(0)

    def group(i, carry):
        for t in range(STEPS_PER_TRIP):
            step(STEPS_PER_TRIP * i + t, t % 2)
        return carry

    n_full = nk - 2
    lax.fori_loop(0, n_full // STEPS_PER_TRIP, group, 0)
    c = STEPS_PER_TRIP * (n_full // STEPS_PER_TRIP)
    while c < n_full:
        step(c, c % 2)
        c += 1
    step(c, c % 2, with_scores=False)
    step(c + 1, (c + 1) % 2, with_softmax=False, with_scores=False)
    a1, a2 = acc_ref[0], acc_ref[1]

    lam = (jnp.exp(jnp.sum(lam_ref[0:1, :] * lam_ref[1:2, :], axis=-1, keepdims=True))
           - jnp.exp(jnp.sum(lam_ref[2:3, :] * lam_ref[3:4, :], axis=-1, keepdims=True))
           + LAM_INIT)
    o = (a1[:DA_V_DIM] / a1[DA_V_DIM:DA_V_DIM + 1]) - lam * (a2[:DA_V_DIM] / a2[DA_V_DIM:DA_V_DIM + 1])
    ms = jnp.mean(o * o, axis=0, keepdims=True)
    o = o * lax.rsqrt(ms + EPS) * g_ref[...] * (1.0 - LAM_INIT)
    o_ref[...] = o.T.astype(BF16)


def _diffattn(qkv, vt, lam_vecs, subln_g, s, da_heads, tq_pref=256, tk_pref=1280):
    rows = qkv.shape[0]
    tq = _pick(s, tq_pref, LANES)
    tk = _pick(rows, min(tk_pref, rows // 2), LANES)
    kb = da_heads
    acc_rows = DA_V_DIM + BF16_SUBLANES
    return pl.pallas_call(
        functools.partial(_diffattn_kernel, tk=tk, rb=8 * BF16_SUBLANES),
        grid=(da_heads, s // tq),
        in_specs=[pl.BlockSpec((4, DA_QK_DIM), lambda h, i: (0, 0)),
                  pl.BlockSpec((tq, LANES), lambda h, i: (i, h)),
                  pl.BlockSpec((rows, LANES), lambda h, i: (0, kb + h)),
                  pl.BlockSpec((DA_V_DIM, rows), lambda h, i: (h, 0)),
                  pl.BlockSpec((DA_V_DIM, 1), lambda h, i: (0, 0))],
        out_specs=pl.BlockSpec((tq, DA_V_DIM), lambda h, i: (i, h)),
        out_shape=jax.ShapeDtypeStruct((s, da_heads * DA_V_DIM), BF16),
        scratch_shapes=[pltpu.VMEM((2, 2, tk, tq), BF16),
                        pltpu.VMEM((2, 2, tk, tq), BF16),
                        pltpu.VMEM((2, 1, tq), F32),
                        pltpu.VMEM((2, 2, 1, tq), F32),
                        pltpu.VMEM((2, acc_rows, tq), F32)],
        compiler_params=_params(("arbitrary", "arbitrary"),
                                2 * rows * LANES * 2 + 2 * tq * LANES * 2,
                                scratch_bytes=4 * tq * tk * 6 + 4 * tq * tk * 4),
        name="diff_attn",
    )(lam_vecs, qkv, qkv, vt, subln_g.reshape(DA_V_DIM, 1))


def _winattn_kernel(sink_ref, q_ref, k_ref, v_ref, o_ref, *, s, c):
    g = pl.program_id(0)
    i = pl.program_id(1)
    tq = q_ref.shape[0]
    band = 3 * WINDOW
    rows = WA_GROUP * WINDOW
    kc = k_ref[s:s + c, :]
    vc = v_ref[s:s + c, :]
    row = lax.broadcasted_iota(jnp.int32, (rows, band), 0)
    col = lax.broadcasted_iota(jnp.int32, (rows, band), 1)
    head = lax.broadcasted_iota(jnp.int32, (rows, 1), 0) // WINDOW
    sink = jnp.zeros((rows, 1), F32)
    for r in range(WA_GROUP):
        sink = jnp.where(head == r, sink_ref[g * WA_GROUP + r] * LOG2_E, sink)
    for b in range(tq // WINDOW):
        q0 = i * tq + b * WINDOW
        start = pl.multiple_of(jnp.clip(q0 - WINDOW, 0, s - band), WINDOW)
        kb = k_ref[pl.ds(start, band), :]
        vb = v_ref[pl.ds(start, band), :]
        valid = jnp.abs(q0 + row % WINDOW - (start + col)) <= WINDOW
        q = jnp.concatenate(
            [q_ref[b * WINDOW:(b + 1) * WINDOW, r * WA_HEAD_DIM:(r + 1) * WA_HEAD_DIM]
             for r in range(WA_GROUP)], axis=0)
        sb = jnp.where(valid, _nt_dot(q, kb), NEG_INF)
        sc = _nt_dot(q, kc)
        m = jnp.maximum(jnp.max(sb, axis=-1, keepdims=True), jnp.max(sc, axis=-1, keepdims=True))
        m = jnp.maximum(m, sink)
        pb = jnp.exp2(sb - m)
        pc = jnp.exp2(sc - m)
        l = (jnp.sum(pb, axis=-1, keepdims=True) + jnp.sum(pc, axis=-1, keepdims=True)
             + jnp.exp2(sink - m))
        o = (jnp.dot(pb.astype(BF16), vb, preferred_element_type=F32)
             + jnp.dot(pc.astype(BF16), vc, preferred_element_type=F32)) * (1.0 / l)
        for r in range(WA_GROUP):
            o_ref[b * WINDOW:(b + 1) * WINDOW, r * WA_HEAD_DIM:(r + 1) * WA_HEAD_DIM] = (
                o[r * WINDOW:(r + 1) * WINDOW].astype(BF16))


def _winattn(qkv, sink, s, c, da_heads, wa_q_heads, tq_pref=512):
    rows = qkv.shape[0]
    kvh = wa_q_heads // WA_GROUP
    tq = _pick(s, tq_pref, WINDOW)
    qb = 3 * da_heads // WA_GROUP
    kb = 3 * da_heads + wa_q_heads
    vb = kb + kvh
    gw = WA_GROUP * WA_HEAD_DIM
    return pl.pallas_call(
        functools.partial(_winattn_kernel, s=s, c=c),
        grid=(kvh, s // tq),
        in_specs=[pl.BlockSpec(memory_space=pltpu.SMEM),
                  pl.BlockSpec((tq, gw), lambda g, i: (i, qb + g)),
                  pl.BlockSpec((rows, LANES), lambda g, i: (0, kb + g)),
                  pl.BlockSpec((rows, LANES), lambda g, i: (0, vb + g))],
        out_specs=pl.BlockSpec((tq, gw), lambda g, i: (i, g)),
        out_shape=jax.ShapeDtypeStruct((s, wa_q_heads * WA_HEAD_DIM), BF16),
        compiler_params=_params(("arbitrary", "arbitrary"),
                                2 * rows * LANES * 2 + 2 * tq * gw * 2,
                                scratch_bytes=8 * tq * (tq + 2 * WINDOW + c) * 4),
        name="win_attn",
    )(sink, qkv, qkv, qkv)


def _merge_kernel(yd_ref, yw_ref, wd_ref, ww_ref, gd_ref, gw_ref, bd_ref, bw_ref, o_ref):
    pd = jnp.dot(yd_ref[...], wd_ref[...], preferred_element_type=F32)
    pw = jnp.dot(yw_ref[...], ww_ref[...], preferred_element_type=F32)
    gd = _sigmoid(gd_ref[...] + bd_ref[...])
    gw = _sigmoid(gw_ref[...] + bw_ref[...])
    o_ref[...] = (gd * pd + gw * pw).astype(BF16)


def _merge(y_da, y_wa, w_o_da, w_o_wa, pr, gate_off, b_gate, tm_pref=512, tn_pref=1024):
    s, kd = y_da.shape
    kw = y_wa.shape[1]
    d = w_o_da.shape[1]
    tm = _pick(s, tm_pref, BF16_SUBLANES)
    tn = _pick(math.gcd(d, gate_off), tn_pref, LANES)
    od, ow = gate_off // tn, (gate_off + d) // tn
    nb = d // tn
    bg = b_gate.reshape(1, N_BRANCH * d)
    return pl.pallas_call(
        _merge_kernel,
        grid=(s // tm, nb),
        in_specs=[pl.BlockSpec((tm, kd), lambda i, j: (i, 0)),
                  pl.BlockSpec((tm, kw), lambda i, j: (i, 0)),
                  pl.BlockSpec((kd, tn), lambda i, j: (0, j)),
                  pl.BlockSpec((kw, tn), lambda i, j: (0, j)),
                  pl.BlockSpec((tm, tn), lambda i, j: (i, od + j)),
                  pl.BlockSpec((tm, tn), lambda i, j: (i, ow + j)),
                  pl.BlockSpec((1, tn), lambda i, j: (0, j)),
                  pl.BlockSpec((1, tn), lambda i, j: (0, nb + j))],
        out_specs=pl.BlockSpec((tm, tn), lambda i, j: (i, j)),
        out_shape=jax.ShapeDtypeStruct((s, d), BF16),
        compiler_params=_params(("arbitrary", "arbitrary"),
                                tm * (kd + kw) * 2 + (kd + kw) * tn * 2 + 2 * tm * tn * 4 + tm * tn * 2,
                                scratch_bytes=4 * tm * tn * 4),
        name="merge",
    )(y_da, y_wa, w_o_da, w_o_wa, pr, pr, bg, bg)


def _mm_res_kernel(a_ref, w_ref, x_ref, g_ref, o_ref, acc_ref):
    kk = pl.program_id(2)

    @pl.when(kk == 0)
    def _():
        acc_ref[...] = jnp.zeros_like(acc_ref)

    acc_ref[...] += jnp.dot(a_ref[...], w_ref[...], preferred_element_type=F32)

    @pl.when(kk == pl.num_programs(2) - 1)
    def _():
        o_ref[...] = x_ref[...] + g_ref[...] * acc_ref[...]


def _matmul_residual(a, w, x, gate, tm_pref, tn_pref, tk_pref, name):
    m, k = a.shape
    n = w.shape[1]
    tm = _pick(m, tm_pref, BF16_SUBLANES)
    tn = _pick(n, tn_pref, LANES)
    tk = _pick(k, tk_pref, LANES)
    return pl.pallas_call(
        _mm_res_kernel,
        grid=(m // tm, n // tn, k // tk),
        in_specs=[pl.BlockSpec((tm, tk), lambda i, j, kk: (i, kk)),
                  pl.BlockSpec((tk, tn), lambda i, j, kk: (kk, j)),
                  pl.BlockSpec((tm, tn), lambda i, j, kk: (i, j)),
                  pl.BlockSpec((1, tn), lambda i, j, kk: (0, j))],
        out_specs=pl.BlockSpec((tm, tn), lambda i, j, kk: (i, j)),
        out_shape=jax.ShapeDtypeStruct((m, n), F32),
        scratch_shapes=[pltpu.VMEM((tm, tn), F32)],
        compiler_params=_params(("arbitrary", "arbitrary", "arbitrary"),
                                tm * tk * 2 + tk * tn * 2 + 2 * tm * tn * 4,
                                scratch_bytes=2 * tm * tn * 4),
        name=name,
    )(a, w, x, gate)


HALO = BF16_SUBLANES


def _ffn_up_kernel(h_ref, hp_ref, hn_ref, wa_ref, wu_ref, cw_ref, cb_ref, o_ref, lhs_ref):
    i = pl.program_id(0)
    j = pl.program_id(1)
    tm = h_ref.shape[0]

    @pl.when(j == 0)
    def _():
        prev = jnp.where(i > 0, hp_ref[...], jnp.zeros_like(hp_ref))
        nxt = jnp.where(i < pl.num_programs(0) - 1, hn_ref[...], jnp.zeros_like(hn_ref))
        lhs_ref[0:HALO, :] = prev
        lhs_ref[HALO:HALO + tm, :] = h_ref[...]
        lhs_ref[HALO + tm:, :] = nxt

    a = jnp.dot(lhs_ref[...], wa_ref[...], preferred_element_type=F32)
    u = jnp.dot(lhs_ref[HALO:HALO + tm, :], wu_ref[...], preferred_element_type=F32)
    rows = tm + 2 * HALO
    a_prev = pltpu.roll(a, 1, 0)[HALO:HALO + tm, :]
    a_next = pltpu.roll(a, rows - 1, 0)[HALO:HALO + tm, :]
    a_mid = a[HALO:HALO + tm, :]
    conv = cb_ref[...] + a_prev * cw_ref[0:1, :] + a_mid * cw_ref[1:2, :] + a_next * cw_ref[2:3, :]
    o_ref[...] = (conv * _sigmoid(conv) * u).astype(BF16)


def _ffn_up(h, w_gate, w_lin, conv_w, conv_b, tm_pref=1024, tn_pref=512):
    s, d = h.shape
    ffp = w_gate.shape[1]
    tm = _pick(s, tm_pref, HALO)
    tn = _pick(ffp, tn_pref, LANES)
    nf = ffp // tn
    per = tm // HALO
    last = s // HALO - 1
    return pl.pallas_call(
        _ffn_up_kernel,
        grid=(s // tm, nf),
        in_specs=[pl.BlockSpec((tm, d), lambda i, j: (i, 0)),
                  pl.BlockSpec((HALO, d), lambda i, j: (jnp.maximum(i * per - 1, 0), 0)),
                  pl.BlockSpec((HALO, d), lambda i, j: (jnp.minimum((i + 1) * per, last), 0)),
                  pl.BlockSpec((d, tn), lambda i, j: (0, j)),
                  pl.BlockSpec((d, tn), lambda i, j: (0, j)),
                  pl.BlockSpec((CONV_W, tn), lambda i, j: (0, j)),
                  pl.BlockSpec((1, tn), lambda i, j: (0, j))],
        out_specs=pl.BlockSpec((tm, tn), lambda i, j: (i, j)),
        out_shape=jax.ShapeDtypeStruct((s, ffp), BF16),
        scratch_shapes=[pltpu.VMEM((tm + 2 * HALO, d), BF16)],
        compiler_params=_params(("arbitrary", "arbitrary"),
                                (tm + 2 * HALO) * d * 2 + 2 * d * tn * 2 + tm * tn * 2,
                                scratch_bytes=(tm + 2 * HALO) * d * 2 + 6 * (tm + 2 * HALO) * tn * 4),
        name="ffn_up",
    )(h, h, h, w_gate, w_lin, conv_w, conv_b)


def kernel(x, c, ctx, c_ctx, w_ada, b_ada, attn_norm_g, w_in, b_gate, da_qn_g, da_kn_g, da_lambda_q1, da_lambda_k1, da_lambda_q2, da_lambda_k2, da_subln_g, wa_qn_g, wa_kn_g, wa_sink, w_o_da, w_o_wa, w_out, ffn_norm_g, w_ffn_up, ffn_conv_w, ffn_conv_b, w_ffn_down):
    b, s, d = x.shape
    cl = ctx.shape[1]
    assert b == 1 and w_in.shape[0] == 1, "one batch element, one layer"
    da_heads = w_o_da.shape[1] // DA_V_DIM
    wa_q_heads = w_o_wa.shape[1] // WA_HEAD_DIM
    wa_kv_heads = wa_q_heads // WA_GROUP
    d_ff = ffn_conv_b.shape[1]
    gate_off = (3 * da_heads + wa_q_heads + 2 * wa_kv_heads) * LANES
    assert w_in.shape[2] == gate_off + N_BRANCH * d

    x2 = x.reshape(s, d)
    ctx2 = ctx.reshape(cl, d)

    ffp = -(-d_ff // 1024) * 1024 if d_ff > 1024 else d_ff
    padf = ffp - d_ff
    w_in_b = w_in[0].astype(BF16)
    w_o_da_b = w_o_da[0].astype(BF16)
    w_o_wa_b = w_o_wa[0].astype(BF16)
    w_out_b = w_out[0].astype(BF16)
    w_gate_b = jnp.pad(w_ffn_up[0][:, :d_ff].astype(BF16), ((0, 0), (0, padf)))
    w_lin_b = jnp.pad(w_ffn_up[0][:, d_ff:].astype(BF16), ((0, 0), (0, padf)))
    w_down_b = jnp.pad(w_ffn_down[0].astype(BF16), ((0, padf), (0, 0)))
    conv_w = jnp.pad(ffn_conv_w[0], ((0, 0), (0, padf)))
    conv_b = jnp.pad(ffn_conv_b[0], (0, padf)).reshape(1, ffp)

    cc = jnp.concatenate([c, c_ctx[None, :], jnp.zeros((6, d), F32)], axis=0)
    mod = _adaln(cc, w_ada[0], b_ada[0])
    sh1, sc1, g1 = mod[0:1, 0:d], mod[0:1, d:2 * d], mod[0:1, 2 * d:3 * d]
    sh2, sc2, g2 = mod[0:1, 3 * d:4 * d], mod[0:1, 4 * d:5 * d], mod[0:1, 5 * d:6 * d]
    mod_x1 = jnp.concatenate([sh1, sc1], axis=0)
    mod_c1 = jnp.concatenate([mod[1:2, 0:d], mod[1:2, d:2 * d]], axis=0)
    mod_x2 = jnp.concatenate([sh2, sc2], axis=0)

    h = _normmod(x2, attn_norm_g[0], mod_x1, ctx2, mod_c1)
    pr = _matmul(h, w_in_b, 1280, 512)
    tabs = _rope_tables(s, cl)
    gains = jnp.stack([jnp.tile(da_qn_g[0] * (DA_SCALE * LOG2_E), 2), jnp.tile(da_kn_g[0], 2),
                       wa_qn_g[0] * (WA_SCALE * LOG2_E), wa_kn_g[0]], axis=0)
    qkv, vt = _headprep(pr, tabs, gains, da_heads, wa_q_heads)
    lam_vecs = jnp.stack([da_lambda_q1[0], da_lambda_k1[0], da_lambda_q2[0], da_lambda_k2[0]], axis=0)
    y_da = _diffattn(qkv, vt, lam_vecs, da_subln_g[0], s, da_heads)
    y_wa = _winattn(qkv, wa_sink[0], s, cl, da_heads, wa_q_heads)
    u = _merge(y_da, y_wa, w_o_da_b, w_o_wa_b, pr, gate_off, b_gate[0])
    x1 = _matmul_residual(u, w_out_b, x2, g1, 1024, 1024, d, "out_proj")

    h2 = _normmod(x1, ffn_norm_g[0], mod_x2)
    act = _ffn_up(h2, w_gate_b, w_lin_b, conv_w, conv_b)
    out = _matmul_residual(act, w_down_b, x1, g2, 1024, 1024, 2816, "ffn_down")
    return out.reshape(b, s, d)
```

```python
import functools
import math

import jax
import jax.numpy as jnp
from jax import lax
from jax.experimental import pallas as pl
from jax.experimental.pallas import tpu as pltpu

F32 = jnp.float32
BF16 = jnp.bfloat16

GRID_W = 64
DA_QK_DIM = 64
DA_V_DIM = 2 * DA_QK_DIM
WA_HEAD_DIM = 128
WA_GROUP = 4
WINDOW = 128
N_BRANCH = 2
CONV_W = 3
ROPE_BASE = 10000.0
EPS = 1e-6
DA_SCALE = DA_QK_DIM ** -0.5
WA_SCALE = WA_HEAD_DIM ** -0.5
NEG_INF = -1e30
LAM_INIT = 0.8 - 0.6 * math.exp(-0.3 * 0)

LANES = 128
BF16_SUBLANES = 16
V7X_VMEM_CAP_BYTES = 58 * 1024 * 1024
VMEM_SLACK_BYTES = 6 * 1024 * 1024


def _pick(dim, pref, unit):
    best = None
    t = unit
    while t <= min(dim, pref):
        if dim % t == 0:
            best = t
        t += unit
    if best is None:
        raise ValueError(f"no tile for dim={dim} unit={unit}")
    return best


def _params(semantics, block_bytes, scratch_bytes=0):
    est = 2 * block_bytes + scratch_bytes + VMEM_SLACK_BYTES
    return pltpu.CompilerParams(
        dimension_semantics=semantics,
        vmem_limit_bytes=min(max(est, 16 * 1024 * 1024), V7X_VMEM_CAP_BYTES))


def _sigmoid(x):
    return 1.0 / (1.0 + jnp.exp(-x))


def _adaln_kernel(c_ref, w_ref, b_ref, o_ref):
    a = c_ref[...]
    a = a * _sigmoid(a)
    o_ref[...] = jnp.dot(a.astype(BF16), w_ref[...].astype(BF16),
                         preferred_element_type=F32) + b_ref[...]


def _adaln(cc, w, b):
    rows, d = cc.shape
    n = w.shape[1]
    tn = _pick(n, 512, LANES)
    return pl.pallas_call(
        _adaln_kernel,
        grid=(n // tn,),
        in_specs=[pl.BlockSpec((rows, d), lambda j: (0, 0)),
                  pl.BlockSpec((d, tn), lambda j: (0, j)),
                  pl.BlockSpec((1, tn), lambda j: (0, j))],
        out_specs=pl.BlockSpec((rows, tn), lambda j: (0, j)),
        out_shape=jax.ShapeDtypeStruct((rows, n), F32),
        compiler_params=_params(("arbitrary",), d * tn * 4 + d * tn * 2),
        name="adaln",
    )(cc, w, b.reshape(1, n))


def _normmod_rows(x, g, mod):
    ms = jnp.mean(x * x, axis=-1, keepdims=True)
    y = x * lax.rsqrt(ms + EPS) * g
    return (y * (1.0 + mod[1:2, :]) + mod[0:1, :]).astype(BF16)


def _normmod2_kernel(x_ref, c_ref, g_ref, mx_ref, mc_ref, o_ref, *, nx):
    i = pl.program_id(0)

    @pl.when(i < nx)
    def _():
        o_ref[...] = _normmod_rows(x_ref[...], g_ref[...], mx_ref[...])

    @pl.when(i >= nx)
    def _():
        o_ref[...] = _normmod_rows(c_ref[...], g_ref[...], mc_ref[...])


def _normmod1_kernel(x_ref, g_ref, mx_ref, o_ref):
    o_ref[...] = _normmod_rows(x_ref[...], g_ref[...], mx_ref[...])


def _normmod(x, g, mod_x, ctx=None, mod_c=None):
    s, d = x.shape
    g = g.reshape(1, d)
    if ctx is None:
        tm = _pick(s, 256, BF16_SUBLANES)
        return pl.pallas_call(
            _normmod1_kernel,
            grid=(s // tm,),
            in_specs=[pl.BlockSpec((tm, d), lambda i: (i, 0)),
                      pl.BlockSpec((1, d), lambda i: (0, 0)),
                      pl.BlockSpec((2, d), lambda i: (0, 0))],
            out_specs=pl.BlockSpec((tm, d), lambda i: (i, 0)),
            out_shape=jax.ShapeDtypeStruct((s, d), BF16),
            compiler_params=_params(("arbitrary",), tm * d * 6),
            name="normmod",
        )(x, g, mod_x)
    c = ctx.shape[0]
    tm = _pick(math.gcd(s, c), 256, BF16_SUBLANES)
    nx, nc = s // tm, c // tm
    return pl.pallas_call(
        functools.partial(_normmod2_kernel, nx=nx),
        grid=(nx + nc,),
        in_specs=[pl.BlockSpec((tm, d), lambda i: (jnp.minimum(i, nx - 1), 0)),
                  pl.BlockSpec((tm, d), lambda i: (jnp.maximum(i - nx, 0), 0)),
                  pl.BlockSpec((1, d), lambda i: (0, 0)),
                  pl.BlockSpec((2, d), lambda i: (0, 0)),
                  pl.BlockSpec((2, d), lambda i: (0, 0))],
        out_specs=pl.BlockSpec((tm, d), lambda i: (i, 0)),
        out_shape=jax.ShapeDtypeStruct((s + c, d), BF16),
        compiler_params=_params(("arbitrary",), tm * d * 10),
        name="normmod_xc",
    )(x, ctx, g, mod_x, mod_c)


def _mm_kernel(a_ref, w_ref, o_ref):
    o_ref[...] = jnp.dot(a_ref[...], w_ref[...], preferred_element_type=F32)


def _matmul(a, w, tm_pref, tn_pref):
    m, k = a.shape
    n = w.shape[1]
    tm = _pick(m, tm_pref, BF16_SUBLANES)
    tn = _pick(n, tn_pref, LANES)
    return pl.pallas_call(
        _mm_kernel,
        grid=(m // tm, n // tn),
        in_specs=[pl.BlockSpec((tm, k), lambda i, j: (i, 0)),
                  pl.BlockSpec((k, tn), lambda i, j: (0, j))],
        out_specs=pl.BlockSpec((tm, tn), lambda i, j: (i, j)),
        out_shape=jax.ShapeDtypeStruct((m, n), F32),
        compiler_params=_params(("arbitrary", "arbitrary"),
                                tm * k * 2 + k * tn * 2 + tm * tn * 4, scratch_bytes=tm * tn * 4),
        name="in_proj",
    )(a, w)


def _rope_tables(s, c):
    pos = jnp.arange(s, dtype=jnp.int32)
    r = (pos // GRID_W).astype(F32)[:, None]
    col = (pos % GRID_W).astype(F32)[:, None]
    lane = jnp.arange(LANES, dtype=jnp.int32)

    def table(head_dim):
        axis_dim = head_dim // 2
        pair = axis_dim // 2
        within = lane % head_dim
        f = (within % pair).astype(F32)
        freq = ROPE_BASE ** (-(2.0 * f) / axis_dim)
        use_row = (within // axis_dim) == 0
        ang = jnp.where(use_row[None, :], r * freq[None, :], col * freq[None, :])
        first = (within % axis_dim) < pair
        cos = jnp.cos(ang)
        sin = jnp.where(first[None, :], -jnp.sin(ang), jnp.sin(ang))
        cos = jnp.concatenate([cos, jnp.ones((c, LANES), F32)], axis=0)
        sin = jnp.concatenate([sin, jnp.zeros((c, LANES), F32)], axis=0)
        return cos, sin

    cd, sd = table(DA_QK_DIM)
    cw, sw = table(WA_HEAD_DIM)
    return jnp.stack([cd, sd, cw, sw], axis=0)


def _headprep_kernel(pr_ref, tab_ref, gain_ref, ones_da_ref, ones_wa_ref, o_ref,
                     *, n_qk_da, n_v_da, n_q_wa, n_k_wa, n_v_wa):
    tm = pr_ref.shape[0]
    lane = lax.broadcasted_iota(jnp.int32, (tm, LANES), 1)
    first_da = (lane % (DA_QK_DIM // 2)) < (DA_QK_DIM // 4)
    first_wa = (lane % (WA_HEAD_DIM // 2)) < (WA_HEAD_DIM // 4)
    cos_da, sin_da = tab_ref[0], tab_ref[1]
    cos_wa, sin_wa = tab_ref[2], tab_ref[3]

    def rope(y, cos, sin, first, shift):
        partner = jnp.where(first, pltpu.roll(y, LANES - shift, 1), pltpu.roll(y, shift, 1))
        return y * cos + partner * sin

    def head_sum_sq(x, ones_ref):
        x2 = x * x
        hi = x2.astype(BF16)
        mid = (x2 - hi.astype(F32)).astype(BF16)
        return jnp.dot(jnp.concatenate([hi, mid], axis=1), ones_ref[...], preferred_element_type=F32)

    def da_group(gidx, gain):
        x = pr_ref[:, gidx * LANES:(gidx + 1) * LANES]
        ms = head_sum_sq(x, ones_da_ref) * (1.0 / DA_QK_DIM)
        y = x * lax.rsqrt(ms + EPS) * gain
        y = rope(y, cos_da, sin_da, first_da, DA_QK_DIM // 4)
        o_ref[:, gidx * LANES:(gidx + 1) * LANES] = y.astype(BF16)

    def wa_group(gidx, gain):
        x = pr_ref[:, gidx * LANES:(gidx + 1) * LANES]
        ms = head_sum_sq(x, ones_wa_ref) * (1.0 / WA_HEAD_DIM)
        y = x * lax.rsqrt(ms + EPS) * gain
        y = rope(y, cos_wa, sin_wa, first_wa, WA_HEAD_DIM // 4)
        o_ref[:, gidx * LANES:(gidx + 1) * LANES] = y.astype(BF16)

    def copy_group(gidx):
        o_ref[:, gidx * LANES:(gidx + 1) * LANES] = pr_ref[:, gidx * LANES:(gidx + 1) * LANES].astype(BF16)

    g = 0
    for _ in range(n_qk_da):
        da_group(g, gain_ref[0:1, :]); g += 1
    for _ in range(n_qk_da):
        da_group(g, gain_ref[1:2, :]); g += 1
    for _ in range(n_v_da):
        copy_group(g); g += 1
    for _ in range(n_q_wa):
        wa_group(g, gain_ref[2:3, :]); g += 1
    for _ in range(n_k_wa):
        wa_group(g, gain_ref[3:4, :]); g += 1
    for _ in range(n_v_wa):
        copy_group(g); g += 1


def _headprep(pr, tabs, gains, da_heads, wa_q_heads, tm_pref=256):
    rows = pr.shape[0]
    wa_kv_heads = wa_q_heads // WA_GROUP
    groups = dict(n_qk_da=da_heads, n_v_da=da_heads, n_q_wa=wa_q_heads,
                  n_k_wa=wa_kv_heads, n_v_wa=wa_kv_heads)
    width = (3 * da_heads + wa_q_heads + 2 * wa_kv_heads) * LANES
    tm = _pick(rows, tm_pref, BF16_SUBLANES)
    lane = jnp.arange(LANES)
    same_da_head = (lane[:, None] // DA_QK_DIM) == (lane[None, :] // DA_QK_DIM)
    ones_da = jnp.tile(same_da_head.astype(BF16), (2, 1))
    ones_wa = jnp.ones((2 * LANES, LANES), BF16)
    return pl.pallas_call(
        functools.partial(_headprep_kernel, **groups),
        grid=(rows // tm,),
        in_specs=[pl.BlockSpec((tm, width), lambda i: (i, 0)),
                  pl.BlockSpec((4, tm, LANES), lambda i: (0, i, 0)),
                  pl.BlockSpec((4, LANES), lambda i: (0, 0)),
                  pl.BlockSpec((2 * LANES, LANES), lambda i: (0, 0)),
                  pl.BlockSpec((2 * LANES, LANES), lambda i: (0, 0))],
        out_specs=pl.BlockSpec((tm, width), lambda i: (i, 0)),
        out_shape=jax.ShapeDtypeStruct((rows, width), BF16),
        compiler_params=_params(("arbitrary",), tm * width * 6 + 4 * tm * LANES * 4),
        name="headprep",
    )(pr, tabs, gains, ones_da, ones_wa)


def _nt_dot(a, b):
    return lax.dot_general(a, b, (((1,), (1,)), ((), ())), preferred_element_type=F32)


STEPS_PER_TRIP = 4
LOG2_E = math.log2(math.e)


def _diffattn_kernel(lam_ref, q_ref, k_ref, v_ref, g_ref, o_ref,
                     s_ref, p_ref, m_ref, al_ref, acc_ref, *, tk, rb):
    tq = q_ref.shape[0]
    nk = k_ref.shape[0] // tk
    q = q_ref[...]
    lane = lax.broadcasted_iota(jnp.int32, (tq, LANES), 1)
    zero = jnp.zeros_like(q)
    qs = (jnp.where(lane < DA_QK_DIM, q, zero), jnp.where(lane < DA_QK_DIM, zero, q))
    ones = jnp.ones((tk, LANES), BF16)

    def scores(c, slot):
        k = k_ref[pl.ds(pl.multiple_of(c * tk, tk), tk), :]
        for h in range(2):
            s_ref[slot, h] = _nt_dot(qs[h], k).astype(BF16)

    def softmax(slot):
        for h in range(2):
            for r in range(0, tq, rb):
                s = s_ref[slot, h, r:r + rb, :]
                m_old = m_ref[h, r:r + rb, :]
                m_new = jnp.maximum(m_old, jnp.max(s, axis=-1, keepdims=True).astype(F32))
                p_ref[slot, h, r:r + rb, :] = jnp.exp2(s - m_new.astype(BF16))
                m_ref[h, r:r + rb, :] = m_new
                al_ref[slot, h, r:r + rb, :] = jnp.exp2(m_old - m_new)

    def accumulate(c, slot):
        v = v_ref[pl.ds(pl.multiple_of(c * tk, tk), tk), :]
        v1 = jnp.concatenate([v, ones], axis=1)
        for h in range(2):
            acc_ref[h] = (al_ref[slot, h] * acc_ref[h]
                          + jnp.dot(p_ref[slot, h], v1, preferred_element_type=F32))

    def step(c, slot, with_softmax=True, with_scores=True):
        if with_softmax:
            softmax(1 - slot)
        accumulate(c, slot)
        if with_scores:
            scores(c + 2, slot)

    m_ref[...] = jnp.full(m_ref.shape, NEG_INF, F32)
    acc_ref[...] = jnp.zeros(acc_ref.shape, F32)
    scores(0, 0)
    scores(1, 1)
    softmax(0)

    def group(i, carry):
        for t in range(STEPS_PER_TRIP):
            step(STEPS_PER_TRIP * i + t, t % 2)
        return carry

    n_full = nk - 2
    lax.fori_loop(0, n_full // STEPS_PER_TRIP, group, 0)
    c = STEPS_PER_TRIP * (n_full // STEPS_PER_TRIP)
    while c < n_full:
        step(c, c % 2)
        c += 1
    step(c, c % 2, with_scores=False)
    step(c + 1, (c + 1) % 2, with_softmax=False, with_scores=False)
    a1, a2 = acc_ref[0], acc_ref[1]

    lam = (jnp.exp(jnp.sum(lam_ref[0:1, :] * lam_ref[1:2, :], axis=-1, keepdims=True))
           - jnp.exp(jnp.sum(lam_ref[2:3, :] * lam_ref[3:4, :], axis=-1, keepdims=True))
           + LAM_INIT)
    o = (a1[:, :DA_V_DIM] / a1[:, DA_V_DIM:]) - lam * (a2[:, :DA_V_DIM] / a2[:, DA_V_DIM:])
    ms = jnp.mean(o * o, axis=-1, keepdims=True)
    o = o * lax.rsqrt(ms + EPS) * g_ref[...] * (1.0 - LAM_INIT)
    o_ref[...] = o.astype(BF16)


def _diffattn(qkv, lam_vecs, subln_g, s, da_heads, tq_pref=512, tk_pref=1280):
    rows = qkv.shape[0]
    tq = _pick(s, tq_pref, BF16_SUBLANES)
    tk = _pick(rows, min(tk_pref, rows // 2), LANES)
    kb, vb = da_heads, 2 * da_heads
    return pl.pallas_call(
        functools.partial(_diffattn_kernel, tk=tk, rb=BF16_SUBLANES),
        grid=(da_heads, s // tq),
        in_specs=[pl.BlockSpec((4, DA_QK_DIM), lambda h, i: (0, 0)),
                  pl.BlockSpec((tq, LANES), lambda h, i: (i, h)),
                  pl.BlockSpec((rows, LANES), lambda h, i: (0, kb + h)),
                  pl.BlockSpec((rows, LANES), lambda h, i: (0, vb + h)),
                  pl.BlockSpec((1, DA_V_DIM), lambda h, i: (0, 0))],
        out_specs=pl.BlockSpec((tq, DA_V_DIM), lambda h, i: (i, h)),
        out_shape=jax.ShapeDtypeStruct((s, da_heads * DA_V_DIM), BF16),
        scratch_shapes=[pltpu.VMEM((2, 2, tq, tk), BF16),
                        pltpu.VMEM((2, 2, tq, tk), BF16),
                        pltpu.VMEM((2, tq, 1), F32),
                        pltpu.VMEM((2, 2, tq, 1), F32),
                        pltpu.VMEM((2, tq, 2 * DA_V_DIM), F32)],
        compiler_params=_params(("arbitrary", "arbitrary"),
                                2 * rows * LANES * 2 + 2 * tq * LANES * 2,
                                scratch_bytes=4 * tq * tk * 6 + 4 * tq * tk * 4),
        name="diff_attn",
    )(lam_vecs, qkv, qkv, qkv, subln_g.reshape(1, DA_V_DIM))


def _winattn_kernel(sink_ref, q_ref, k_ref, v_ref, o_ref, *, s, c):
    g = pl.program_id(0)
    i = pl.program_id(1)
    tq = q_ref.shape[0]
    band = 3 * WINDOW
    rows = WA_GROUP * WINDOW
    kc = k_ref[s:s + c, :]
    vc = v_ref[s:s + c, :]
    row = lax.broadcasted_iota(jnp.int32, (rows, band), 0)
    col = lax.broadcasted_iota(jnp.int32, (rows, band), 1)
    head = lax.broadcasted_iota(jnp.int32, (rows, 1), 0) // WINDOW
    sink = jnp.zeros((rows, 1), F32)
    for r in range(WA_GROUP):
        sink = jnp.where(head == r, sink_ref[g * WA_GROUP + r] * LOG2_E, sink)
    for b in range(tq // WINDOW):
        q0 = i * tq + b * WINDOW
        start = pl.multiple_of(jnp.clip(q0 - WINDOW, 0, s - band), WINDOW)
        kb = k_ref[pl.ds(start, band), :]
        vb = v_ref[pl.ds(start, band), :]
        valid = jnp.abs(q0 + row % WINDOW - (start + col)) <= WINDOW
        q = jnp.concatenate(
            [q_ref[b * WINDOW:(b + 1) * WINDOW, r * WA_HEAD_DIM:(r + 1) * WA_HEAD_DIM]
             for r in range(WA_GROUP)], axis=0)
        sb = jnp.where(valid, _nt_dot(q, kb), NEG_INF)
        sc = _nt_dot(q, kc)
        m = jnp.maximum(jnp.max(sb, axis=-1, keepdims=True), jnp.max(sc, axis=-1, keepdims=True))
        m = jnp.maximum(m, sink)
        pb = jnp.exp2(sb - m)
        pc = jnp.exp2(sc - m)
        l = (jnp.sum(pb, axis=-1, keepdims=True) + jnp.sum(pc, axis=-1, keepdims=True)
             + jnp.exp2(sink - m))
        o = (jnp.dot(pb.astype(BF16), vb, preferred_element_type=F32)
             + jnp.dot(pc.astype(BF16), vc, preferred_element_type=F32)) * (1.0 / l)
        for r in range(WA_GROUP):
            o_ref[b * WINDOW:(b + 1) * WINDOW, r * WA_HEAD_DIM:(r + 1) * WA_HEAD_DIM] = (
                o[r * WINDOW:(r + 1) * WINDOW].astype(BF16))


def _winattn(qkv, sink, s, c, da_heads, wa_q_heads, tq_pref=512):
    rows = qkv.shape[0]
    kvh = wa_q_heads // WA_GROUP
    tq = _pick(s, tq_pref, WINDOW)
    qb = 3 * da_heads // WA_GROUP
    kb = 3 * da_heads + wa_q_heads
    vb = kb + kvh
    gw = WA_GROUP * WA_HEAD_DIM
    return pl.pallas_call(
        functools.partial(_winattn_kernel, s=s, c=c),
        grid=(kvh, s // tq),
        in_specs=[pl.BlockSpec(memory_space=pltpu.SMEM),
                  pl.BlockSpec((tq, gw), lambda g, i: (i, qb + g)),
                  pl.BlockSpec((rows, LANES), lambda g, i: (0, kb + g)),
                  pl.BlockSpec((rows, LANES), lambda g, i: (0, vb + g))],
        out_specs=pl.BlockSpec((tq, gw), lambda g, i: (i, g)),
        out_shape=jax.ShapeDtypeStruct((s, wa_q_heads * WA_HEAD_DIM), BF16),
        compiler_params=_params(("arbitrary", "arbitrary"),
                                2 * rows * LANES * 2 + 2 * tq * gw * 2,
                                scratch_bytes=8 * tq * (tq + 2 * WINDOW + c) * 4),
        name="win_attn",
    )(sink, qkv, qkv, qkv)


def _merge_kernel(yd_ref, yw_ref, wd_ref, ww_ref, gd_ref, gw_ref, bd_ref, bw_ref, o_ref):
    pd = jnp.dot(yd_ref[...], wd_ref[...], preferred_element_type=F32)
    pw = jnp.dot(yw_ref[...], ww_ref[...], preferred_element_type=F32)
    gd = _sigmoid(gd_ref[...] + bd_ref[...])
    gw = _sigmoid(gw_ref[...] + bw_ref[...])
    o_ref[...] = (gd * pd + gw * pw).astype(BF16)


def _merge(y_da, y_wa, w_o_da, w_o_wa, pr, gate_off, b_gate, tm_pref=512, tn_pref=1024):
    s, kd = y_da.shape
    kw = y_wa.shape[1]
    d = w_o_da.shape[1]
    tm = _pick(s, tm_pref, BF16_SUBLANES)
    tn = _pick(math.gcd(d, gate_off), tn_pref, LANES)
    od, ow = gate_off // tn, (gate_off + d) // tn
    nb = d // tn
    bg = b_gate.reshape(1, N_BRANCH * d)
    return pl.pallas_call(
        _merge_kernel,
        grid=(s // tm, nb),
        in_specs=[pl.BlockSpec((tm, kd), lambda i, j: (i, 0)),
                  pl.BlockSpec((tm, kw), lambda i, j: (i, 0)),
                  pl.BlockSpec((kd, tn), lambda i, j: (0, j)),
                  pl.BlockSpec((kw, tn), lambda i, j: (0, j)),
                  pl.BlockSpec((tm, tn), lambda i, j: (i, od + j)),
                  pl.BlockSpec((tm, tn), lambda i, j: (i, ow + j)),
                  pl.BlockSpec((1, tn), lambda i, j: (0, j)),
                  pl.BlockSpec((1, tn), lambda i, j: (0, nb + j))],
        out_specs=pl.BlockSpec((tm, tn), lambda i, j: (i, j)),
        out_shape=jax.ShapeDtypeStruct((s, d), BF16),
        compiler_params=_params(("arbitrary", "arbitrary"),
                                tm * (kd + kw) * 2 + (kd + kw) * tn * 2 + 2 * tm * tn * 4 + tm * tn * 2,
                                scratch_bytes=4 * tm * tn * 4),
        name="merge",
    )(y_da, y_wa, w_o_da, w_o_wa, pr, pr, bg, bg)


def _mm_res_kernel(a_ref, w_ref, x_ref, g_ref, o_ref, acc_ref):
    kk = pl.program_id(2)

    @pl.when(kk == 0)
    def _():
        acc_ref[...] = jnp.zeros_like(acc_ref)

    acc_ref[...] += jnp.dot(a_ref[...], w_ref[...], preferred_element_type=F32)

    @pl.when(kk == pl.num_programs(2) - 1)
    def _():
        o_ref[...] = x_ref[...] + g_ref[...] * acc_ref[...]


def _matmul_residual(a, w, x, gate, tm_pref, tn_pref, tk_pref, name):
    m, k = a.shape
    n = w.shape[1]
    tm = _pick(m, tm_pref, BF16_SUBLANES)
    tn = _pick(n, tn_pref, LANES)
    tk = _pick(k, tk_pref, LANES)
    return pl.pallas_call(
        _mm_res_kernel,
        grid=(m // tm, n // tn, k // tk),
        in_specs=[pl.BlockSpec((tm, tk), lambda i, j, kk: (i, kk)),
                  pl.BlockSpec((tk, tn), lambda i, j, kk: (kk, j)),
                  pl.BlockSpec((tm, tn), lambda i, j, kk: (i, j)),
                  pl.BlockSpec((1, tn), lambda i, j, kk: (0, j))],
        out_specs=pl.BlockSpec((tm, tn), lambda i, j, kk: (i, j)),
        out_shape=jax.ShapeDtypeStruct((m, n), F32),
        scratch_shapes=[pltpu.VMEM((tm, tn), F32)],
        compiler_params=_params(("arbitrary", "arbitrary", "arbitrary"),
                                tm * tk * 2 + tk * tn * 2 + 2 * tm * tn * 4,
                                scratch_bytes=2 * tm * tn * 4),
        name=name,
    )(a, w, x, gate)


HALO = BF16_SUBLANES


def _ffn_up_kernel(h_ref, hp_ref, hn_ref, wa_ref, wu_ref, cw_ref, cb_ref, o_ref, lhs_ref):
    i = pl.program_id(0)
    j = pl.program_id(1)
    tm = h_ref.shape[0]

    @pl.when(j == 0)
    def _():
        prev = jnp.where(i > 0, hp_ref[...], jnp.zeros_like(hp_ref))
        nxt = jnp.where(i < pl.num_programs(0) - 1, hn_ref[...], jnp.zeros_like(hn_ref))
        lhs_ref[0:HALO, :] = prev
        lhs_ref[HALO:HALO + tm, :] = h_ref[...]
        lhs_ref[HALO + tm:, :] = nxt

    a = jnp.dot(lhs_ref[...], wa_ref[...], preferred_element_type=F32)
    u = jnp.dot(lhs_ref[HALO:HALO + tm, :], wu_ref[...], preferred_element_type=F32)
    rows = tm + 2 * HALO
    a_prev = pltpu.roll(a, 1, 0)[HALO:HALO + tm, :]
    a_next = pltpu.roll(a, rows - 1, 0)[HALO:HALO + tm, :]
    a_mid = a[HALO:HALO + tm, :]
    conv = cb_ref[...] + a_prev * cw_ref[0:1, :] + a_mid * cw_ref[1:2, :] + a_next * cw_ref[2:3, :]
    o_ref[...] = (conv * _sigmoid(conv) * u).astype(BF16)


def _ffn_up(h, w_gate, w_lin, conv_w, conv_b, tm_pref=1024, tn_pref=512):
    s, d = h.shape
    ffp = w_gate.shape[1]
    tm = _pick(s, tm_pref, HALO)
    tn = _pick(ffp, tn_pref, LANES)
    nf = ffp // tn
    per = tm // HALO
    last = s // HALO - 1
    return pl.pallas_call(
        _ffn_up_kernel,
        grid=(s // tm, nf),
        in_specs=[pl.BlockSpec((tm, d), lambda i, j: (i, 0)),
                  pl.BlockSpec((HALO, d), lambda i, j: (jnp.maximum(i * per - 1, 0), 0)),
                  pl.BlockSpec((HALO, d), lambda i, j: (jnp.minimum((i + 1) * per, last), 0)),
                  pl.BlockSpec((d, tn), lambda i, j: (0, j)),
                  pl.BlockSpec((d, tn), lambda i, j: (0, j)),
                  pl.BlockSpec((CONV_W, tn), lambda i, j: (0, j)),
                  pl.BlockSpec((1, tn), lambda i, j: (0, j))],
        out_specs=pl.BlockSpec((tm, tn), lambda i, j: (i, j)),
        out_shape=jax.ShapeDtypeStruct((s, ffp), BF16),
        scratch_shapes=[pltpu.VMEM((tm + 2 * HALO, d), BF16)],
        compiler_params=_params(("arbitrary", "arbitrary"),
                                (tm + 2 * HALO) * d * 2 + 2 * d * tn * 2 + tm * tn * 2,
                                scratch_bytes=(tm + 2 * HALO) * d * 2 + 6 * (tm + 2 * HALO) * tn * 4),
        name="ffn_up",
    )(h, h, h, w_gate, w_lin, conv_w, conv_b)


def kernel(x, c, ctx, c_ctx, w_ada, b_ada, attn_norm_g, w_in, b_gate, da_qn_g, da_kn_g, da_lambda_q1, da_lambda_k1, da_lambda_q2, da_lambda_k2, da_subln_g, wa_qn_g, wa_kn_g, wa_sink, w_o_da, w_o_wa, w_out, ffn_norm_g, w_ffn_up, ffn_conv_w, ffn_conv_b, w_ffn_down):
    b, s, d = x.shape
    cl = ctx.shape[1]
    assert b == 1 and w_in.shape[0] == 1, "one batch element, one layer"
    da_heads = w_o_da.shape[1] // DA_V_DIM
    wa_q_heads = w_o_wa.shape[1] // WA_HEAD_DIM
    wa_kv_heads = wa_q_heads // WA_GROUP
    d_ff = ffn_conv_b.shape[1]
    gate_off = (3 * da_heads + wa_q_heads + 2 * wa_kv_heads) * LANES
    assert w_in.shape[2] == gate_off + N_BRANCH * d

    x2 = x.reshape(s, d)
    ctx2 = ctx.reshape(cl, d)

    ffp = -(-d_ff // 1024) * 1024 if d_ff > 1024 else d_ff
    padf = ffp - d_ff
    w_in_b = w_in[0].astype(BF16)
    w_o_da_b = w_o_da[0].astype(BF16)
    w_o_wa_b = w_o_wa[0].astype(BF16)
    w_out_b = w_out[0].astype(BF16)
    w_gate_b = jnp.pad(w_ffn_up[0][:, :d_ff].astype(BF16), ((0, 0), (0, padf)))
    w_lin_b = jnp.pad(w_ffn_up[0][:, d_ff:].astype(BF16), ((0, 0), (0, padf)))
    w_down_b = jnp.pad(w_ffn_down[0].astype(BF16), ((0, padf), (0, 0)))
    conv_w = jnp.pad(ffn_conv_w[0], ((0, 0), (0, padf)))
    conv_b = jnp.pad(ffn_conv_b[0], (0, padf)).reshape(1, ffp)

    cc = jnp.concatenate([c, c_ctx[None, :], jnp.zeros((6, d), F32)], axis=0)
    mod = _adaln(cc, w_ada[0], b_ada[0])
    sh1, sc1, g1 = mod[0:1, 0:d], mod[0:1, d:2 * d], mod[0:1, 2 * d:3 * d]
    sh2, sc2, g2 = mod[0:1, 3 * d:4 * d], mod[0:1, 4 * d:5 * d], mod[0:1, 5 * d:6 * d]
    mod_x1 = jnp.concatenate([sh1, sc1], axis=0)
    mod_c1 = jnp.concatenate([mod[1:2, 0:d], mod[1:2, d:2 * d]], axis=0)
    mod_x2 = jnp.concatenate([sh2, sc2], axis=0)

    h = _normmod(x2, attn_norm_g[0], mod_x1, ctx2, mod_c1)
    pr = _matmul(h, w_in_b, 1280, 512)
    tabs = _rope_tables(s, cl)
    gains = jnp.stack([jnp.tile(da_qn_g[0] * (DA_SCALE * LOG2_E), 2), jnp.tile(da_kn_g[0], 2),
                       wa_qn_g[0] * (WA_SCALE * LOG2_E), wa_kn_g[0]], axis=0)
    qkv = _headprep(pr, tabs, gains, da_heads, wa_q_heads)
    lam_vecs = jnp.stack([da_lambda_q1[0], da_lambda_k1[0], da_lambda_q2[0], da_lambda_k2[0]], axis=0)
    y_da = _diffattn(qkv, lam_vecs, da_subln_g[0], s, da_heads)
    y_wa = _winattn(qkv, wa_sink[0], s, cl, da_heads, wa_q_heads)
    u = _merge(y_da, y_wa, w_o_da_b, w_o_wa_b, pr, gate_off, b_gate[0])
    x1 = _matmul_residual(u, w_out_b, x2, g1, 1024, 1024, d, "out_proj")

    h2 = _normmod(x1, ffn_norm_g[0], mod_x2)
    act = _ffn_up(h2, w_gate_b, w_lin_b, conv_w, conv_b)
    out = _matmul_residual(act, w_down_b, x1, g2, 1024, 1024, 2816, "ffn_down")
    return out.reshape(b, s, d)
```

```python
import functools
import math

import jax
import jax.numpy as jnp
from jax import lax
from jax.experimental import pallas as pl
from jax.experimental.pallas import tpu as pltpu

F32 = jnp.float32
BF16 = jnp.bfloat16

GRID_W = 64
DA_QK_DIM = 64
DA_V_DIM = 2 * DA_QK_DIM
WA_HEAD_DIM = 128
WA_GROUP = 4
WINDOW = 128
N_BRANCH = 2
CONV_W = 3
ROPE_BASE = 10000.0
EPS = 1e-6
DA_SCALE = DA_QK_DIM ** -0.5
WA_SCALE = WA_HEAD_DIM ** -0.5
NEG_INF = -1e30
LAM_INIT = 0.8 - 0.6 * math.exp(-0.3 * 0)

LANES = 128
BF16_SUBLANES = 16
V7X_VMEM_CAP_BYTES = 58 * 1024 * 1024
VMEM_SLACK_BYTES = 6 * 1024 * 1024


def _pick(dim, pref, unit):
    best = None
    t = unit
    while t <= min(dim, pref):
        if dim % t == 0:
            best = t
        t += unit
    if best is None:
        raise ValueError(f"no tile for dim={dim} unit={unit}")
    return best


def _params(semantics, block_bytes, scratch_bytes=0):
    est = 2 * block_bytes + scratch_bytes + VMEM_SLACK_BYTES
    return pltpu.CompilerParams(
        dimension_semantics=semantics,
        vmem_limit_bytes=min(max(est, 16 * 1024 * 1024), V7X_VMEM_CAP_BYTES))


def _sigmoid(x):
    return 1.0 / (1.0 + jnp.exp(-x))


def _adaln_kernel(c_ref, w_ref, b_ref, o_ref):
    a = c_ref[...]
    a = a * _sigmoid(a)
    o_ref[...] = jnp.dot(a.astype(BF16), w_ref[...].astype(BF16),
                         preferred_element_type=F32) + b_ref[...]


def _adaln(cc, w, b):
    rows, d = cc.shape
    n = w.shape[1]
    tn = _pick(n, 512, LANES)
    return pl.pallas_call(
        _adaln_kernel,
        grid=(n // tn,),
        in_specs=[pl.BlockSpec((rows, d), lambda j: (0, 0)),
                  pl.BlockSpec((d, tn), lambda j: (0, j)),
                  pl.BlockSpec((1, tn), lambda j: (0, j))],
        out_specs=pl.BlockSpec((rows, tn), lambda j: (0, j)),
        out_shape=jax.ShapeDtypeStruct((rows, n), F32),
        compiler_params=_params(("arbitrary",), d * tn * 4 + d * tn * 2),
        name="adaln",
    )(cc, w, b.reshape(1, n))


def _normmod_rows(x, g, mod):
    ms = jnp.mean(x * x, axis=-1, keepdims=True)
    y = x * lax.rsqrt(ms + EPS) * g
    return (y * (1.0 + mod[1:2, :]) + mod[0:1, :]).astype(BF16)


def _normmod2_kernel(x_ref, c_ref, g_ref, mx_ref, mc_ref, o_ref, *, nx):
    i = pl.program_id(0)

    @pl.when(i < nx)
    def _():
        o_ref[...] = _normmod_rows(x_ref[...], g_ref[...], mx_ref[...])

    @pl.when(i >= nx)
    def _():
        o_ref[...] = _normmod_rows(c_ref[...], g_ref[...], mc_ref[...])


def _normmod1_kernel(x_ref, g_ref, mx_ref, o_ref):
    o_ref[...] = _normmod_rows(x_ref[...], g_ref[...], mx_ref[...])


def _normmod(x, g, mod_x, ctx=None, mod_c=None):
    s, d = x.shape
    g = g.reshape(1, d)
    if ctx is None:
        tm = _pick(s, 256, BF16_SUBLANES)
        return pl.pallas_call(
            _normmod1_kernel,
            grid=(s // tm,),
            in_specs=[pl.BlockSpec((tm, d), lambda i: (i, 0)),
                      pl.BlockSpec((1, d), lambda i: (0, 0)),
                      pl.BlockSpec((2, d), lambda i: (0, 0))],
            out_specs=pl.BlockSpec((tm, d), lambda i: (i, 0)),
            out_shape=jax.ShapeDtypeStruct((s, d), BF16),
            compiler_params=_params(("arbitrary",), tm * d * 6),
            name="normmod",
        )(x, g, mod_x)
    c = ctx.shape[0]
    tm = _pick(math.gcd(s, c), 256, BF16_SUBLANES)
    nx, nc = s // tm, c // tm
    return pl.pallas_call(
        functools.partial(_normmod2_kernel, nx=nx),
        grid=(nx + nc,),
        in_specs=[pl.BlockSpec((tm, d), lambda i: (jnp.minimum(i, nx - 1), 0)),
                  pl.BlockSpec((tm, d), lambda i: (jnp.maximum(i - nx, 0), 0)),
                  pl.BlockSpec((1, d), lambda i: (0, 0)),
                  pl.BlockSpec((2, d), lambda i: (0, 0)),
                  pl.BlockSpec((2, d), lambda i: (0, 0))],
        out_specs=pl.BlockSpec((tm, d), lambda i: (i, 0)),
        out_shape=jax.ShapeDtypeStruct((s + c, d), BF16),
        compiler_params=_params(("arbitrary",), tm * d * 10),
        name="normmod_xc",
    )(x, ctx, g, mod_x, mod_c)


def _mm_kernel(a_ref, w_ref, o_ref):
    o_ref[...] = jnp.dot(a_ref[...], w_ref[...], preferred_element_type=F32)


def _matmul(a, w, tm_pref, tn_pref):
    m, k = a.shape
    n = w.shape[1]
    tm = _pick(m, tm_pref, BF16_SUBLANES)
    tn = _pick(n, tn_pref, LANES)
    return pl.pallas_call(
        _mm_kernel,
        grid=(m // tm, n // tn),
        in_specs=[pl.BlockSpec((tm, k), lambda i, j: (i, 0)),
                  pl.BlockSpec((k, tn), lambda i, j: (0, j))],
        out_specs=pl.BlockSpec((tm, tn), lambda i, j: (i, j)),
        out_shape=jax.ShapeDtypeStruct((m, n), F32),
        compiler_params=_params(("arbitrary", "arbitrary"),
                                tm * k * 2 + k * tn * 2 + tm * tn * 4, scratch_bytes=tm * tn * 4),
        name="in_proj",
    )(a, w)


def _rope_tables(s, c):
    pos = jnp.arange(s, dtype=jnp.int32)
    r = (pos // GRID_W).astype(F32)[:, None]
    col = (pos % GRID_W).astype(F32)[:, None]
    lane = jnp.arange(LANES, dtype=jnp.int32)

    def table(head_dim):
        axis_dim = head_dim // 2
        pair = axis_dim // 2
        within = lane % head_dim
        f = (within % pair).astype(F32)
        freq = ROPE_BASE ** (-(2.0 * f) / axis_dim)
        use_row = (within // axis_dim) == 0
        ang = jnp.where(use_row[None, :], r * freq[None, :], col * freq[None, :])
        first = (within % axis_dim) < pair
        cos = jnp.cos(ang)
        sin = jnp.where(first[None, :], -jnp.sin(ang), jnp.sin(ang))
        cos = jnp.concatenate([cos, jnp.ones((c, LANES), F32)], axis=0)
        sin = jnp.concatenate([sin, jnp.zeros((c, LANES), F32)], axis=0)
        return cos, sin

    cd, sd = table(DA_QK_DIM)
    cw, sw = table(WA_HEAD_DIM)
    return jnp.stack([cd, sd, cw, sw], axis=0)


def _headprep_kernel(pr_ref, tab_ref, gain_ref, ones_da_ref, ones_wa_ref, o_ref,
                     *, n_qk_da, n_v_da, n_q_wa, n_k_wa, n_v_wa):
    tm = pr_ref.shape[0]
    lane = lax.broadcasted_iota(jnp.int32, (tm, LANES), 1)
    first_da = (lane % (DA_QK_DIM // 2)) < (DA_QK_DIM // 4)
    first_wa = (lane % (WA_HEAD_DIM // 2)) < (WA_HEAD_DIM // 4)
    cos_da, sin_da = tab_ref[0], tab_ref[1]
    cos_wa, sin_wa = tab_ref[2], tab_ref[3]

    def rope(y, cos, sin, first, shift):
        partner = jnp.where(first, pltpu.roll(y, LANES - shift, 1), pltpu.roll(y, shift, 1))
        return y * cos + partner * sin

    def head_sum_sq(x, ones_ref):
        x2 = x * x
        hi = x2.astype(BF16)
        mid = (x2 - hi.astype(F32)).astype(BF16)
        return jnp.dot(jnp.concatenate([hi, mid], axis=1), ones_ref[...], preferred_element_type=F32)

    def da_group(gidx, gain):
        x = pr_ref[:, gidx * LANES:(gidx + 1) * LANES]
        ms = head_sum_sq(x, ones_da_ref) * (1.0 / DA_QK_DIM)
        y = x * lax.rsqrt(ms + EPS) * gain
        y = rope(y, cos_da, sin_da, first_da, DA_QK_DIM // 4)
        o_ref[:, gidx * LANES:(gidx + 1) * LANES] = y.astype(BF16)

    def wa_group(gidx, gain):
        x = pr_ref[:, gidx * LANES:(gidx + 1) * LANES]
        ms = head_sum_sq(x, ones_wa_ref) * (1.0 / WA_HEAD_DIM)
        y = x * lax.rsqrt(ms + EPS) * gain
        y = rope(y, cos_wa, sin_wa, first_wa, WA_HEAD_DIM // 4)
        o_ref[:, gidx * LANES:(gidx + 1) * LANES] = y.astype(BF16)

    def copy_group(gidx):
        o_ref[:, gidx * LANES:(gidx + 1) * LANES] = pr_ref[:, gidx * LANES:(gidx + 1) * LANES].astype(BF16)

    g = 0
    for _ in range(n_qk_da):
        da_group(g, gain_ref[0:1, :]); g += 1
    for _ in range(n_qk_da):
        da_group(g, gain_ref[1:2, :]); g += 1
    for _ in range(n_v_da):
        copy_group(g); g += 1
    for _ in range(n_q_wa):
        wa_group(g, gain_ref[2:3, :]); g += 1
    for _ in range(n_k_wa):
        wa_group(g, gain_ref[3:4, :]); g += 1
    for _ in range(n_v_wa):
        copy_group(g); g += 1


def _headprep(pr, tabs, gains, da_heads, wa_q_heads, tm_pref=256):
    rows = pr.shape[0]
    wa_kv_heads = wa_q_heads // WA_GROUP
    groups = dict(n_qk_da=da_heads, n_v_da=da_heads, n_q_wa=wa_q_heads,
                  n_k_wa=wa_kv_heads, n_v_wa=wa_kv_heads)
    width = (3 * da_heads + wa_q_heads + 2 * wa_kv_heads) * LANES
    tm = _pick(rows, tm_pref, BF16_SUBLANES)
    lane = jnp.arange(LANES)
    same_da_head = (lane[:, None] // DA_QK_DIM) == (lane[None, :] // DA_QK_DIM)
    ones_da = jnp.tile(same_da_head.astype(BF16), (2, 1))
    ones_wa = jnp.ones((2 * LANES, LANES), BF16)
    return pl.pallas_call(
        functools.partial(_headprep_kernel, **groups),
        grid=(rows // tm,),
        in_specs=[pl.BlockSpec((tm, width), lambda i: (i, 0)),
                  pl.BlockSpec((4, tm, LANES), lambda i: (0, i, 0)),
                  pl.BlockSpec((4, LANES), lambda i: (0, 0)),
                  pl.BlockSpec((2 * LANES, LANES), lambda i: (0, 0)),
                  pl.BlockSpec((2 * LANES, LANES), lambda i: (0, 0))],
        out_specs=pl.BlockSpec((tm, width), lambda i: (i, 0)),
        out_shape=jax.ShapeDtypeStruct((rows, width), BF16),
        compiler_params=_params(("arbitrary",), tm * width * 6 + 4 * tm * LANES * 4),
        name="headprep",
    )(pr, tabs, gains, ones_da, ones_wa)


def _nt_dot(a, b):
    return lax.dot_general(a, b, (((1,), (1,)), ((), ())), preferred_element_type=F32)


STEPS_PER_TRIP = 12
LOG2_E = math.log2(math.e)


def _diffattn_kernel(lam_ref, q_ref, k_ref, v_ref, g_ref, o_ref,
                     s_ref, p_ref, m_ref, al_ref, acc_ref, *, tk, rb):
    tq = q_ref.shape[0]
    nk = k_ref.shape[0] // tk
    q = q_ref[...]
    lane = lax.broadcasted_iota(jnp.int32, (tq, LANES), 1)
    zero = jnp.zeros_like(q)
    qs = (jnp.where(lane < DA_QK_DIM, q, zero), jnp.where(lane < DA_QK_DIM, zero, q))
    ones = jnp.ones((tk, LANES), BF16)

    def scores(c, slot):
        k = k_ref[pl.ds(pl.multiple_of(c * tk, tk), tk), :]
        for h in range(2):
            s_ref[slot, h] = _nt_dot(qs[h], k).astype(BF16)

    def softmax(slot):
        for h in range(2):
            for r in range(0, tq, rb):
                s = s_ref[slot, h, r:r + rb, :]
                m_old = m_ref[h, r:r + rb, :]
                m_new = jnp.maximum(m_old, jnp.max(s, axis=-1, keepdims=True).astype(F32))
                p_ref[slot, h, r:r + rb, :] = jnp.exp2(s - m_new.astype(BF16))
                m_ref[h, r:r + rb, :] = m_new
                al_ref[slot, h, r:r + rb, :] = jnp.exp2(m_old - m_new)

    def accumulate(c, slot):
        v = v_ref[pl.ds(pl.multiple_of(c * tk, tk), tk), :]
        v1 = jnp.concatenate([v, ones], axis=1)
        for h in range(2):
            acc_ref[h] = (al_ref[slot, h] * acc_ref[h]
                          + jnp.dot(p_ref[slot, h], v1, preferred_element_type=F32))

    def step(c, slot, with_softmax=True, with_scores=True):
        if with_softmax:
            softmax(1 - slot)
        accumulate(c, slot)
        if with_scores:
            scores(c + 2, slot)

    m_ref[...] = jnp.full(m_ref.shape, NEG_INF, F32)
    acc_ref[...] = jnp.zeros(acc_ref.shape, F32)
    scores(0, 0)
    scores(1, 1)
    softmax(0)

    def group(i, carry):
        for t in range(STEPS_PER_TRIP):
            step(STEPS_PER_TRIP * i + t, t % 2)
        return carry

    n_full = nk - 2
    lax.fori_loop(0, n_full // STEPS_PER_TRIP, group, 0)
    c = STEPS_PER_TRIP * (n_full // STEPS_PER_TRIP)
    while c < n_full:
        step(c, c % 2)
        c += 1
    step(c, c % 2, with_scores=False)
    step(c + 1, (c + 1) % 2, with_softmax=False, with_scores=False)
    a1, a2 = acc_ref[0], acc_ref[1]

    lam = (jnp.exp(jnp.sum(lam_ref[0:1, :] * lam_ref[1:2, :], axis=-1, keepdims=True))
           - jnp.exp(jnp.sum(lam_ref[2:3, :] * lam_ref[3:4, :], axis=-1, keepdims=True))
           + LAM_INIT)
    o = (a1[:, :DA_V_DIM] / a1[:, DA_V_DIM:]) - lam * (a2[:, :DA_V_DIM] / a2[:, DA_V_DIM:])
    ms = jnp.mean(o * o, axis=-1, keepdims=True)
    o = o * lax.rsqrt(ms + EPS) * g_ref[...] * (1.0 - LAM_INIT)
    o_ref[...] = o.astype(BF16)


def _diffattn(qkv, lam_vecs, subln_g, s, da_heads, tq_pref=512, tk_pref=1280):
    rows = qkv.shape[0]
    tq = _pick(s, tq_pref, BF16_SUBLANES)
    tk = _pick(rows, min(tk_pref, rows // 2), LANES)
    kb, vb = da_heads, 2 * da_heads
    return pl.pallas_call(
        functools.partial(_diffattn_kernel, tk=tk, rb=BF16_SUBLANES),
        grid=(da_heads, s // tq),
        in_specs=[pl.BlockSpec((4, DA_QK_DIM), lambda h, i: (0, 0)),
                  pl.BlockSpec((tq, LANES), lambda h, i: (i, h)),
                  pl.BlockSpec((rows, LANES), lambda h, i: (0, kb + h)),
                  pl.BlockSpec((rows, LANES), lambda h, i: (0, vb + h)),
                  pl.BlockSpec((1, DA_V_DIM), lambda h, i: (0, 0))],
        out_specs=pl.BlockSpec((tq, DA_V_DIM), lambda h, i: (i, h)),
        out_shape=jax.ShapeDtypeStruct((s, da_heads * DA_V_DIM), BF16),
        scratch_shapes=[pltpu.VMEM((2, 2, tq, tk), BF16),
                        pltpu.VMEM((2, 2, tq, tk), BF16),
                        pltpu.VMEM((2, tq, 1), F32),
                        pltpu.VMEM((2, 2, tq, 1), F32),
                        pltpu.VMEM((2, tq, 2 * DA_V_DIM), F32)],
        compiler_params=_params(("arbitrary", "arbitrary"),
                                2 * rows * LANES * 2 + 2 * tq * LANES * 2,
                                scratch_bytes=4 * tq * tk * 6 + 4 * tq * tk * 4),
        name="diff_attn",
    )(lam_vecs, qkv, qkv, qkv, subln_g.reshape(1, DA_V_DIM))


def _winattn_kernel(sink_ref, q_ref, k_ref, v_ref, o_ref, *, s, c):
    g = pl.program_id(0)
    i = pl.program_id(1)
    tq = q_ref.shape[0]
    band = 3 * WINDOW
    rows = WA_GROUP * WINDOW
    kc = k_ref[s:s + c, :]
    vc = v_ref[s:s + c, :]
    row = lax.broadcasted_iota(jnp.int32, (rows, band), 0)
    col = lax.broadcasted_iota(jnp.int32, (rows, band), 1)
    head = lax.broadcasted_iota(jnp.int32, (rows, 1), 0) // WINDOW
    sink = jnp.zeros((rows, 1), F32)
    for r in range(WA_GROUP):
        sink = jnp.where(head == r, sink_ref[g * WA_GROUP + r] * LOG2_E, sink)
    for b in range(tq // WINDOW):
        q0 = i * tq + b * WINDOW
        start = pl.multiple_of(jnp.clip(q0 - WINDOW, 0, s - band), WINDOW)
        kb = k_ref[pl.ds(start, band), :]
        vb = v_ref[pl.ds(start, band), :]
        valid = jnp.abs(q0 + row % WINDOW - (start + col)) <= WINDOW
        q = jnp.concatenate(
            [q_ref[b * WINDOW:(b + 1) * WINDOW, r * WA_HEAD_DIM:(r + 1) * WA_HEAD_DIM]
             for r in range(WA_GROUP)], axis=0)
        sb = jnp.where(valid, _nt_dot(q, kb), NEG_INF)
        sc = _nt_dot(q, kc)
        m = jnp.maximum(jnp.max(sb, axis=-1, keepdims=True), jnp.max(sc, axis=-1, keepdims=True))
        m = jnp.maximum(m, sink)
        pb = jnp.exp2(sb - m)
        pc = jnp.exp2(sc - m)
        l = (jnp.sum(pb, axis=-1, keepdims=True) + jnp.sum(pc, axis=-1, keepdims=True)
             + jnp.exp2(sink - m))
        o = (jnp.dot(pb.astype(BF16), vb, preferred_element_type=F32)
             + jnp.dot(pc.astype(BF16), vc, preferred_element_type=F32)) * (1.0 / l)
        for r in range(WA_GROUP):
            o_ref[b * WINDOW:(b + 1) * WINDOW, r * WA_HEAD_DIM:(r + 1) * WA_HEAD_DIM] = (
                o[r * WINDOW:(r + 1) * WINDOW].astype(BF16))


def _winattn(qkv, sink, s, c, da_heads, wa_q_heads, tq_pref=512):
    rows = qkv.shape[0]
    kvh = wa_q_heads // WA_GROUP
    tq = _pick(s, tq_pref, WINDOW)
    qb = 3 * da_heads // WA_GROUP
    kb = 3 * da_heads + wa_q_heads
    vb = kb + kvh
    gw = WA_GROUP * WA_HEAD_DIM
    return pl.pallas_call(
        functools.partial(_winattn_kernel, s=s, c=c),
        grid=(kvh, s // tq),
        in_specs=[pl.BlockSpec(memory_space=pltpu.SMEM),
                  pl.BlockSpec((tq, gw), lambda g, i: (i, qb + g)),
                  pl.BlockSpec((rows, LANES), lambda g, i: (0, kb + g)),
                  pl.BlockSpec((rows, LANES), lambda g, i: (0, vb + g))],
        out_specs=pl.BlockSpec((tq, gw), lambda g, i: (i, g)),
        out_shape=jax.ShapeDtypeStruct((s, wa_q_heads * WA_HEAD_DIM), BF16),
        compiler_params=_params(("arbitrary", "arbitrary"),
                                2 * rows * LANES * 2 + 2 * tq * gw * 2,
                                scratch_bytes=8 * tq * (tq + 2 * WINDOW + c) * 4),
        name="win_attn",
    )(sink, qkv, qkv, qkv)


def _merge_kernel(yd_ref, yw_ref, wd_ref, ww_ref, gd_ref, gw_ref, bd_ref, bw_ref, o_ref):
    pd = jnp.dot(yd_ref[...], wd_ref[...], preferred_element_type=F32)
    pw = jnp.dot(yw_ref[...], ww_ref[...], preferred_element_type=F32)
    gd = _sigmoid(gd_ref[...] + bd_ref[...])
    gw = _sigmoid(gw_ref[...] + bw_ref[...])
    o_ref[...] = (gd * pd + gw * pw).astype(BF16)


def _merge(y_da, y_wa, w_o_da, w_o_wa, pr, gate_off, b_gate, tm_pref=512, tn_pref=1024):
    s, kd = y_da.shape
    kw = y_wa.shape[1]
    d = w_o_da.shape[1]
    tm = _pick(s, tm_pref, BF16_SUBLANES)
    tn = _pick(math.gcd(d, gate_off), tn_pref, LANES)
    od, ow = gate_off // tn, (gate_off + d) // tn
    nb = d // tn
    bg = b_gate.reshape(1, N_BRANCH * d)
    return pl.pallas_call(
        _merge_kernel,
        grid=(s // tm, nb),
        in_specs=[pl.BlockSpec((tm, kd), lambda i, j: (i, 0)),
                  pl.BlockSpec((tm, kw), lambda i, j: (i, 0)),
                  pl.BlockSpec((kd, tn), lambda i, j: (0, j)),
                  pl.BlockSpec((kw, tn), lambda i, j: (0, j)),
                  pl.BlockSpec((tm, tn), lambda i, j: (i, od + j)),
                  pl.BlockSpec((tm, tn), lambda i, j: (i, ow + j)),
                  pl.BlockSpec((1, tn), lambda i, j: (0, j)),
                  pl.BlockSpec((1, tn), lambda i, j: (0, nb + j))],
        out_specs=pl.BlockSpec((tm, tn), lambda i, j: (i, j)),
        out_shape=jax.ShapeDtypeStruct((s, d), BF16),
        compiler_params=_params(("arbitrary", "arbitrary"),
                                tm * (kd + kw) * 2 + (kd + kw) * tn * 2 + 2 * tm * tn * 4 + tm * tn * 2,
                                scratch_bytes=4 * tm * tn * 4),
        name="merge",
    )(y_da, y_wa, w_o_da, w_o_wa, pr, pr, bg, bg)


def _mm_res_kernel(a_ref, w_ref, x_ref, g_ref, o_ref, acc_ref):
    kk = pl.program_id(2)

    @pl.when(kk == 0)
    def _():
        acc_ref[...] = jnp.zeros_like(acc_ref)

    acc_ref[...] += jnp.dot(a_ref[...], w_ref[...], preferred_element_type=F32)

    @pl.when(kk == pl.num_programs(2) - 1)
    def _():
        o_ref[...] = x_ref[...] + g_ref[...] * acc_ref[...]


def _matmul_residual(a, w, x, gate, tm_pref, tn_pref, tk_pref, name):
    m, k = a.shape
    n = w.shape[1]
    tm = _pick(m, tm_pref, BF16_SUBLANES)
    tn = _pick(n, tn_pref, LANES)
    tk = _pick(k, tk_pref, LANES)
    return pl.pallas_call(
        _mm_res_kernel,
        grid=(m // tm, n // tn, k // tk),
        in_specs=[pl.BlockSpec((tm, tk), lambda i, j, kk: (i, kk)),
                  pl.BlockSpec((tk, tn), lambda i, j, kk: (kk, j)),
                  pl.BlockSpec((tm, tn), lambda i, j, kk: (i, j)),
                  pl.BlockSpec((1, tn), lambda i, j, kk: (0, j))],
        out_specs=pl.BlockSpec((tm, tn), lambda i, j, kk: (i, j)),
        out_shape=jax.ShapeDtypeStruct((m, n), F32),
        scratch_shapes=[pltpu.VMEM((tm, tn), F32)],
        compiler_params=_params(("arbitrary", "arbitrary", "arbitrary"),
                                tm * tk * 2 + tk * tn * 2 + 2 * tm * tn * 4,
                                scratch_bytes=2 * tm * tn * 4),
        name=name,
    )(a, w, x, gate)


HALO = BF16_SUBLANES


def _ffn_up_kernel(h_ref, hp_ref, hn_ref, wa_ref, wu_ref, cw_ref, cb_ref, o_ref, lhs_ref):
    i = pl.program_id(0)
    j = pl.program_id(1)
    tm = h_ref.shape[0]

    @pl.when(j == 0)
    def _():
        prev = jnp.where(i > 0, hp_ref[...], jnp.zeros_like(hp_ref))
        nxt = jnp.where(i < pl.num_programs(0) - 1, hn_ref[...], jnp.zeros_like(hn_ref))
        lhs_ref[0:HALO, :] = prev
        lhs_ref[HALO:HALO + tm, :] = h_ref[...]
        lhs_ref[HALO + tm:, :] = nxt

    a = jnp.dot(lhs_ref[...], wa_ref[...], preferred_element_type=F32)
    u = jnp.dot(lhs_ref[HALO:HALO + tm, :], wu_ref[...], preferred_element_type=F32)
    rows = tm + 2 * HALO
    a_prev = pltpu.roll(a, 1, 0)[HALO:HALO + tm, :]
    a_next = pltpu.roll(a, rows - 1, 0)[HALO:HALO + tm, :]
    a_mid = a[HALO:HALO + tm, :]
    conv = cb_ref[...] + a_prev * cw_ref[0:1, :] + a_mid * cw_ref[1:2, :] + a_next * cw_ref[2:3, :]
    o_ref[...] = (conv * _sigmoid(conv) * u).astype(BF16)


def _ffn_up(h, w_gate, w_lin, conv_w, conv_b, tm_pref=1024, tn_pref=512):
    s, d = h.shape
    ffp = w_gate.shape[1]
    tm = _pick(s, tm_pref, HALO)
    tn = _pick(ffp, tn_pref, LANES)
    nf = ffp // tn
    per = tm // HALO
    last = s // HALO - 1
    return pl.pallas_call(
        _ffn_up_kernel,
        grid=(s // tm, nf),
        in_specs=[pl.BlockSpec((tm, d), lambda i, j: (i, 0)),
                  pl.BlockSpec((HALO, d), lambda i, j: (jnp.maximum(i * per - 1, 0), 0)),
                  pl.BlockSpec((HALO, d), lambda i, j: (jnp.minimum((i + 1) * per, last), 0)),
                  pl.BlockSpec((d, tn), lambda i, j: (0, j)),
                  pl.BlockSpec((d, tn), lambda i, j: (0, j)),
                  pl.BlockSpec((CONV_W, tn), lambda i, j: (0, j)),
                  pl.BlockSpec((1, tn), lambda i, j: (0, j))],
        out_specs=pl.BlockSpec((tm, tn), lambda i, j: (i, j)),
        out_shape=jax.ShapeDtypeStruct((s, ffp), BF16),
        scratch_shapes=[pltpu.VMEM((tm + 2 * HALO, d), BF16)],
        compiler_params=_params(("arbitrary", "arbitrary"),
                                (tm + 2 * HALO) * d * 2 + 2 * d * tn * 2 + tm * tn * 2,
                                scratch_bytes=(tm + 2 * HALO) * d * 2 + 6 * (tm + 2 * HALO) * tn * 4),
        name="ffn_up",
    )(h, h, h, w_gate, w_lin, conv_w, conv_b)


def kernel(x, c, ctx, c_ctx, w_ada, b_ada, attn_norm_g, w_in, b_gate, da_qn_g, da_kn_g, da_lambda_q1, da_lambda_k1, da_lambda_q2, da_lambda_k2, da_subln_g, wa_qn_g, wa_kn_g, wa_sink, w_o_da, w_o_wa, w_out, ffn_norm_g, w_ffn_up, ffn_conv_w, ffn_conv_b, w_ffn_down):
    b, s, d = x.shape
    cl = ctx.shape[1]
    assert b == 1 and w_in.shape[0] == 1, "one batch element, one layer"
    da_heads = w_o_da.shape[1] // DA_V_DIM
    wa_q_heads = w_o_wa.shape[1] // WA_HEAD_DIM
    wa_kv_heads = wa_q_heads // WA_GROUP
    d_ff = ffn_conv_b.shape[1]
    gate_off = (3 * da_heads + wa_q_heads + 2 * wa_kv_heads) * LANES
    assert w_in.shape[2] == gate_off + N_BRANCH * d

    x2 = x.reshape(s, d)
    ctx2 = ctx.reshape(cl, d)

    ffp = -(-d_ff // 1024) * 1024 if d_ff > 1024 else d_ff
    padf = ffp - d_ff
    w_in_b = w_in[0].astype(BF16)
    w_o_da_b = w_o_da[0].astype(BF16)
    w_o_wa_b = w_o_wa[0].astype(BF16)
    w_out_b = w_out[0].astype(BF16)
    w_gate_b = jnp.pad(w_ffn_up[0][:, :d_ff].astype(BF16), ((0, 0), (0, padf)))
    w_lin_b = jnp.pad(w_ffn_up[0][:, d_ff:].astype(BF16), ((0, 0), (0, padf)))
    w_down_b = jnp.pad(w_ffn_down[0].astype(BF16), ((0, padf), (0, 0)))
    conv_w = jnp.pad(ffn_conv_w[0], ((0, 0), (0, padf)))
    conv_b = jnp.pad(ffn_conv_b[0], (0, padf)).reshape(1, ffp)

    cc = jnp.concatenate([c, c_ctx[None, :], jnp.zeros((6, d), F32)], axis=0)
    mod = _adaln(cc, w_ada[0], b_ada[0])
    sh1, sc1, g1 = mod[0:1, 0:d], mod[0:1, d:2 * d], mod[0:1, 2 * d:3 * d]
    sh2, sc2, g2 = mod[0:1, 3 * d:4 * d], mod[0:1, 4 * d:5 * d], mod[0:1, 5 * d:6 * d]
    mod_x1 = jnp.concatenate([sh1, sc1], axis=0)
    mod_c1 = jnp.concatenate([mod[1:2, 0:d], mod[1:2, d:2 * d]], axis=0)
    mod_x2 = jnp.concatenate([sh2, sc2], axis=0)

    h = _normmod(x2, attn_norm_g[0], mod_x1, ctx2, mod_c1)
    pr = _matmul(h, w_in_b, 1280, 512)
    tabs = _rope_tables(s, cl)
    gains = jnp.stack([jnp.tile(da_qn_g[0] * (DA_SCALE * LOG2_E), 2), jnp.tile(da_kn_g[0], 2),
                       wa_qn_g[0] * (WA_SCALE * LOG2_E), wa_kn_g[0]], axis=0)
    qkv = _headprep(pr, tabs, gains, da_heads, wa_q_heads)
    lam_vecs = jnp.stack([da_lambda_q1[0], da_lambda_k1[0], da_lambda_q2[0], da_lambda_k2[0]], axis=0)
    y_da = _diffattn(qkv, lam_vecs, da_subln_g[0], s, da_heads)
    y_wa = _winattn(qkv, wa_sink[0], s, cl, da_heads, wa_q_heads)
    u = _merge(y_da, y_wa, w_o_da_b, w_o_wa_b, pr, gate_off, b_gate[0])
    x1 = _matmul_residual(u, w_out_b, x2, g1, 1024, 1024, d, "out_proj")

    h2 = _normmod(x1, ffn_norm_g[0], mod_x2)
    act = _ffn_up(h2, w_gate_b, w_lin_b, conv_w, conv_b)
    out = _matmul_residual(act, w_down_b, x1, g2, 1024, 1024, 2816, "ffn_down")
    return out.reshape(b, s, d)
```

```python
import functools
import math

import jax
import jax.numpy as jnp
from jax import lax
from jax.experimental import pallas as pl
from jax.experimental.pallas import tpu as pltpu

F32 = jnp.float32
BF16 = jnp.bfloat16

GRID_W = 64
DA_QK_DIM = 64
DA_V_DIM = 2 * DA_QK_DIM
WA_HEAD_DIM = 128
WA_GROUP = 4
WINDOW = 128
N_BRANCH = 2
CONV_W = 3
ROPE_BASE = 10000.0
EPS = 1e-6
DA_SCALE = DA_QK_DIM ** -0.5
WA_SCALE = WA_HEAD_DIM ** -0.5
NEG_INF = -1e30
LAM_INIT = 0.8 - 0.6 * math.exp(-0.3 * 0)

LANES = 128
BF16_SUBLANES = 16
V7X_VMEM_CAP_BYTES = 58 * 1024 * 1024
VMEM_SLACK_BYTES = 6 * 1024 * 1024


def _pick(dim, pref, unit):
    best = None
    t = unit
    while t <= min(dim, pref):
        if dim % t == 0:
            best = t
        t += unit
    if best is None:
        raise ValueError(f"no tile for dim={dim} unit={unit}")
    return best


def _params(semantics, block_bytes, scratch_bytes=0):
    est = 2 * block_bytes + scratch_bytes + VMEM_SLACK_BYTES
    return pltpu.CompilerParams(
        dimension_semantics=semantics,
        vmem_limit_bytes=min(max(est, 16 * 1024 * 1024), V7X_VMEM_CAP_BYTES))


def _sigmoid(x):
    return 1.0 / (1.0 + jnp.exp(-x))


def _adaln_kernel(c_ref, w_ref, b_ref, o_ref):
    a = c_ref[...]
    a = a * _sigmoid(a)
    o_ref[...] = jnp.dot(a.astype(BF16), w_ref[...].astype(BF16),
                         preferred_element_type=F32) + b_ref[...]


def _adaln(cc, w, b):
    rows, d = cc.shape
    n = w.shape[1]
    tn = _pick(n, 512, LANES)
    return pl.pallas_call(
        _adaln_kernel,
        grid=(n // tn,),
        in_specs=[pl.BlockSpec((rows, d), lambda j: (0, 0)),
                  pl.BlockSpec((d, tn), lambda j: (0, j)),
                  pl.BlockSpec((1, tn), lambda j: (0, j))],
        out_specs=pl.BlockSpec((rows, tn), lambda j: (0, j)),
        out_shape=jax.ShapeDtypeStruct((rows, n), F32),
        compiler_params=_params(("arbitrary",), d * tn * 4 + d * tn * 2),
        name="adaln",
    )(cc, w, b.reshape(1, n))


def _normmod_rows(x, g, mod):
    ms = jnp.mean(x * x, axis=-1, keepdims=True)
    y = x * lax.rsqrt(ms + EPS) * g
    return (y * (1.0 + mod[1:2, :]) + mod[0:1, :]).astype(BF16)


def _normmod2_kernel(x_ref, c_ref, g_ref, mx_ref, mc_ref, o_ref, *, nx):
    i = pl.program_id(0)

    @pl.when(i < nx)
    def _():
        o_ref[...] = _normmod_rows(x_ref[...], g_ref[...], mx_ref[...])

    @pl.when(i >= nx)
    def _():
        o_ref[...] = _normmod_rows(c_ref[...], g_ref[...], mc_ref[...])


def _normmod1_kernel(x_ref, g_ref, mx_ref, o_ref):
    o_ref[...] = _normmod_rows(x_ref[...], g_ref[...], mx_ref[...])


def _normmod(x, g, mod_x, ctx=None, mod_c=None):
    s, d = x.shape
    g = g.reshape(1, d)
    if ctx is None:
        tm = _pick(s, 256, BF16_SUBLANES)
        return pl.pallas_call(
            _normmod1_kernel,
            grid=(s // tm,),
            in_specs=[pl.BlockSpec((tm, d), lambda i: (i, 0)),
                      pl.BlockSpec((1, d), lambda i: (0, 0)),
                      pl.BlockSpec((2, d), lambda i: (0, 0))],
            out_specs=pl.BlockSpec((tm, d), lambda i: (i, 0)),
            out_shape=jax.ShapeDtypeStruct((s, d), BF16),
            compiler_params=_params(("arbitrary",), tm * d * 6),
            name="normmod",
        )(x, g, mod_x)
    c = ctx.shape[0]
    tm = _pick(math.gcd(s, c), 256, BF16_SUBLANES)
    nx, nc = s // tm, c // tm
    return pl.pallas_call(
        functools.partial(_normmod2_kernel, nx=nx),
        grid=(nx + nc,),
        in_specs=[pl.BlockSpec((tm, d), lambda i: (jnp.minimum(i, nx - 1), 0)),
                  pl.BlockSpec((tm, d), lambda i: (jnp.maximum(i - nx, 0), 0)),
                  pl.BlockSpec((1, d), lambda i: (0, 0)),
                  pl.BlockSpec((2, d), lambda i: (0, 0)),
                  pl.BlockSpec((2, d), lambda i: (0, 0))],
        out_specs=pl.BlockSpec((tm, d), lambda i: (i, 0)),
        out_shape=jax.ShapeDtypeStruct((s + c, d), BF16),
        compiler_params=_params(("arbitrary",), tm * d * 10),
        name="normmod_xc",
    )(x, ctx, g, mod_x, mod_c)


def _mm_kernel(a_ref, w_ref, o_ref):
    o_ref[...] = jnp.dot(a_ref[...], w_ref[...], preferred_element_type=F32)


def _matmul(a, w, tm_pref, tn_pref):
    m, k = a.shape
    n = w.shape[1]
    tm = _pick(m, tm_pref, BF16_SUBLANES)
    tn = _pick(n, tn_pref, LANES)
    return pl.pallas_call(
        _mm_kernel,
        grid=(m // tm, n // tn),
        in_specs=[pl.BlockSpec((tm, k), lambda i, j: (i, 0)),
                  pl.BlockSpec((k, tn), lambda i, j: (0, j))],
        out_specs=pl.BlockSpec((tm, tn), lambda i, j: (i, j)),
        out_shape=jax.ShapeDtypeStruct((m, n), F32),
        compiler_params=_params(("arbitrary", "arbitrary"),
                                tm * k * 2 + k * tn * 2 + tm * tn * 4, scratch_bytes=tm * tn * 4),
        name="in_proj",
    )(a, w)


def _rope_tables(s, c):
    pos = jnp.arange(s, dtype=jnp.int32)
    r = (pos // GRID_W).astype(F32)[:, None]
    col = (pos % GRID_W).astype(F32)[:, None]
    lane = jnp.arange(LANES, dtype=jnp.int32)

    def table(head_dim):
        axis_dim = head_dim // 2
        pair = axis_dim // 2
        within = lane % head_dim
        f = (within % pair).astype(F32)
        freq = ROPE_BASE ** (-(2.0 * f) / axis_dim)
        use_row = (within // axis_dim) == 0
        ang = jnp.where(use_row[None, :], r * freq[None, :], col * freq[None, :])
        first = (within % axis_dim) < pair
        cos = jnp.cos(ang)
        sin = jnp.where(first[None, :], -jnp.sin(ang), jnp.sin(ang))
        cos = jnp.concatenate([cos, jnp.ones((c, LANES), F32)], axis=0)
        sin = jnp.concatenate([sin, jnp.zeros((c, LANES), F32)], axis=0)
        return cos, sin

    cd, sd = table(DA_QK_DIM)
    cw, sw = table(WA_HEAD_DIM)
    return jnp.stack([cd, sd, cw, sw], axis=0)


def _headprep_kernel(pr_ref, tab_ref, gain_ref, ones_da_ref, ones_wa_ref, o_ref,
                     *, n_qk_da, n_v_da, n_q_wa, n_k_wa, n_v_wa):
    tm = pr_ref.shape[0]
    lane = lax.broadcasted_iota(jnp.int32, (tm, LANES), 1)
    first_da = (lane % (DA_QK_DIM // 2)) < (DA_QK_DIM // 4)
    first_wa = (lane % (WA_HEAD_DIM // 2)) < (WA_HEAD_DIM // 4)
    cos_da, sin_da = tab_ref[0], tab_ref[1]
    cos_wa, sin_wa = tab_ref[2], tab_ref[3]

    def rope(y, cos, sin, first, shift):
        partner = jnp.where(first, pltpu.roll(y, LANES - shift, 1), pltpu.roll(y, shift, 1))
        return y * cos + partner * sin

    def head_sum_sq(x, ones_ref):
        x2 = x * x
        hi = x2.astype(BF16)
        mid = (x2 - hi.astype(F32)).astype(BF16)
        return jnp.dot(jnp.concatenate([hi, mid], axis=1), ones_ref[...], preferred_element_type=F32)

    def da_group(gidx, gain):
        x = pr_ref[:, gidx * LANES:(gidx + 1) * LANES]
        ms = head_sum_sq(x, ones_da_ref) * (1.0 / DA_QK_DIM)
        y = x * lax.rsqrt(ms + EPS) * gain
        y = rope(y, cos_da, sin_da, first_da, DA_QK_DIM // 4)
        o_ref[:, gidx * LANES:(gidx + 1) * LANES] = y.astype(BF16)

    def wa_group(gidx, gain):
        x = pr_ref[:, gidx * LANES:(gidx + 1) * LANES]
        ms = head_sum_sq(x, ones_wa_ref) * (1.0 / WA_HEAD_DIM)
        y = x * lax.rsqrt(ms + EPS) * gain
        y = rope(y, cos_wa, sin_wa, first_wa, WA_HEAD_DIM // 4)
        o_ref[:, gidx * LANES:(gidx + 1) * LANES] = y.astype(BF16)

    def copy_group(gidx):
        o_ref[:, gidx * LANES:(gidx + 1) * LANES] = pr_ref[:, gidx * LANES:(gidx + 1) * LANES].astype(BF16)

    g = 0
    for _ in range(n_qk_da):
        da_group(g, gain_ref[0:1, :]); g += 1
    for _ in range(n_qk_da):
        da_group(g, gain_ref[1:2, :]); g += 1
    for _ in range(n_v_da):
        copy_group(g); g += 1
    for _ in range(n_q_wa):
        wa_group(g, gain_ref[2:3, :]); g += 1
    for _ in range(n_k_wa):
        wa_group(g, gain_ref[3:4, :]); g += 1
    for _ in range(n_v_wa):
        copy_group(g); g += 1


def _headprep(pr, tabs, gains, da_heads, wa_q_heads, tm_pref=256):
    rows = pr.shape[0]
    wa_kv_heads = wa_q_heads // WA_GROUP
    groups = dict(n_qk_da=da_heads, n_v_da=da_heads, n_q_wa=wa_q_heads,
                  n_k_wa=wa_kv_heads, n_v_wa=wa_kv_heads)
    width = (3 * da_heads + wa_q_heads + 2 * wa_kv_heads) * LANES
    tm = _pick(rows, tm_pref, BF16_SUBLANES)
    lane = jnp.arange(LANES)
    same_da_head = (lane[:, None] // DA_QK_DIM) == (lane[None, :] // DA_QK_DIM)
    ones_da = jnp.tile(same_da_head.astype(BF16), (2, 1))
    ones_wa = jnp.ones((2 * LANES, LANES), BF16)
    return pl.pallas_call(
        functools.partial(_headprep_kernel, **groups),
        grid=(rows // tm,),
        in_specs=[pl.BlockSpec((tm, width), lambda i: (i, 0)),
                  pl.BlockSpec((4, tm, LANES), lambda i: (0, i, 0)),
                  pl.BlockSpec((4, LANES), lambda i: (0, 0)),
                  pl.BlockSpec((2 * LANES, LANES), lambda i: (0, 0)),
                  pl.BlockSpec((2 * LANES, LANES), lambda i: (0, 0))],
        out_specs=pl.BlockSpec((tm, width), lambda i: (i, 0)),
        out_shape=jax.ShapeDtypeStruct((rows, width), BF16),
        compiler_params=_params(("arbitrary",), tm * width * 6 + 4 * tm * LANES * 4),
        name="headprep",
    )(pr, tabs, gains, ones_da, ones_wa)


def _nt_dot(a, b):
    return lax.dot_general(a, b, (((1,), (1,)), ((), ())), preferred_element_type=F32)


STEPS_PER_TRIP = 4
LOG2_E = math.log2(math.e)


def _diffattn_kernel(lam_ref, q_ref, k_ref, v_ref, g_ref, o_ref,
                     s_ref, p_ref, m_ref, al_ref, acc_ref, *, tk, rb):
    tq = q_ref.shape[0]
    nk = k_ref.shape[0] // tk
    q = q_ref[...]
    lane = lax.broadcasted_iota(jnp.int32, (tq, LANES), 1)
    zero = jnp.zeros_like(q)
    qs = (jnp.where(lane < DA_QK_DIM, q, zero), jnp.where(lane < DA_QK_DIM, zero, q))
    ones = jnp.ones((tk, LANES), BF16)

    def scores(c, slot):
        k = k_ref[pl.ds(pl.multiple_of(c * tk, tk), tk), :]
        for h in range(2):
            s_ref[slot, h] = _nt_dot(qs[h], k).astype(BF16)

    def softmax(slot):
        for h in range(2):
            for r in range(0, tq, rb):
                s = s_ref[slot, h, r:r + rb, :]
                m_old = m_ref[h, r:r + rb, :]
                m_new = jnp.maximum(m_old, jnp.max(s, axis=-1, keepdims=True).astype(F32))
                p_ref[slot, h, r:r + rb, :] = jnp.exp2(s - m_new.astype(BF16))
                m_ref[h, r:r + rb, :] = m_new
                al_ref[slot, h, r:r + rb, :] = jnp.exp2(m_old - m_new)

    def accumulate(c, slot):
        v = v_ref[pl.ds(pl.multiple_of(c * tk, tk), tk), :]
        v1 = jnp.concatenate([v, ones], axis=1)
        for h in range(2):
            acc_ref[h] = (al_ref[slot, h] * acc_ref[h]
                          + jnp.dot(p_ref[slot, h], v1, preferred_element_type=F32))

    def step(c, slot, with_softmax=True, with_scores=True):
        if with_softmax:
            softmax(1 - slot)
        accumulate(c, slot)
        if with_scores:
            scores(c + 2, slot)

    m_ref[...] = jnp.full(m_ref.shape, NEG_INF, F32)
    acc_ref[...] = jnp.zeros(acc_ref.shape, F32)
    scores(0, 0)
    scores(1, 1)
    softmax(0)

    def group(i, carry):
        for t in range(STEPS_PER_TRIP):
            step(STEPS_PER_TRIP * i + t, t % 2)
        return carry

    n_full = nk - 2
    lax.fori_loop(0, n_full // STEPS_PER_TRIP, group, 0)
    c = STEPS_PER_TRIP * (n_full // STEPS_PER_TRIP)
    while c < n_full:
        step(c, c % 2)
        c += 1
    step(c, c % 2, with_scores=False)
    step(c + 1, (c + 1) % 2, with_softmax=False, with_scores=False)
    a1, a2 = acc_ref[0], acc_ref[1]

    lam = (jnp.exp(jnp.sum(lam_ref[0:1, :] * lam_ref[1:2, :], axis=-1, keepdims=True))
           - jnp.exp(jnp.sum(lam_ref[2:3, :] * lam_ref[3:4, :], axis=-1, keepdims=True))
           + LAM_INIT)
    o = (a1[:, :DA_V_DIM] / a1[:, DA_V_DIM:]) - lam * (a2[:, :DA_V_DIM] / a2[:, DA_V_DIM:])
    ms = jnp.mean(o * o, axis=-1, keepdims=True)
    o = o * lax.rsqrt(ms + EPS) * g_ref[...] * (1.0 - LAM_INIT)
    o_ref[...] = o.astype(BF16)


def _diffattn(qkv, lam_vecs, subln_g, s, da_heads, tq_pref=512, tk_pref=1280):
    rows = qkv.shape[0]
    tq = _pick(s, tq_pref, BF16_SUBLANES)
    tk = _pick(rows, min(tk_pref, rows // 2), LANES)
    kb, vb = da_heads, 2 * da_heads
    return pl.pallas_call(
        functools.partial(_diffattn_kernel, tk=tk, rb=BF16_SUBLANES),
        grid=(da_heads, s // tq),
        in_specs=[pl.BlockSpec((4, DA_QK_DIM), lambda h, i: (0, 0)),
                  pl.BlockSpec((tq, LANES), lambda h, i: (i, h)),
                  pl.BlockSpec((rows, LANES), lambda h, i: (0, kb + h)),
                  pl.BlockSpec((rows, LANES), lambda h, i: (0, vb + h)),
                  pl.BlockSpec((1, DA_V_DIM), lambda h, i: (0, 0))],
        out_specs=pl.BlockSpec((tq, DA_V_DIM), lambda h, i: (i, h)),
        out_shape=jax.ShapeDtypeStruct((s, da_heads * DA_V_DIM), BF16),
        scratch_shapes=[pltpu.VMEM((2, 2, tq, tk), BF16),
                        pltpu.VMEM((2, 2, tq, tk), BF16),
                        pltpu.VMEM((2, tq, 1), F32),
                        pltpu.VMEM((2, 2, tq, 1), F32),
                        pltpu.VMEM((2, tq, 2 * DA_V_DIM), F32)],
        compiler_params=_params(("arbitrary", "arbitrary"),
                                2 * rows * LANES * 2 + 2 * tq * LANES * 2,
                                scratch_bytes=4 * tq * tk * 6 + 4 * tq * tk * 4),
        name="diff_attn",
    )(lam_vecs, qkv, qkv, qkv, subln_g.reshape(1, DA_V_DIM))


def _winattn_kernel(sink_ref, q_ref, k_ref, v_ref, o_ref, *, s, c):
    g = pl.program_id(0)
    i = pl.program_id(1)
    tq = q_ref.shape[0]
    band = 3 * WINDOW
    rows = WA_GROUP * WINDOW
    kc = k_ref[s:s + c, :]
    vc = v_ref[s:s + c, :]
    row = lax.broadcasted_iota(jnp.int32, (rows, band), 0)
    col = lax.broadcasted_iota(jnp.int32, (rows, band), 1)
    head = lax.broadcasted_iota(jnp.int32, (rows, 1), 0) // WINDOW
    sink = jnp.zeros((rows, 1), F32)
    for r in range(WA_GROUP):
        sink = jnp.where(head == r, sink_ref[g * WA_GROUP + r] * LOG2_E, sink)
    for b in range(tq // WINDOW):
        q0 = i * tq + b * WINDOW
        start = pl.multiple_of(jnp.clip(q0 - WINDOW, 0, s - band), WINDOW)
        kb = k_ref[pl.ds(start, band), :]
        vb = v_ref[pl.ds(start, band), :]
        valid = jnp.abs(q0 + row % WINDOW - (start + col)) <= WINDOW
        q = jnp.concatenate(
            [q_ref[b * WINDOW:(b + 1) * WINDOW, r * WA_HEAD_DIM:(r + 1) * WA_HEAD_DIM]
             for r in range(WA_GROUP)], axis=0)
        sb = jnp.where(valid, _nt_dot(q, kb), NEG_INF)
        sc = _nt_dot(q, kc)
        m = jnp.maximum(jnp.max(sb, axis=-1, keepdims=True), jnp.max(sc, axis=-1, keepdims=True))
        m = jnp.maximum(m, sink)
        pb = jnp.exp2(sb - m)
        pc = jnp.exp2(sc - m)
        l = (jnp.sum(pb, axis=-1, keepdims=True) + jnp.sum(pc, axis=-1, keepdims=True)
             + jnp.exp2(sink - m))
        o = (jnp.dot(pb.astype(BF16), vb, preferred_element_type=F32)
             + jnp.dot(pc.astype(BF16), vc, preferred_element_type=F32)) * (1.0 / l)
        for r in range(WA_GROUP):
            o_ref[b * WINDOW:(b + 1) * WINDOW, r * WA_HEAD_DIM:(r + 1) * WA_HEAD_DIM] = (
                o[r * WINDOW:(r + 1) * WINDOW].astype(BF16))


def _winattn(qkv, sink, s, c, da_heads, wa_q_heads, tq_pref=512):
    rows = qkv.shape[0]
    kvh = wa_q_heads // WA_GROUP
    tq = _pick(s, tq_pref, WINDOW)
    qb = 3 * da_heads // WA_GROUP
    kb = 3 * da_heads + wa_q_heads
    vb = kb + kvh
    gw = WA_GROUP * WA_HEAD_DIM
    return pl.pallas_call(
        functools.partial(_winattn_kernel, s=s, c=c),
        grid=(kvh, s // tq),
        in_specs=[pl.BlockSpec(memory_space=pltpu.SMEM),
                  pl.BlockSpec((tq, gw), lambda g, i: (i, qb + g)),
                  pl.BlockSpec((rows, LANES), lambda g, i: (0, kb + g)),
                  pl.BlockSpec((rows, LANES), lambda g, i: (0, vb + g))],
        out_specs=pl.BlockSpec((tq, gw), lambda g, i: (i, g)),
        out_shape=jax.ShapeDtypeStruct((s, wa_q_heads * WA_HEAD_DIM), BF16),
        compiler_params=_params(("arbitrary", "arbitrary"),
                                2 * rows * LANES * 2 + 2 * tq * gw * 2,
                                scratch_bytes=8 * tq * (tq + 2 * WINDOW + c) * 4),
        name="win_attn",
    )(sink, qkv, qkv, qkv)


def _merge_kernel(yd_ref, yw_ref, wd_ref, ww_ref, gd_ref, gw_ref, bd_ref, bw_ref, o_ref):
    pd = jnp.dot(yd_ref[...], wd_ref[...], preferred_element_type=F32)
    pw = jnp.dot(yw_ref[...], ww_ref[...], preferred_element_type=F32)
    gd = _sigmoid(gd_ref[...] + bd_ref[...])
    gw = _sigmoid(gw_ref[...] + bw_ref[...])
    o_ref[...] = (gd * pd + gw * pw).astype(BF16)


def _merge(y_da, y_wa, w_o_da, w_o_wa, pr, gate_off, b_gate, tm_pref=512, tn_pref=1024):
    s, kd = y_da.shape
    kw = y_wa.shape[1]
    d = w_o_da.shape[1]
    tm = _pick(s, tm_pref, BF16_SUBLANES)
    tn = _pick(math.gcd(d, gate_off), tn_pref, LANES)
    od, ow = gate_off // tn, (gate_off + d) // tn
    nb = d // tn
    bg = b_gate.reshape(1, N_BRANCH * d)
    return pl.pallas_call(
        _merge_kernel,
        grid=(s // tm, nb),
        in_specs=[pl.BlockSpec((tm, kd), lambda i, j: (i, 0)),
                  pl.BlockSpec((tm, kw), lambda i, j: (i, 0)),
                  pl.BlockSpec((kd, tn), lambda i, j: (0, j)),
                  pl.BlockSpec((kw, tn), lambda i, j: (0, j)),
                  pl.BlockSpec((tm, tn), lambda i, j: (i, od + j)),
                  pl.BlockSpec((tm, tn), lambda i, j: (i, ow + j)),
                  pl.BlockSpec((1, tn), lambda i, j: (0, j)),
                  pl.BlockSpec((1, tn), lambda i, j: (0, nb + j))],
        out_specs=pl.BlockSpec((tm, tn), lambda i, j: (i, j)),
        out_shape=jax.ShapeDtypeStruct((s, d), BF16),
        compiler_params=_params(("arbitrary", "arbitrary"),
                                tm * (kd + kw) * 2 + (kd + kw) * tn * 2 + 2 * tm * tn * 4 + tm * tn * 2,
                                scratch_bytes=4 * tm * tn * 4),
        name="merge",
    )(y_da, y_wa, w_o_da, w_o_wa, pr, pr, bg, bg)


def _mm_res_kernel(a_ref, w_ref, x_ref, g_ref, o_ref, acc_ref):
    kk = pl.program_id(2)

    @pl.when(kk == 0)
    def _():
        acc_ref[...] = jnp.zeros_like(acc_ref)

    acc_ref[...] += jnp.dot(a_ref[...], w_ref[...], preferred_element_type=F32)

    @pl.when(kk == pl.num_programs(2) - 1)
    def _():
        o_ref[...] = x_ref[...] + g_ref[...] * acc_ref[...]


def _matmul_residual(a, w, x, gate, tm_pref, tn_pref, tk_pref, name):
    m, k = a.shape
    n = w.shape[1]
    tm = _pick(m, tm_pref, BF16_SUBLANES)
    tn = _pick(n, tn_pref, LANES)
    tk = _pick(k, tk_pref, LANES)
    return pl.pallas_call(
        _mm_res_kernel,
        grid=(m // tm, n // tn, k // tk),
        in_specs=[pl.BlockSpec((tm, tk), lambda i, j, kk: (i, kk)),
                  pl.BlockSpec((tk, tn), lambda i, j, kk: (kk, j)),
                  pl.BlockSpec((tm, tn), lambda i, j, kk: (i, j)),
                  pl.BlockSpec((1, tn), lambda i, j, kk: (0, j))],
        out_specs=pl.BlockSpec((tm, tn), lambda i, j, kk: (i, j)),
        out_shape=jax.ShapeDtypeStruct((m, n), F32),
        scratch_shapes=[pltpu.VMEM((tm, tn), F32)],
        compiler_params=_params(("arbitrary", "arbitrary", "arbitrary"),
                                tm * tk * 2 + tk * tn * 2 + 2 * tm * tn * 4,
                                scratch_bytes=2 * tm * tn * 4),
        name=name,
    )(a, w, x, gate)


HALO = BF16_SUBLANES


def _ffn_up_kernel(h_ref, hp_ref, hn_ref, wa_ref, wu_ref, cw_ref, cb_ref, o_ref, lhs_ref):
    i = pl.program_id(0)
    j = pl.program_id(1)
    tm = h_ref.shape[0]

    @pl.when(j == 0)
    def _():
        prev = jnp.where(i > 0, hp_ref[...], jnp.zeros_like(hp_ref))
        nxt = jnp.where(i < pl.num_programs(0) - 1, hn_ref[...], jnp.zeros_like(hn_ref))
        lhs_ref[0:HALO, :] = prev
        lhs_ref[HALO:HALO + tm, :] = h_ref[...]
        lhs_ref[HALO + tm:, :] = nxt

    a = jnp.dot(lhs_ref[...], wa_ref[...], preferred_element_type=F32)
    u = jnp.dot(lhs_ref[HALO:HALO + tm, :], wu_ref[...], preferred_element_type=F32)
    rows = tm + 2 * HALO
    a_prev = pltpu.roll(a, 1, 0)[HALO:HALO + tm, :]
    a_next = pltpu.roll(a, rows - 1, 0)[HALO:HALO + tm, :]
    a_mid = a[HALO:HALO + tm, :]
    conv = cb_ref[...] + a_prev * cw_ref[0:1, :] + a_mid * cw_ref[1:2, :] + a_next * cw_ref[2:3, :]
    o_ref[...] = (conv * _sigmoid(conv) * u).astype(BF16)


def _ffn_up(h, w_gate, w_lin, conv_w, conv_b, tm_pref=1024, tn_pref=512):
    s, d = h.shape
    ffp = w_gate.shape[1]
    tm = _pick(s, tm_pref, HALO)
    tn = _pick(ffp, tn_pref, LANES)
    nf = ffp // tn
    per = tm // HALO
    last = s // HALO - 1
    return pl.pallas_call(
        _ffn_up_kernel,
        grid=(s // tm, nf),
        in_specs=[pl.BlockSpec((tm, d), lambda i, j: (i, 0)),
                  pl.BlockSpec((HALO, d), lambda i, j: (jnp.maximum(i * per - 1, 0), 0)),
                  pl.BlockSpec((HALO, d), lambda i, j: (jnp.minimum((i + 1) * per, last), 0)),
                  pl.BlockSpec((d, tn), lambda i, j: (0, j)),
                  pl.BlockSpec((d, tn), lambda i, j: (0, j)),
                  pl.BlockSpec((CONV_W, tn), lambda i, j: (0, j)),
                  pl.BlockSpec((1, tn), lambda i, j: (0, j))],
        out_specs=pl.BlockSpec((tm, tn), lambda i, j: (i, j)),
        out_shape=jax.ShapeDtypeStruct((s, ffp), BF16),
        scratch_shapes=[pltpu.VMEM((tm + 2 * HALO, d), BF16)],
        compiler_params=_params(("arbitrary", "arbitrary"),
                                (tm + 2 * HALO) * d * 2 + 2 * d * tn * 2 + tm * tn * 2,
                                scratch_bytes=(tm + 2 * HALO) * d * 2 + 6 * (tm + 2 * HALO) * tn * 4),
        name="ffn_up",
    )(h, h, h, w_gate, w_lin, conv_w, conv_b)


def _split_cast_kernel(wg_ref, wl_ref, og_ref, ol_ref, *, n_valid):
    j = pl.program_id(0)

    @pl.when(j < n_valid)
    def _():
        og_ref[...] = wg_ref[...].astype(BF16)
        ol_ref[...] = wl_ref[...].astype(BF16)

    @pl.when(j >= n_valid)
    def _():
        og_ref[...] = jnp.zeros_like(og_ref)
        ol_ref[...] = jnp.zeros_like(ol_ref)


def _split_cast_pad(w, d_ff, ffp):
    d = w.shape[0]
    assert d_ff % LANES == 0 and ffp % LANES == 0
    tw = _pick(math.gcd(d_ff, ffp), 256, LANES)
    n_valid, n_tiles = d_ff // tw, ffp // tw
    out = jax.ShapeDtypeStruct((d, ffp), BF16)
    return pl.pallas_call(
        functools.partial(_split_cast_kernel, n_valid=n_valid),
        grid=(n_tiles,),
        in_specs=[pl.BlockSpec((d, tw), lambda j: (0, jnp.minimum(j, n_valid - 1))),
                  pl.BlockSpec((d, tw), lambda j: (0, n_valid + jnp.minimum(j, n_valid - 1)))],
        out_specs=[pl.BlockSpec((d, tw), lambda j: (0, j)), pl.BlockSpec((d, tw), lambda j: (0, j))],
        out_shape=[out, out],
        compiler_params=_params(("arbitrary",), 2 * d * tw * 4 + 2 * d * tw * 2),
        name="w_up_cast",
    )(w, w)


def kernel(x, c, ctx, c_ctx, w_ada, b_ada, attn_norm_g, w_in, b_gate, da_qn_g, da_kn_g, da_lambda_q1, da_lambda_k1, da_lambda_q2, da_lambda_k2, da_subln_g, wa_qn_g, wa_kn_g, wa_sink, w_o_da, w_o_wa, w_out, ffn_norm_g, w_ffn_up, ffn_conv_w, ffn_conv_b, w_ffn_down):
    b, s, d = x.shape
    cl = ctx.shape[1]
    assert b == 1 and w_in.shape[0] == 1, "one batch element, one layer"
    da_heads = w_o_da.shape[1] // DA_V_DIM
    wa_q_heads = w_o_wa.shape[1] // WA_HEAD_DIM
    wa_kv_heads = wa_q_heads // WA_GROUP
    d_ff = ffn_conv_b.shape[1]
    gate_off = (3 * da_heads + wa_q_heads + 2 * wa_kv_heads) * LANES
    assert w_in.shape[2] == gate_off + N_BRANCH * d

    x2 = x.reshape(s, d)
    ctx2 = ctx.reshape(cl, d)

    ffp = -(-d_ff // 1024) * 1024 if d_ff > 1024 else d_ff
    padf = ffp - d_ff
    w_in_b = w_in[0].astype(BF16)
    w_o_da_b = w_o_da[0].astype(BF16)
    w_o_wa_b = w_o_wa[0].astype(BF16)
    w_out_b = w_out[0].astype(BF16)
    w_gate_b, w_lin_b = _split_cast_pad(w_ffn_up[0], d_ff, ffp)
    w_down_b = jnp.pad(w_ffn_down[0].astype(BF16), ((0, padf), (0, 0)))
    conv_w = jnp.pad(ffn_conv_w[0], ((0, 0), (0, padf)))
    conv_b = jnp.pad(ffn_conv_b[0], (0, padf)).reshape(1, ffp)

    cc = jnp.concatenate([c, c_ctx[None, :], jnp.zeros((6, d), F32)], axis=0)
    mod = _adaln(cc, w_ada[0], b_ada[0])
    sh1, sc1, g1 = mod[0:1, 0:d], mod[0:1, d:2 * d], mod[0:1, 2 * d:3 * d]
    sh2, sc2, g2 = mod[0:1, 3 * d:4 * d], mod[0:1, 4 * d:5 * d], mod[0:1, 5 * d:6 * d]
    mod_x1 = jnp.concatenate([sh1, sc1], axis=0)
    mod_c1 = jnp.concatenate([mod[1:2, 0:d], mod[1:2, d:2 * d]], axis=0)
    mod_x2 = jnp.concatenate([sh2, sc2], axis=0)

    h = _normmod(x2, attn_norm_g[0], mod_x1, ctx2, mod_c1)
    pr = _matmul(h, w_in_b, 1280, 512)
    tabs = _rope_tables(s, cl)
    gains = jnp.stack([jnp.tile(da_qn_g[0] * (DA_SCALE * LOG2_E), 2), jnp.tile(da_kn_g[0], 2),
                       wa_qn_g[0] * (WA_SCALE * LOG2_E), wa_kn_g[0]], axis=0)
    qkv = _headprep(pr, tabs, gains, da_heads, wa_q_heads)
    lam_vecs = jnp.stack([da_lambda_q1[0], da_lambda_k1[0], da_lambda_q2[0], da_lambda_k2[0]], axis=0)
    y_da = _diffattn(qkv, lam_vecs, da_subln_g[0], s, da_heads)
    y_wa = _winattn(qkv, wa_sink[0], s, cl, da_heads, wa_q_heads)
    u = _merge(y_da, y_wa, w_o_da_b, w_o_wa_b, pr, gate_off, b_gate[0])
    x1 = _matmul_residual(u, w_out_b, x2, g1, 1024, 1024, d, "out_proj")

    h2 = _normmod(x1, ffn_norm_g[0], mod_x2)
    act = _ffn_up(h2, w_gate_b, w_lin_b, conv_w, conv_b)
    out = _matmul_residual(act, w_down_b, x1, g2, 1024, 1024, 2816, "ffn_down")
    return out.reshape(b, s, d)
```

```python
import functools
import math

import jax
import jax.numpy as jnp
from jax import lax
from jax.experimental import pallas as pl
from jax.experimental.pallas import tpu as pltpu

F32 = jnp.float32
BF16 = jnp.bfloat16

GRID_W = 64
DA_QK_DIM = 64
DA_V_DIM = 2 * DA_QK_DIM
WA_HEAD_DIM = 128
WA_GROUP = 4
WINDOW = 128
N_BRANCH = 2
CONV_W = 3
ROPE_BASE = 10000.0
EPS = 1e-6
DA_SCALE = DA_QK_DIM ** -0.5
WA_SCALE = WA_HEAD_DIM ** -0.5
NEG_INF = -1e30
LAM_INIT = 0.8 - 0.6 * math.exp(-0.3 * 0)

LANES = 128
BF16_SUBLANES = 16
V7X_VMEM_CAP_BYTES = 58 * 1024 * 1024
VMEM_SLACK_BYTES = 6 * 1024 * 1024


def _pick(dim, pref, unit):
    best = None
    t = unit
    while t <= min(dim, pref):
        if dim % t == 0:
            best = t
        t += unit
    if best is None:
        raise ValueError(f"no tile for dim={dim} unit={unit}")
    return best


def _params(semantics, block_bytes, scratch_bytes=0):
    est = 2 * block_bytes + scratch_bytes + VMEM_SLACK_BYTES
    return pltpu.CompilerParams(
        dimension_semantics=semantics,
        vmem_limit_bytes=min(max(est, 16 * 1024 * 1024), V7X_VMEM_CAP_BYTES))


def _sigmoid(x):
    return 1.0 / (1.0 + jnp.exp(-x))


def _adaln_kernel(c_ref, w_ref, b_ref, o_ref):
    a = c_ref[...]
    a = a * _sigmoid(a)
    o_ref[...] = jnp.dot(a.astype(BF16), w_ref[...].astype(BF16),
                         preferred_element_type=F32) + b_ref[...]


def _adaln(cc, w, b):
    rows, d = cc.shape
    n = w.shape[1]
    tn = _pick(n, 512, LANES)
    return pl.pallas_call(
        _adaln_kernel,
        grid=(n // tn,),
        in_specs=[pl.BlockSpec((rows, d), lambda j: (0, 0)),
                  pl.BlockSpec((d, tn), lambda j: (0, j)),
                  pl.BlockSpec((1, tn), lambda j: (0, j))],
        out_specs=pl.BlockSpec((rows, tn), lambda j: (0, j)),
        out_shape=jax.ShapeDtypeStruct((rows, n), F32),
        compiler_params=_params(("arbitrary",), d * tn * 4 + d * tn * 2),
        name="adaln",
    )(cc, w, b.reshape(1, n))


def _normmod_rows(x, g, mod):
    ms = jnp.mean(x * x, axis=-1, keepdims=True)
    y = x * lax.rsqrt(ms + EPS) * g
    return (y * (1.0 + mod[1:2, :]) + mod[0:1, :]).astype(BF16)


def _normmod2_kernel(x_ref, c_ref, g_ref, mx_ref, mc_ref, o_ref, *, nx):
    i = pl.program_id(0)

    @pl.when(i < nx)
    def _():
        o_ref[...] = _normmod_rows(x_ref[...], g_ref[...], mx_ref[...])

    @pl.when(i >= nx)
    def _():
        o_ref[...] = _normmod_rows(c_ref[...], g_ref[...], mc_ref[...])


def _normmod1_kernel(x_ref, g_ref, mx_ref, o_ref):
    o_ref[...] = _normmod_rows(x_ref[...], g_ref[...], mx_ref[...])


def _normmod(x, g, mod_x, ctx=None, mod_c=None):
    s, d = x.shape
    g = g.reshape(1, d)
    if ctx is None:
        tm = _pick(s, 256, BF16_SUBLANES)
        return pl.pallas_call(
            _normmod1_kernel,
            grid=(s // tm,),
            in_specs=[pl.BlockSpec((tm, d), lambda i: (i, 0)),
                      pl.BlockSpec((1, d), lambda i: (0, 0)),
                      pl.BlockSpec((2, d), lambda i: (0, 0))],
            out_specs=pl.BlockSpec((tm, d), lambda i: (i, 0)),
            out_shape=jax.ShapeDtypeStruct((s, d), BF16),
            compiler_params=_params(("arbitrary",), tm * d * 6),
            name="normmod",
        )(x, g, mod_x)
    c = ctx.shape[0]
    tm = _pick(math.gcd(s, c), 256, BF16_SUBLANES)
    nx, nc = s // tm, c // tm
    return pl.pallas_call(
        functools.partial(_normmod2_kernel, nx=nx),
        grid=(nx + nc,),
        in_specs=[pl.BlockSpec((tm, d), lambda i: (jnp.minimum(i, nx - 1), 0)),
                  pl.BlockSpec((tm, d), lambda i: (jnp.maximum(i - nx, 0), 0)),
                  pl.BlockSpec((1, d), lambda i: (0, 0)),
                  pl.BlockSpec((2, d), lambda i: (0, 0)),
                  pl.BlockSpec((2, d), lambda i: (0, 0))],
        out_specs=pl.BlockSpec((tm, d), lambda i: (i, 0)),
        out_shape=jax.ShapeDtypeStruct((s + c, d), BF16),
        compiler_params=_params(("arbitrary",), tm * d * 10),
        name="normmod_xc",
    )(x, ctx, g, mod_x, mod_c)


def _mm_kernel(a_ref, w_ref, o_ref):
    o_ref[...] = jnp.dot(a_ref[...], w_ref[...], preferred_element_type=F32)


def _matmul(a, w, tm_pref, tn_pref):
    m, k = a.shape
    n = w.shape[1]
    tm = _pick(m, tm_pref, BF16_SUBLANES)
    tn = _pick(n, tn_pref, LANES)
    return pl.pallas_call(
        _mm_kernel,
        grid=(m // tm, n // tn),
        in_specs=[pl.BlockSpec((tm, k), lambda i, j: (i, 0)),
                  pl.BlockSpec((k, tn), lambda i, j: (0, j))],
        out_specs=pl.BlockSpec((tm, tn), lambda i, j: (i, j)),
        out_shape=jax.ShapeDtypeStruct((m, n), F32),
        compiler_params=_params(("arbitrary", "arbitrary"),
                                tm * k * 2 + k * tn * 2 + tm * tn * 4, scratch_bytes=tm * tn * 4),
        name="in_proj",
    )(a, w)


def _rope_tables(s, c):
    pos = jnp.arange(s, dtype=jnp.int32)
    r = (pos // GRID_W).astype(F32)[:, None]
    col = (pos % GRID_W).astype(F32)[:, None]
    lane = jnp.arange(LANES, dtype=jnp.int32)

    def table(head_dim):
        axis_dim = head_dim // 2
        pair = axis_dim // 2
        within = lane % head_dim
        f = (within % pair).astype(F32)
        freq = ROPE_BASE ** (-(2.0 * f) / axis_dim)
        use_row = (within // axis_dim) == 0
        ang = jnp.where(use_row[None, :], r * freq[None, :], col * freq[None, :])
        first = (within % axis_dim) < pair
        cos = jnp.cos(ang)
        sin = jnp.where(first[None, :], -jnp.sin(ang), jnp.sin(ang))
        cos = jnp.concatenate([cos, jnp.ones((c, LANES), F32)], axis=0)
        sin = jnp.concatenate([sin, jnp.zeros((c, LANES), F32)], axis=0)
        return cos, sin

    cd, sd = table(DA_QK_DIM)
    cw, sw = table(WA_HEAD_DIM)
    return jnp.stack([cd, sd, cw, sw], axis=0)


def _headprep_kernel(pr_ref, tab_ref, gain_ref, ones_da_ref, ones_wa_ref, o_ref,
                     *, n_qk_da, n_v_da, n_q_wa, n_k_wa, n_v_wa):
    tm = pr_ref.shape[0]
    lane = lax.broadcasted_iota(jnp.int32, (tm, LANES), 1)
    first_da = (lane % (DA_QK_DIM // 2)) < (DA_QK_DIM // 4)
    first_wa = (lane % (WA_HEAD_DIM // 2)) < (WA_HEAD_DIM // 4)
    cos_da, sin_da = tab_ref[0], tab_ref[1]
    cos_wa, sin_wa = tab_ref[2], tab_ref[3]

    def rope(y, cos, sin, first, shift):
        partner = jnp.where(first, pltpu.roll(y, LANES - shift, 1), pltpu.roll(y, shift, 1))
        return y * cos + partner * sin

    def head_sum_sq(x, ones_ref):
        x2 = x * x
        hi = x2.astype(BF16)
        mid = (x2 - hi.astype(F32)).astype(BF16)
        return jnp.dot(jnp.concatenate([hi, mid], axis=1), ones_ref[...], preferred_element_type=F32)

    def da_group(gidx, gain):
        x = pr_ref[:, gidx * LANES:(gidx + 1) * LANES]
        ms = head_sum_sq(x, ones_da_ref) * (1.0 / DA_QK_DIM)
        y = x * lax.rsqrt(ms + EPS) * gain
        y = rope(y, cos_da, sin_da, first_da, DA_QK_DIM // 4)
        o_ref[:, gidx * LANES:(gidx + 1) * LANES] = y.astype(BF16)

    def wa_group(gidx, gain):
        x = pr_ref[:, gidx * LANES:(gidx + 1) * LANES]
        ms = head_sum_sq(x, ones_wa_ref) * (1.0 / WA_HEAD_DIM)
        y = x * lax.rsqrt(ms + EPS) * gain
        y = rope(y, cos_wa, sin_wa, first_wa, WA_HEAD_DIM // 4)
        o_ref[:, gidx * LANES:(gidx + 1) * LANES] = y.astype(BF16)

    def copy_group(gidx):
        o_ref[:, gidx * LANES:(gidx + 1) * LANES] = pr_ref[:, gidx * LANES:(gidx + 1) * LANES].astype(BF16)

    g = 0
    for _ in range(n_qk_da):
        da_group(g, gain_ref[0:1, :]); g += 1
    for _ in range(n_qk_da):
        da_group(g, gain_ref[1:2, :]); g += 1
    for _ in range(n_v_da):
        copy_group(g); g += 1
    for _ in range(n_q_wa):
        wa_group(g, gain_ref[2:3, :]); g += 1
    for _ in range(n_k_wa):
        wa_group(g, gain_ref[3:4, :]); g += 1
    for _ in range(n_v_wa):
        copy_group(g); g += 1


def _headprep(pr, tabs, gains, da_heads, wa_q_heads, tm_pref=256):
    rows = pr.shape[0]
    wa_kv_heads = wa_q_heads // WA_GROUP
    groups = dict(n_qk_da=da_heads, n_v_da=da_heads, n_q_wa=wa_q_heads,
                  n_k_wa=wa_kv_heads, n_v_wa=wa_kv_heads)
    width = (3 * da_heads + wa_q_heads + 2 * wa_kv_heads) * LANES
    tm = _pick(rows, tm_pref, BF16_SUBLANES)
    lane = jnp.arange(LANES)
    same_da_head = (lane[:, None] // DA_QK_DIM) == (lane[None, :] // DA_QK_DIM)
    ones_da = jnp.tile(same_da_head.astype(BF16), (2, 1))
    ones_wa = jnp.ones((2 * LANES, LANES), BF16)
    return pl.pallas_call(
        functools.partial(_headprep_kernel, **groups),
        grid=(rows // tm,),
        in_specs=[pl.BlockSpec((tm, width), lambda i: (i, 0)),
                  pl.BlockSpec((4, tm, LANES), lambda i: (0, i, 0)),
                  pl.BlockSpec((4, LANES), lambda i: (0, 0)),
                  pl.BlockSpec((2 * LANES, LANES), lambda i: (0, 0)),
                  pl.BlockSpec((2 * LANES, LANES), lambda i: (0, 0))],
        out_specs=pl.BlockSpec((tm, width), lambda i: (i, 0)),
        out_shape=jax.ShapeDtypeStruct((rows, width), BF16),
        compiler_params=_params(("arbitrary",), tm * width * 6 + 4 * tm * LANES * 4),
        name="headprep",
    )(pr, tabs, gains, ones_da, ones_wa)


def _nt_dot(a, b):
    return lax.dot_general(a, b, (((1,), (1,)), ((), ())), preferred_element_type=F32)


STEPS_PER_TRIP = 4
LOG2_E = math.log2(math.e)


def _diffattn_kernel(lam_ref, q_ref, k_ref, v_ref, g_ref, o_ref,
                     s_ref, p_ref, m_ref, al_ref, acc_ref, *, tk, rb):
    tq = q_ref.shape[0]
    nk = k_ref.shape[0] // tk
    q = q_ref[...]
    lane = lax.broadcasted_iota(jnp.int32, (tq, LANES), 1)
    zero = jnp.zeros_like(q)
    qs = (jnp.where(lane < DA_QK_DIM, q, zero), jnp.where(lane < DA_QK_DIM, zero, q))
    ones = jnp.ones((tk, LANES), BF16)

    def scores(c, slot):
        k = k_ref[pl.ds(pl.multiple_of(c * tk, tk), tk), :]
        for h in range(2):
            s_ref[slot, h] = _nt_dot(qs[h], k).astype(BF16)

    def softmax(slot):
        for h in range(2):
            for r in range(0, tq, rb):
                s = s_ref[slot, h, r:r + rb, :]
                m_old = m_ref[h, r:r + rb, :]
                m_new = jnp.maximum(m_old, jnp.max(s, axis=-1, keepdims=True).astype(F32))
                p_ref[slot, h, r:r + rb, :] = jnp.exp2(s - m_new.astype(BF16))
                m_ref[h, r:r + rb, :] = m_new
                al_ref[slot, h, r:r + rb, :] = jnp.exp2(m_old - m_new)

    def accumulate(c, slot):
        v = v_ref[pl.ds(pl.multiple_of(c * tk, tk), tk), :]
        v1 = jnp.concatenate([v, ones], axis=1)
        for h in range(2):
            acc_ref[h] = (al_ref[slot, h] * acc_ref[h]
                          + jnp.dot(p_ref[slot, h], v1, preferred_element_type=F32))

    def step(c, slot, with_softmax=True, with_scores=True):
        if with_softmax:
            softmax(1 - slot)
        accumulate(c, slot)
        if with_scores:
            scores(c + 2, slot)

    m_ref[...] = jnp.full(m_ref.shape, NEG_INF, F32)
    acc_ref[...] = jnp.zeros(acc_ref.shape, F32)
    scores(0, 0)
    scores(1, 1)
    softmax(0)

    def group(i, carry):
        for t in range(STEPS_PER_TRIP):
            step(STEPS_PER_TRIP * i + t, t % 2)
        return carry

    n_full = nk - 2
    lax.fori_loop(0, n_full // STEPS_PER_TRIP, group, 0)
    c = STEPS_PER_TRIP * (n_full // STEPS_PER_TRIP)
    while c < n_full:
        step(c, c % 2)
        c += 1
    step(c, c % 2, with_scores=False)
    step(c + 1, (c + 1) % 2, with_softmax=False, with_scores=False)
    a1, a2 = acc_ref[0], acc_ref[1]

    lam = (jnp.exp(jnp.sum(lam_ref[0:1, :] * lam_ref[1:2, :], axis=-1, keepdims=True))
           - jnp.exp(jnp.sum(lam_ref[2:3, :] * lam_ref[3:4, :], axis=-1, keepdims=True))
           + LAM_INIT)
    o = (a1[:, :DA_V_DIM] / a1[:, DA_V_DIM:]) - lam * (a2[:, :DA_V_DIM] / a2[:, DA_V_DIM:])
    ms = jnp.mean(o * o, axis=-1, keepdims=True)
    o = o * lax.rsqrt(ms + EPS) * g_ref[...] * (1.0 - LAM_INIT)
    o_ref[...] = o.astype(BF16)


def _diffattn(qkv, lam_vecs, subln_g, s, da_heads, tq_pref=512, tk_pref=1280):
    rows = qkv.shape[0]
    tq = _pick(s, tq_pref, BF16_SUBLANES)
    tk = _pick(rows, min(tk_pref, rows // 2), LANES)
    kb, vb = da_heads, 2 * da_heads
    return pl.pallas_call(
        functools.partial(_diffattn_kernel, tk=tk, rb=2 * BF16_SUBLANES),
        grid=(da_heads, s // tq),
        in_specs=[pl.BlockSpec((4, DA_QK_DIM), lambda h, i: (0, 0)),
                  pl.BlockSpec((tq, LANES), lambda h, i: (i, h)),
                  pl.BlockSpec((rows, LANES), lambda h, i: (0, kb + h)),
                  pl.BlockSpec((rows, LANES), lambda h, i: (0, vb + h)),
                  pl.BlockSpec((1, DA_V_DIM), lambda h, i: (0, 0))],
        out_specs=pl.BlockSpec((tq, DA_V_DIM), lambda h, i: (i, h)),
        out_shape=jax.ShapeDtypeStruct((s, da_heads * DA_V_DIM), BF16),
        scratch_shapes=[pltpu.VMEM((2, 2, tq, tk), BF16),
                        pltpu.VMEM((2, 2, tq, tk), BF16),
                        pltpu.VMEM((2, tq, 1), F32),
                        pltpu.VMEM((2, 2, tq, 1), F32),
                        pltpu.VMEM((2, tq, 2 * DA_V_DIM), F32)],
        compiler_params=_params(("arbitrary", "arbitrary"),
                                2 * rows * LANES * 2 + 2 * tq * LANES * 2,
                                scratch_bytes=4 * tq * tk * 6 + 4 * tq * tk * 4),
        name="diff_attn",
    )(lam_vecs, qkv, qkv, qkv, subln_g.reshape(1, DA_V_DIM))


def _winattn_kernel(sink_ref, q_ref, k_ref, v_ref, o_ref, *, s, c):
    g = pl.program_id(0)
    i = pl.program_id(1)
    tq = q_ref.shape[0]
    band = 3 * WINDOW
    rows = WA_GROUP * WINDOW
    kc = k_ref[s:s + c, :]
    vc = v_ref[s:s + c, :]
    row = lax.broadcasted_iota(jnp.int32, (rows, band), 0)
    col = lax.broadcasted_iota(jnp.int32, (rows, band), 1)
    head = lax.broadcasted_iota(jnp.int32, (rows, 1), 0) // WINDOW
    sink = jnp.zeros((rows, 1), F32)
    for r in range(WA_GROUP):
        sink = jnp.where(head == r, sink_ref[g * WA_GROUP + r] * LOG2_E, sink)
    for b in range(tq // WINDOW):
        q0 = i * tq + b * WINDOW
        start = pl.multiple_of(jnp.clip(q0 - WINDOW, 0, s - band), WINDOW)
        kb = k_ref[pl.ds(start, band), :]
        vb = v_ref[pl.ds(start, band), :]
        valid = jnp.abs(q0 + row % WINDOW - (start + col)) <= WINDOW
        q = jnp.concatenate(
            [q_ref[b * WINDOW:(b + 1) * WINDOW, r * WA_HEAD_DIM:(r + 1) * WA_HEAD_DIM]
             for r in range(WA_GROUP)], axis=0)
        sb = jnp.where(valid, _nt_dot(q, kb), NEG_INF)
        sc = _nt_dot(q, kc)
        m = jnp.maximum(jnp.max(sb, axis=-1, keepdims=True), jnp.max(sc, axis=-1, keepdims=True))
        m = jnp.maximum(m, sink)
        pb = jnp.exp2(sb - m)
        pc = jnp.exp2(sc - m)
        l = (jnp.sum(pb, axis=-1, keepdims=True) + jnp.sum(pc, axis=-1, keepdims=True)
             + jnp.exp2(sink - m))
        o = (jnp.dot(pb.astype(BF16), vb, preferred_element_type=F32)
             + jnp.dot(pc.astype(BF16), vc, preferred_element_type=F32)) * (1.0 / l)
        for r in range(WA_GROUP):
            o_ref[b * WINDOW:(b + 1) * WINDOW, r * WA_HEAD_DIM:(r + 1) * WA_HEAD_DIM] = (
                o[r * WINDOW:(r + 1) * WINDOW].astype(BF16))


def _winattn(qkv, sink, s, c, da_heads, wa_q_heads, tq_pref=512):
    rows = qkv.shape[0]
    kvh = wa_q_heads // WA_GROUP
    tq = _pick(s, tq_pref, WINDOW)
    qb = 3 * da_heads // WA_GROUP
    kb = 3 * da_heads + wa_q_heads
    vb = kb + kvh
    gw = WA_GROUP * WA_HEAD_DIM
    return pl.pallas_call(
        functools.partial(_winattn_kernel, s=s, c=c),
        grid=(kvh, s // tq),
        in_specs=[pl.BlockSpec(memory_space=pltpu.SMEM),
                  pl.BlockSpec((tq, gw), lambda g, i: (i, qb + g)),
                  pl.BlockSpec((rows, LANES), lambda g, i: (0, kb + g)),
                  pl.BlockSpec((rows, LANES), lambda g, i: (0, vb + g))],
        out_specs=pl.BlockSpec((tq, gw), lambda g, i: (i, g)),
        out_shape=jax.ShapeDtypeStruct((s, wa_q_heads * WA_HEAD_DIM), BF16),
        compiler_params=_params(("arbitrary", "arbitrary"),
                                2 * rows * LANES * 2 + 2 * tq * gw * 2,
                                scratch_bytes=8 * tq * (tq + 2 * WINDOW + c) * 4),
        name="win_attn",
    )(sink, qkv, qkv, qkv)


def _merge_kernel(yd_ref, yw_ref, wd_ref, ww_ref, gd_ref, gw_ref, bd_ref, bw_ref, o_ref):
    pd = jnp.dot(yd_ref[...], wd_ref[...], preferred_element_type=F32)
    pw = jnp.dot(yw_ref[...], ww_ref[...], preferred_element_type=F32)
    gd = _sigmoid(gd_ref[...] + bd_ref[...])
    gw = _sigmoid(gw_ref[...] + bw_ref[...])
    o_ref[...] = (gd * pd + gw * pw).astype(BF16)


def _merge(y_da, y_wa, w_o_da, w_o_wa, pr, gate_off, b_gate, tm_pref=512, tn_pref=1024):
    s, kd = y_da.shape
    kw = y_wa.shape[1]
    d = w_o_da.shape[1]
    tm = _pick(s, tm_pref, BF16_SUBLANES)
    tn = _pick(math.gcd(d, gate_off), tn_pref, LANES)
    od, ow = gate_off // tn, (gate_off + d) // tn
    nb = d // tn
    bg = b_gate.reshape(1, N_BRANCH * d)
    return pl.pallas_call(
        _merge_kernel,
        grid=(s // tm, nb),
        in_specs=[pl.BlockSpec((tm, kd), lambda i, j: (i, 0)),
                  pl.BlockSpec((tm, kw), lambda i, j: (i, 0)),
                  pl.BlockSpec((kd, tn), lambda i, j: (0, j)),
                  pl.BlockSpec((kw, tn), lambda i, j: (0, j)),
                  pl.BlockSpec((tm, tn), lambda i, j: (i, od + j)),
                  pl.BlockSpec((tm, tn), lambda i, j: (i, ow + j)),
                  pl.BlockSpec((1, tn), lambda i, j: (0, j)),
                  pl.BlockSpec((1, tn), lambda i, j: (0, nb + j))],
        out_specs=pl.BlockSpec((tm, tn), lambda i, j: (i, j)),
        out_shape=jax.ShapeDtypeStruct((s, d), BF16),
        compiler_params=_params(("arbitrary", "arbitrary"),
                                tm * (kd + kw) * 2 + (kd + kw) * tn * 2 + 2 * tm * tn * 4 + tm * tn * 2,
                                scratch_bytes=4 * tm * tn * 4),
        name="merge",
    )(y_da, y_wa, w_o_da, w_o_wa, pr, pr, bg, bg)


def _mm_res_kernel(a_ref, w_ref, x_ref, g_ref, o_ref, acc_ref):
    kk = pl.program_id(2)

    @pl.when(kk == 0)
    def _():
        acc_ref[...] = jnp.zeros_like(acc_ref)

    acc_ref[...] += jnp.dot(a_ref[...], w_ref[...], preferred_element_type=F32)

    @pl.when(kk == pl.num_programs(2) - 1)
    def _():
        o_ref[...] = x_ref[...] + g_ref[...] * acc_ref[...]


def _matmul_residual(a, w, x, gate, tm_pref, tn_pref, tk_pref, name):
    m, k = a.shape
    n = w.shape[1]
    tm = _pick(m, tm_pref, BF16_SUBLANES)
    tn = _pick(n, tn_pref, LANES)
    tk = _pick(k, tk_pref, LANES)
    return pl.pallas_call(
        _mm_res_kernel,
        grid=(m // tm, n // tn, k // tk),
        in_specs=[pl.BlockSpec((tm, tk), lambda i, j, kk: (i, kk)),
                  pl.BlockSpec((tk, tn), lambda i, j, kk: (kk, j)),
                  pl.BlockSpec((tm, tn), lambda i, j, kk: (i, j)),
                  pl.BlockSpec((1, tn), lambda i, j, kk: (0, j))],
        out_specs=pl.BlockSpec((tm, tn), lambda i, j, kk: (i, j)),
        out_shape=jax.ShapeDtypeStruct((m, n), F32),
        scratch_shapes=[pltpu.VMEM((tm, tn), F32)],
        compiler_params=_params(("arbitrary", "arbitrary", "arbitrary"),
                                tm * tk * 2 + tk * tn * 2 + 2 * tm * tn * 4,
                                scratch_bytes=2 * tm * tn * 4),
        name=name,
    )(a, w, x, gate)


HALO = BF16_SUBLANES


def _ffn_up_kernel(h_ref, hp_ref, hn_ref, wa_ref, wu_ref, cw_ref, cb_ref, o_ref, lhs_ref):
    i = pl.program_id(0)
    j = pl.program_id(1)
    tm = h_ref.shape[0]

    @pl.when(j == 0)
    def _():
        prev = jnp.where(i > 0, hp_ref[...], jnp.zeros_like(hp_ref))
        nxt = jnp.where(i < pl.num_programs(0) - 1, hn_ref[...], jnp.zeros_like(hn_ref))
        lhs_ref[0:HALO, :] = prev
        lhs_ref[HALO:HALO + tm, :] = h_ref[...]
        lhs_ref[HALO + tm:, :] = nxt

    a = jnp.dot(lhs_ref[...], wa_ref[...], preferred_element_type=F32)
    u = jnp.dot(lhs_ref[HALO:HALO + tm, :], wu_ref[...], preferred_element_type=F32)
    rows = tm + 2 * HALO
    a_prev = pltpu.roll(a, 1, 0)[HALO:HALO + tm, :]
    a_next = pltpu.roll(a, rows - 1, 0)[HALO:HALO + tm, :]
    a_mid = a[HALO:HALO + tm, :]
    conv = cb_ref[...] + a_prev * cw_ref[0:1, :] + a_mid * cw_ref[1:2, :] + a_next * cw_ref[2:3, :]
    o_ref[...] = (conv * _sigmoid(conv) * u).astype(BF16)


def _ffn_up(h, w_gate, w_lin, conv_w, conv_b, tm_pref=1024, tn_pref=512):
    s, d = h.shape
    ffp = w_gate.shape[1]
    tm = _pick(s, tm_pref, HALO)
    tn = _pick(ffp, tn_pref, LANES)
    nf = ffp // tn
    per = tm // HALO
    last = s // HALO - 1
    return pl.pallas_call(
        _ffn_up_kernel,
        grid=(s // tm, nf),
        in_specs=[pl.BlockSpec((tm, d), lambda i, j: (i, 0)),
                  pl.BlockSpec((HALO, d), lambda i, j: (jnp.maximum(i * per - 1, 0), 0)),
                  pl.BlockSpec((HALO, d), lambda i, j: (jnp.minimum((i + 1) * per, last), 0)),
                  pl.BlockSpec((d, tn), lambda i, j: (0, j)),
                  pl.BlockSpec((d, tn), lambda i, j: (0, j)),
                  pl.BlockSpec((CONV_W, tn), lambda i, j: (0, j)),
                  pl.BlockSpec((1, tn), lambda i, j: (0, j))],
        out_specs=pl.BlockSpec((tm, tn), lambda i, j: (i, j)),
        out_shape=jax.ShapeDtypeStruct((s, ffp), BF16),
        scratch_shapes=[pltpu.VMEM((tm + 2 * HALO, d), BF16)],
        compiler_params=_params(("arbitrary", "arbitrary"),
                                (tm + 2 * HALO) * d * 2 + 2 * d * tn * 2 + tm * tn * 2,
                                scratch_bytes=(tm + 2 * HALO) * d * 2 + 6 * (tm + 2 * HALO) * tn * 4),
        name="ffn_up",
    )(h, h, h, w_gate, w_lin, conv_w, conv_b)


def _split_cast_kernel(wg_ref, wl_ref, og_ref, ol_ref, *, n_valid):
    j = pl.program_id(0)

    @pl.when(j < n_valid)
    def _():
        og_ref[...] = wg_ref[...].astype(BF16)
        ol_ref[...] = wl_ref[...].astype(BF16)

    @pl.when(j >= n_valid)
    def _():
        og_ref[...] = jnp.zeros_like(og_ref)
        ol_ref[...] = jnp.zeros_like(ol_ref)


def _split_cast_pad(w, d_ff, ffp):
    d = w.shape[0]
    assert d_ff % LANES == 0 and ffp % LANES == 0
    tw = _pick(math.gcd(d_ff, ffp), 256, LANES)
    n_valid, n_tiles = d_ff // tw, ffp // tw
    out = jax.ShapeDtypeStruct((d, ffp), BF16)
    return pl.pallas_call(
        functools.partial(_split_cast_kernel, n_valid=n_valid),
        grid=(n_tiles,),
        in_specs=[pl.BlockSpec((d, tw), lambda j: (0, jnp.minimum(j, n_valid - 1))),
                  pl.BlockSpec((d, tw), lambda j: (0, n_valid + jnp.minimum(j, n_valid - 1)))],
        out_specs=[pl.BlockSpec((d, tw), lambda j: (0, j)), pl.BlockSpec((d, tw), lambda j: (0, j))],
        out_shape=[out, out],
        compiler_params=_params(("arbitrary",), 2 * d * tw * 4 + 2 * d * tw * 2),
        name="w_up_cast",
    )(w, w)


def _cast_pad_rows_kernel(w_ref, o_ref, *, n_valid):
    i = pl.program_id(0)

    @pl.when(i < n_valid)
    def _():
        o_ref[...] = w_ref[...].astype(BF16)

    @pl.when(i >= n_valid)
    def _():
        o_ref[...] = jnp.zeros_like(o_ref)


def _cast_pad_rows(w, rows_p):
    k, n = w.shape
    assert k % LANES == 0 and rows_p % LANES == 0
    tr = _pick(math.gcd(k, rows_p), 256, LANES)
    n_valid = k // tr
    return pl.pallas_call(
        functools.partial(_cast_pad_rows_kernel, n_valid=n_valid),
        grid=(rows_p // tr,),
        in_specs=[pl.BlockSpec((tr, n), lambda i: (jnp.minimum(i, n_valid - 1), 0))],
        out_specs=pl.BlockSpec((tr, n), lambda i: (i, 0)),
        out_shape=jax.ShapeDtypeStruct((rows_p, n), BF16),
        compiler_params=_params(("arbitrary",), tr * n * 6),
        name="w_down_cast",
    )(w)


def kernel(x, c, ctx, c_ctx, w_ada, b_ada, attn_norm_g, w_in, b_gate, da_qn_g, da_kn_g, da_lambda_q1, da_lambda_k1, da_lambda_q2, da_lambda_k2, da_subln_g, wa_qn_g, wa_kn_g, wa_sink, w_o_da, w_o_wa, w_out, ffn_norm_g, w_ffn_up, ffn_conv_w, ffn_conv_b, w_ffn_down):
    b, s, d = x.shape
    cl = ctx.shape[1]
    assert b == 1 and w_in.shape[0] == 1, "one batch element, one layer"
    da_heads = w_o_da.shape[1] // DA_V_DIM
    wa_q_heads = w_o_wa.shape[1] // WA_HEAD_DIM
    wa_kv_heads = wa_q_heads // WA_GROUP
    d_ff = ffn_conv_b.shape[1]
    gate_off = (3 * da_heads + wa_q_heads + 2 * wa_kv_heads) * LANES
    assert w_in.shape[2] == gate_off + N_BRANCH * d

    x2 = x.reshape(s, d)
    ctx2 = ctx.reshape(cl, d)

    ffp = -(-d_ff // 1024) * 1024 if d_ff > 1024 else d_ff
    padf = ffp - d_ff
    w_in_b = w_in[0].astype(BF16)
    w_o_da_b = w_o_da[0].astype(BF16)
    w_o_wa_b = w_o_wa[0].astype(BF16)
    w_out_b = w_out[0].astype(BF16)
    w_gate_b, w_lin_b = _split_cast_pad(w_ffn_up[0], d_ff, ffp)
    w_down_b = _cast_pad_rows(w_ffn_down[0], ffp)
    conv_w = jnp.pad(ffn_conv_w[0], ((0, 0), (0, padf)))
    conv_b = jnp.pad(ffn_conv_b[0], (0, padf)).reshape(1, ffp)

    cc = jnp.concatenate([c, c_ctx[None, :], jnp.zeros((6, d), F32)], axis=0)
    mod = _adaln(cc, w_ada[0], b_ada[0])
    sh1, sc1, g1 = mod[0:1, 0:d], mod[0:1, d:2 * d], mod[0:1, 2 * d:3 * d]
    sh2, sc2, g2 = mod[0:1, 3 * d:4 * d], mod[0:1, 4 * d:5 * d], mod[0:1, 5 * d:6 * d]
    mod_x1 = jnp.concatenate([sh1, sc1], axis=0)
    mod_c1 = jnp.concatenate([mod[1:2, 0:d], mod[1:2, d:2 * d]], axis=0)
    mod_x2 = jnp.concatenate([sh2, sc2], axis=0)

    h = _normmod(x2, attn_norm_g[0], mod_x1, ctx2, mod_c1)
    pr = _matmul(h, w_in_b, 1280, 512)
    tabs = _rope_tables(s, cl)
    gains = jnp.stack([jnp.tile(da_qn_g[0] * (DA_SCALE * LOG2_E), 2), jnp.tile(da_kn_g[0], 2),
                       wa_qn_g[0] * (WA_SCALE * LOG2_E), wa_kn_g[0]], axis=0)
    qkv = _headprep(pr, tabs, gains, da_heads, wa_q_heads)
    lam_vecs = jnp.stack([da_lambda_q1[0], da_lambda_k1[0], da_lambda_q2[0], da_lambda_k2[0]], axis=0)
    y_da = _diffattn(qkv, lam_vecs, da_subln_g[0], s, da_heads)
    y_wa = _winattn(qkv, wa_sink[0], s, cl, da_heads, wa_q_heads)
    u = _merge(y_da, y_wa, w_o_da_b, w_o_wa_b, pr, gate_off, b_gate[0])
    x1 = _matmul_residual(u, w_out_b, x2, g1, 1024, 1024, d, "out_proj")

    h2 = _normmod(x1, ffn_norm_g[0], mod_x2)
    act = _ffn_up(h2, w_gate_b, w_lin_b, conv_w, conv_b)
    out = _matmul_residual(act, w_down_b, x1, g2, 1024, 1024, 2816, "ffn_down")
    return out.reshape(b, s, d)
```

```python
import functools
import math

import jax
import jax.numpy as jnp
from jax import lax
from jax.experimental import pallas as pl
from jax.experimental.pallas import tpu as pltpu

F32 = jnp.float32
BF16 = jnp.bfloat16

GRID_W = 64
DA_QK_DIM = 64
DA_V_DIM = 2 * DA_QK_DIM
WA_HEAD_DIM = 128
WA_GROUP = 4
WINDOW = 128
N_BRANCH = 2
CONV_W = 3
ROPE_BASE = 10000.0
EPS = 1e-6
DA_SCALE = DA_QK_DIM ** -0.5
WA_SCALE = WA_HEAD_DIM ** -0.5
NEG_INF = -1e30
LAM_INIT = 0.8 - 0.6 * math.exp(-0.3 * 0)

LANES = 128
BF16_SUBLANES = 16
V7X_VMEM_CAP_BYTES = 58 * 1024 * 1024
VMEM_SLACK_BYTES = 6 * 1024 * 1024


def _pick(dim, pref, unit):
    best = None
    t = unit
    while t <= min(dim, pref):
        if dim % t == 0:
            best = t
        t += unit
    if best is None:
        raise ValueError(f"no tile for dim={dim} unit={unit}")
    return best


def _params(semantics, block_bytes, scratch_bytes=0):
    est = 2 * block_bytes + scratch_bytes + VMEM_SLACK_BYTES
    return pltpu.CompilerParams(
        dimension_semantics=semantics,
        vmem_limit_bytes=min(max(est, 16 * 1024 * 1024), V7X_VMEM_CAP_BYTES))


def _sigmoid(x):
    return 1.0 / (1.0 + jnp.exp(-x))


def _adaln_kernel(c_ref, w_ref, b_ref, o_ref):
    a = c_ref[...]
    a = a * _sigmoid(a)
    o_ref[...] = jnp.dot(a.astype(BF16), w_ref[...].astype(BF16),
                         preferred_element_type=F32) + b_ref[...]


def _adaln(cc, w, b):
    rows, d = cc.shape
    n = w.shape[1]
    tn = _pick(n, 512, LANES)
    return pl.pallas_call(
        _adaln_kernel,
        grid=(n // tn,),
        in_specs=[pl.BlockSpec((rows, d), lambda j: (0, 0)),
                  pl.BlockSpec((d, tn), lambda j: (0, j)),
                  pl.BlockSpec((1, tn), lambda j: (0, j))],
        out_specs=pl.BlockSpec((rows, tn), lambda j: (0, j)),
        out_shape=jax.ShapeDtypeStruct((rows, n), F32),
        compiler_params=_params(("arbitrary",), d * tn * 4 + d * tn * 2),
        name="adaln",
    )(cc, w, b.reshape(1, n))


def _normmod_rows(x, g, mod):
    ms = jnp.mean(x * x, axis=-1, keepdims=True)
    y = x * lax.rsqrt(ms + EPS) * g
    return (y * (1.0 + mod[1:2, :]) + mod[0:1, :]).astype(BF16)


def _normmod2_kernel(x_ref, c_ref, g_ref, mx_ref, mc_ref, o_ref, *, nx):
    i = pl.program_id(0)

    @pl.when(i < nx)
    def _():
        o_ref[...] = _normmod_rows(x_ref[...], g_ref[...], mx_ref[...])

    @pl.when(i >= nx)
    def _():
        o_ref[...] = _normmod_rows(c_ref[...], g_ref[...], mc_ref[...])


def _normmod1_kernel(x_ref, g_ref, mx_ref, o_ref):
    o_ref[...] = _normmod_rows(x_ref[...], g_ref[...], mx_ref[...])


def _normmod(x, g, mod_x, ctx=None, mod_c=None):
    s, d = x.shape
    g = g.reshape(1, d)
    if ctx is None:
        tm = _pick(s, 256, BF16_SUBLANES)
        return pl.pallas_call(
            _normmod1_kernel,
            grid=(s // tm,),
            in_specs=[pl.BlockSpec((tm, d), lambda i: (i, 0)),
                      pl.BlockSpec((1, d), lambda i: (0, 0)),
                      pl.BlockSpec((2, d), lambda i: (0, 0))],
            out_specs=pl.BlockSpec((tm, d), lambda i: (i, 0)),
            out_shape=jax.ShapeDtypeStruct((s, d), BF16),
            compiler_params=_params(("arbitrary",), tm * d * 6),
            name="normmod",
        )(x, g, mod_x)
    c = ctx.shape[0]
    tm = _pick(math.gcd(s, c), 256, BF16_SUBLANES)
    nx, nc = s // tm, c // tm
    return pl.pallas_call(
        functools.partial(_normmod2_kernel, nx=nx),
        grid=(nx + nc,),
        in_specs=[pl.BlockSpec((tm, d), lambda i: (jnp.minimum(i, nx - 1), 0)),
                  pl.BlockSpec((tm, d), lambda i: (jnp.maximum(i - nx, 0), 0)),
                  pl.BlockSpec((1, d), lambda i: (0, 0)),
                  pl.BlockSpec((2, d), lambda i: (0, 0)),
                  pl.BlockSpec((2, d), lambda i: (0, 0))],
        out_specs=pl.BlockSpec((tm, d), lambda i: (i, 0)),
        out_shape=jax.ShapeDtypeStruct((s + c, d), BF16),
        compiler_params=_params(("arbitrary",), tm * d * 10),
        name="normmod_xc",
    )(x, ctx, g, mod_x, mod_c)


def _mm_kernel(a_ref, w_ref, o_ref):
    o_ref[...] = jnp.dot(a_ref[...], w_ref[...], preferred_element_type=F32)


def _matmul(a, w, tm_pref, tn_pref):
    m, k = a.shape
    n = w.shape[1]
    tm = _pick(m, tm_pref, BF16_SUBLANES)
    tn = _pick(n, tn_pref, LANES)
    return pl.pallas_call(
        _mm_kernel,
        grid=(m // tm, n // tn),
        in_specs=[pl.BlockSpec((tm, k), lambda i, j: (i, 0)),
                  pl.BlockSpec((k, tn), lambda i, j: (0, j))],
        out_specs=pl.BlockSpec((tm, tn), lambda i, j: (i, j)),
        out_shape=jax.ShapeDtypeStruct((m, n), F32),
        compiler_params=_params(("arbitrary", "arbitrary"),
                                tm * k * 2 + k * tn * 2 + tm * tn * 4, scratch_bytes=tm * tn * 4),
        name="in_proj",
    )(a, w)


def _rope_tables(s, c):
    pos = jnp.arange(s, dtype=jnp.int32)
    r = (pos // GRID_W).astype(F32)[:, None]
    col = (pos % GRID_W).astype(F32)[:, None]
    lane = jnp.arange(LANES, dtype=jnp.int32)

    def table(head_dim):
        axis_dim = head_dim // 2
        pair = axis_dim // 2
        within = lane % head_dim
        f = (within % pair).astype(F32)
        freq = ROPE_BASE ** (-(2.0 * f) / axis_dim)
        use_row = (within // axis_dim) == 0
        ang = jnp.where(use_row[None, :], r * freq[None, :], col * freq[None, :])
        first = (within % axis_dim) < pair
        cos = jnp.cos(ang)
        sin = jnp.where(first[None, :], -jnp.sin(ang), jnp.sin(ang))
        cos = jnp.concatenate([cos, jnp.ones((c, LANES), F32)], axis=0)
        sin = jnp.concatenate([sin, jnp.zeros((c, LANES), F32)], axis=0)
        return cos, sin

    cd, sd = table(DA_QK_DIM)
    cw, sw = table(WA_HEAD_DIM)
    return jnp.stack([cd, sd, cw, sw], axis=0)


def _headprep_kernel(pr_ref, tab_ref, gain_ref, ones_da_ref, ones_wa_ref, o_ref,
                     *, n_qk_da, n_v_da, n_q_wa, n_k_wa, n_v_wa):
    tm = pr_ref.shape[0]
    lane = lax.broadcasted_iota(jnp.int32, (tm, LANES), 1)
    first_da = (lane % (DA_QK_DIM // 2)) < (DA_QK_DIM // 4)
    first_wa = (lane % (WA_HEAD_DIM // 2)) < (WA_HEAD_DIM // 4)
    cos_da, sin_da = tab_ref[0], tab_ref[1]
    cos_wa, sin_wa = tab_ref[2], tab_ref[3]

    def rope(y, cos, sin, first, shift):
        partner = jnp.where(first, pltpu.roll(y, LANES - shift, 1), pltpu.roll(y, shift, 1))
        return y * cos + partner * sin

    def head_sum_sq(x, ones_ref):
        x2 = x * x
        hi = x2.astype(BF16)
        mid = (x2 - hi.astype(F32)).astype(BF16)
        return jnp.dot(jnp.concatenate([hi, mid], axis=1), ones_ref[...], preferred_element_type=F32)

    def da_group(gidx, gain):
        x = pr_ref[:, gidx * LANES:(gidx + 1) * LANES]
        ms = head_sum_sq(x, ones_da_ref) * (1.0 / DA_QK_DIM)
        y = x * lax.rsqrt(ms + EPS) * gain
        y = rope(y, cos_da, sin_da, first_da, DA_QK_DIM // 4)
        o_ref[:, gidx * LANES:(gidx + 1) * LANES] = y.astype(BF16)

    def wa_group(gidx, gain):
        x = pr_ref[:, gidx * LANES:(gidx + 1) * LANES]
        ms = head_sum_sq(x, ones_wa_ref) * (1.0 / WA_HEAD_DIM)
        y = x * lax.rsqrt(ms + EPS) * gain
        y = rope(y, cos_wa, sin_wa, first_wa, WA_HEAD_DIM // 4)
        o_ref[:, gidx * LANES:(gidx + 1) * LANES] = y.astype(BF16)

    def copy_group(gidx):
        o_ref[:, gidx * LANES:(gidx + 1) * LANES] = pr_ref[:, gidx * LANES:(gidx + 1) * LANES].astype(BF16)

    g = 0
    for _ in range(n_qk_da):
        da_group(g, gain_ref[0:1, :]); g += 1
    for _ in range(n_qk_da):
        da_group(g, gain_ref[1:2, :]); g += 1
    for _ in range(n_v_da):
        copy_group(g); g += 1
    for _ in range(n_q_wa):
        wa_group(g, gain_ref[2:3, :]); g += 1
    for _ in range(n_k_wa):
        wa_group(g, gain_ref[3:4, :]); g += 1
    for _ in range(n_v_wa):
        copy_group(g); g += 1


def _headprep(pr, tabs, gains, da_heads, wa_q_heads, tm_pref=256):
    rows = pr.shape[0]
    wa_kv_heads = wa_q_heads // WA_GROUP
    groups = dict(n_qk_da=da_heads, n_v_da=da_heads, n_q_wa=wa_q_heads,
                  n_k_wa=wa_kv_heads, n_v_wa=wa_kv_heads)
    width = (3 * da_heads + wa_q_heads + 2 * wa_kv_heads) * LANES
    tm = _pick(rows, tm_pref, BF16_SUBLANES)
    lane = jnp.arange(LANES)
    same_da_head = (lane[:, None] // DA_QK_DIM) == (lane[None, :] // DA_QK_DIM)
    ones_da = jnp.tile(same_da_head.astype(BF16), (2, 1))
    ones_wa = jnp.ones((2 * LANES, LANES), BF16)
    return pl.pallas_call(
        functools.partial(_headprep_kernel, **groups),
        grid=(rows // tm,),
        in_specs=[pl.BlockSpec((tm, width), lambda i: (i, 0)),
                  pl.BlockSpec((4, tm, LANES), lambda i: (0, i, 0)),
                  pl.BlockSpec((4, LANES), lambda i: (0, 0)),
                  pl.BlockSpec((2 * LANES, LANES), lambda i: (0, 0)),
                  pl.BlockSpec((2 * LANES, LANES), lambda i: (0, 0))],
        out_specs=pl.BlockSpec((tm, width), lambda i: (i, 0)),
        out_shape=jax.ShapeDtypeStruct((rows, width), BF16),
        compiler_params=_params(("arbitrary",), tm * width * 6 + 4 * tm * LANES * 4),
        name="headprep",
    )(pr, tabs, gains, ones_da, ones_wa)


def _nt_dot(a, b):
    return lax.dot_general(a, b, (((1,), (1,)), ((), ())), preferred_element_type=F32)


STEPS_PER_TRIP = 4
LOG2_E = math.log2(math.e)


def _diffattn_kernel(lam_ref, q_ref, k_ref, v_ref, g_ref, o_ref,
                     s_ref, p_ref, m_ref, al_ref, acc_ref, *, tk, rb):
    tq = q_ref.shape[0]
    nk = k_ref.shape[0] // tk
    q = q_ref[...]
    lane = lax.broadcasted_iota(jnp.int32, (tq, LANES), 1)
    zero = jnp.zeros_like(q)
    qs = (jnp.where(lane < DA_QK_DIM, q, zero), jnp.where(lane < DA_QK_DIM, zero, q))
    ones = jnp.ones((tk, LANES), BF16)

    def scores(c, slot):
        k = k_ref[pl.ds(pl.multiple_of(c * tk, tk), tk), :]
        for h in range(2):
            s_ref[slot, h, :, 0:tk] = _nt_dot(qs[h], k).astype(BF16)

    def softmax(slot):
        for h in range(2):
            for r in range(0, tq, rb):
                s = s_ref[slot, h, r:r + rb, 0:tk]
                m_old = m_ref[h, r:r + rb, :]
                m_new = jnp.maximum(m_old, jnp.max(s, axis=-1, keepdims=True).astype(F32))
                p_ref[slot, h, r:r + rb, 0:tk] = jnp.exp2(s - m_new.astype(BF16))
                m_ref[h, r:r + rb, :] = m_new
                al_ref[slot, h, r:r + rb, :] = jnp.exp2(m_old - m_new)

    def accumulate(c, slot):
        v = v_ref[pl.ds(pl.multiple_of(c * tk, tk), tk), :]
        v1 = jnp.concatenate([v, ones], axis=1)
        for h in range(2):
            acc_ref[h] = (al_ref[slot, h] * acc_ref[h]
                          + jnp.dot(p_ref[slot, h, :, 0:tk], v1, preferred_element_type=F32))

    def step(c, slot, with_softmax=True, with_scores=True):
        if with_softmax:
            softmax(1 - slot)
        accumulate(c, slot)
        if with_scores:
            scores(c + 2, slot)

    m_ref[...] = jnp.full(m_ref.shape, NEG_INF, F32)
    acc_ref[...] = jnp.zeros(acc_ref.shape, F32)
    scores(0, 0)
    scores(1, 1)
    softmax(0)

    def group(i, carry):
        for t in range(STEPS_PER_TRIP):
            step(STEPS_PER_TRIP * i + t, t % 2)
        return carry

    n_full = nk - 2
    lax.fori_loop(0, n_full // STEPS_PER_TRIP, group, 0)
    c = STEPS_PER_TRIP * (n_full // STEPS_PER_TRIP)
    while c < n_full:
        step(c, c % 2)
        c += 1
    step(c, c % 2, with_scores=False)
    step(c + 1, (c + 1) % 2, with_softmax=False, with_scores=False)
    a1, a2 = acc_ref[0], acc_ref[1]

    lam = (jnp.exp(jnp.sum(lam_ref[0:1, :] * lam_ref[1:2, :], axis=-1, keepdims=True))
           - jnp.exp(jnp.sum(lam_ref[2:3, :] * lam_ref[3:4, :], axis=-1, keepdims=True))
           + LAM_INIT)
    o = (a1[:, :DA_V_DIM] / a1[:, DA_V_DIM:]) - lam * (a2[:, :DA_V_DIM] / a2[:, DA_V_DIM:])
    ms = jnp.mean(o * o, axis=-1, keepdims=True)
    o = o * lax.rsqrt(ms + EPS) * g_ref[...] * (1.0 - LAM_INIT)
    o_ref[...] = o.astype(BF16)


def _diffattn(qkv, lam_vecs, subln_g, s, da_heads, tq_pref=512, tk_pref=1280):
    rows = qkv.shape[0]
    tq = _pick(s, tq_pref, BF16_SUBLANES)
    tk = _pick(rows, min(tk_pref, rows // 2), LANES)
    kb, vb = da_heads, 2 * da_heads
    return pl.pallas_call(
        functools.partial(_diffattn_kernel, tk=tk, rb=2 * BF16_SUBLANES),
        grid=(da_heads, s // tq),
        in_specs=[pl.BlockSpec((4, DA_QK_DIM), lambda h, i: (0, 0)),
                  pl.BlockSpec((tq, LANES), lambda h, i: (i, h)),
                  pl.BlockSpec((rows, LANES), lambda h, i: (0, kb + h)),
                  pl.BlockSpec((rows, LANES), lambda h, i: (0, vb + h)),
                  pl.BlockSpec((1, DA_V_DIM), lambda h, i: (0, 0))],
        out_specs=pl.BlockSpec((tq, DA_V_DIM), lambda h, i: (i, h)),
        out_shape=jax.ShapeDtypeStruct((s, da_heads * DA_V_DIM), BF16),
        scratch_shapes=[pltpu.VMEM((2, 2, tq, tk + LANES), BF16),
                        pltpu.VMEM((2, 2, tq, tk + LANES), BF16),
                        pltpu.VMEM((2, tq, 1), F32),
                        pltpu.VMEM((2, 2, tq, 1), F32),
                        pltpu.VMEM((2, tq, 2 * DA_V_DIM), F32)],
        compiler_params=_params(("arbitrary", "arbitrary"),
                                2 * rows * LANES * 2 + 2 * tq * LANES * 2,
                                scratch_bytes=4 * tq * tk * 6 + 4 * tq * tk * 4),
        name="diff_attn",
    )(lam_vecs, qkv, qkv, qkv, subln_g.reshape(1, DA_V_DIM))


def _winattn_kernel(sink_ref, q_ref, k_ref, v_ref, o_ref, *, s, c):
    g = pl.program_id(0)
    i = pl.program_id(1)
    tq = q_ref.shape[0]
    band = 3 * WINDOW
    rows = WA_GROUP * WINDOW
    kc = k_ref[s:s + c, :]
    vc = v_ref[s:s + c, :]
    row = lax.broadcasted_iota(jnp.int32, (rows, band), 0)
    col = lax.broadcasted_iota(jnp.int32, (rows, band), 1)
    head = lax.broadcasted_iota(jnp.int32, (rows, 1), 0) // WINDOW
    sink = jnp.zeros((rows, 1), F32)
    for r in range(WA_GROUP):
        sink = jnp.where(head == r, sink_ref[g * WA_GROUP + r] * LOG2_E, sink)
    for b in range(tq // WINDOW):
        q0 = i * tq + b * WINDOW
        start = pl.multiple_of(jnp.clip(q0 - WINDOW, 0, s - band), WINDOW)
        kb = k_ref[pl.ds(start, band), :]
        vb = v_ref[pl.ds(start, band), :]
        valid = jnp.abs(q0 + row % WINDOW - (start + col)) <= WINDOW
        q = jnp.concatenate(
            [q_ref[b * WINDOW:(b + 1) * WINDOW, r * WA_HEAD_DIM:(r + 1) * WA_HEAD_DIM]
             for r in range(WA_GROUP)], axis=0)
        sb = jnp.where(valid, _nt_dot(q, kb), NEG_INF)
        sc = _nt_dot(q, kc)
        m = jnp.maximum(jnp.max(sb, axis=-1, keepdims=True), jnp.max(sc, axis=-1, keepdims=True))
        m = jnp.maximum(m, sink)
        pb = jnp.exp2(sb - m)
        pc = jnp.exp2(sc - m)
        l = (jnp.sum(pb, axis=-1, keepdims=True) + jnp.sum(pc, axis=-1, keepdims=True)
             + jnp.exp2(sink - m))
        o = (jnp.dot(pb.astype(BF16), vb, preferred_element_type=F32)
             + jnp.dot(pc.astype(BF16), vc, preferred_element_type=F32)) * (1.0 / l)
        for r in range(WA_GROUP):
            o_ref[b * WINDOW:(b + 1) * WINDOW, r * WA_HEAD_DIM:(r + 1) * WA_HEAD_DIM] = (
                o[r * WINDOW:(r + 1) * WINDOW].astype(BF16))


def _winattn(qkv, sink, s, c, da_heads, wa_q_heads, tq_pref=512):
    rows = qkv.shape[0]
    kvh = wa_q_heads // WA_GROUP
    tq = _pick(s, tq_pref, WINDOW)
    qb = 3 * da_heads // WA_GROUP
    kb = 3 * da_heads + wa_q_heads
    vb = kb + kvh
    gw = WA_GROUP * WA_HEAD_DIM
    return pl.pallas_call(
        functools.partial(_winattn_kernel, s=s, c=c),
        grid=(kvh, s // tq),
        in_specs=[pl.BlockSpec(memory_space=pltpu.SMEM),
                  pl.BlockSpec((tq, gw), lambda g, i: (i, qb + g)),
                  pl.BlockSpec((rows, LANES), lambda g, i: (0, kb + g)),
                  pl.BlockSpec((rows, LANES), lambda g, i: (0, vb + g))],
        out_specs=pl.BlockSpec((tq, gw), lambda g, i: (i, g)),
        out_shape=jax.ShapeDtypeStruct((s, wa_q_heads * WA_HEAD_DIM), BF16),
        compiler_params=_params(("arbitrary", "arbitrary"),
                                2 * rows * LANES * 2 + 2 * tq * gw * 2,
                                scratch_bytes=8 * tq * (tq + 2 * WINDOW + c) * 4),
        name="win_attn",
    )(sink, qkv, qkv, qkv)


def _merge_kernel(yd_ref, yw_ref, wd_ref, ww_ref, gd_ref, gw_ref, bd_ref, bw_ref, o_ref):
    pd = jnp.dot(yd_ref[...], wd_ref[...], preferred_element_type=F32)
    pw = jnp.dot(yw_ref[...], ww_ref[...], preferred_element_type=F32)
    gd = _sigmoid(gd_ref[...] + bd_ref[...])
    gw = _sigmoid(gw_ref[...] + bw_ref[...])
    o_ref[...] = (gd * pd + gw * pw).astype(BF16)


def _merge(y_da, y_wa, w_o_da, w_o_wa, pr, gate_off, b_gate, tm_pref=512, tn_pref=1024):
    s, kd = y_da.shape
    kw = y_wa.shape[1]
    d = w_o_da.shape[1]
    tm = _pick(s, tm_pref, BF16_SUBLANES)
    tn = _pick(math.gcd(d, gate_off), tn_pref, LANES)
    od, ow = gate_off // tn, (gate_off + d) // tn
    nb = d // tn
    bg = b_gate.reshape(1, N_BRANCH * d)
    return pl.pallas_call(
        _merge_kernel,
        grid=(s // tm, nb),
        in_specs=[pl.BlockSpec((tm, kd), lambda i, j: (i, 0)),
                  pl.BlockSpec((tm, kw), lambda i, j: (i, 0)),
                  pl.BlockSpec((kd, tn), lambda i, j: (0, j)),
                  pl.BlockSpec((kw, tn), lambda i, j: (0, j)),
                  pl.BlockSpec((tm, tn), lambda i, j: (i, od + j)),
                  pl.BlockSpec((tm, tn), lambda i, j: (i, ow + j)),
                  pl.BlockSpec((1, tn), lambda i, j: (0, j)),
                  pl.BlockSpec((1, tn), lambda i, j: (0, nb + j))],
        out_specs=pl.BlockSpec((tm, tn), lambda i, j: (i, j)),
        out_shape=jax.ShapeDtypeStruct((s, d), BF16),
        compiler_params=_params(("arbitrary", "arbitrary"),
                                tm * (kd + kw) * 2 + (kd + kw) * tn * 2 + 2 * tm * tn * 4 + tm * tn * 2,
                                scratch_bytes=4 * tm * tn * 4),
        name="merge",
    )(y_da, y_wa, w_o_da, w_o_wa, pr, pr, bg, bg)


def _mm_res_kernel(a_ref, w_ref, x_ref, g_ref, o_ref, acc_ref):
    kk = pl.program_id(2)

    @pl.when(kk == 0)
    def _():
        acc_ref[...] = jnp.zeros_like(acc_ref)

    acc_ref[...] += jnp.dot(a_ref[...], w_ref[...], preferred_element_type=F32)

    @pl.when(kk == pl.num_programs(2) - 1)
    def _():
        o_ref[...] = x_ref[...] + g_ref[...] * acc_ref[...]


def _matmul_residual(a, w, x, gate, tm_pref, tn_pref, tk_pref, name):
    m, k = a.shape
    n = w.shape[1]
    tm = _pick(m, tm_pref, BF16_SUBLANES)
    tn = _pick(n, tn_pref, LANES)
    tk = _pick(k, tk_pref, LANES)
    return pl.pallas_call(
        _mm_res_kernel,
        grid=(m // tm, n // tn, k // tk),
        in_specs=[pl.BlockSpec((tm, tk), lambda i, j, kk: (i, kk)),
                  pl.BlockSpec((tk, tn), lambda i, j, kk: (kk, j)),
                  pl.BlockSpec((tm, tn), lambda i, j, kk: (i, j)),
                  pl.BlockSpec((1, tn), lambda i, j, kk: (0, j))],
        out_specs=pl.BlockSpec((tm, tn), lambda i, j, kk: (i, j)),
        out_shape=jax.ShapeDtypeStruct((m, n), F32),
        scratch_shapes=[pltpu.VMEM((tm, tn), F32)],
        compiler_params=_params(("arbitrary", "arbitrary", "arbitrary"),
                                tm * tk * 2 + tk * tn * 2 + 2 * tm * tn * 4,
                                scratch_bytes=2 * tm * tn * 4),
        name=name,
    )(a, w, x, gate)


HALO = BF16_SUBLANES


def _ffn_up_kernel(h_ref, hp_ref, hn_ref, wa_ref, wu_ref, cw_ref, cb_ref, o_ref, lhs_ref):
    i = pl.program_id(0)
    j = pl.program_id(1)
    tm = h_ref.shape[0]

    @pl.when(j == 0)
    def _():
        prev = jnp.where(i > 0, hp_ref[...], jnp.zeros_like(hp_ref))
        nxt = jnp.where(i < pl.num_programs(0) - 1, hn_ref[...], jnp.zeros_like(hn_ref))
        lhs_ref[0:HALO, :] = prev
        lhs_ref[HALO:HALO + tm, :] = h_ref[...]
        lhs_ref[HALO + tm:, :] = nxt

    a = jnp.dot(lhs_ref[...], wa_ref[...], preferred_element_type=F32)
    u = jnp.dot(lhs_ref[HALO:HALO + tm, :], wu_ref[...], preferred_element_type=F32)
    rows = tm + 2 * HALO
    a_prev = pltpu.roll(a, 1, 0)[HALO:HALO + tm, :]
    a_next = pltpu.roll(a, rows - 1, 0)[HALO:HALO + tm, :]
    a_mid = a[HALO:HALO + tm, :]
    conv = cb_ref[...] + a_prev * cw_ref[0:1, :] + a_mid * cw_ref[1:2, :] + a_next * cw_ref[2:3, :]
    o_ref[...] = (conv * _sigmoid(conv) * u).astype(BF16)


def _ffn_up(h, w_gate, w_lin, conv_w, conv_b, tm_pref=1024, tn_pref=512):
    s, d = h.shape
    ffp = w_gate.shape[1]
    tm = _pick(s, tm_pref, HALO)
    tn = _pick(ffp, tn_pref, LANES)
    nf = ffp // tn
    per = tm // HALO
    last = s // HALO - 1
    return pl.pallas_call(
        _ffn_up_kernel,
        grid=(s // tm, nf),
        in_specs=[pl.BlockSpec((tm, d), lambda i, j: (i, 0)),
                  pl.BlockSpec((HALO, d), lambda i, j: (jnp.maximum(i * per - 1, 0), 0)),
                  pl.BlockSpec((HALO, d), lambda i, j: (jnp.minimum((i + 1) * per, last), 0)),
                  pl.BlockSpec((d, tn), lambda i, j: (0, j)),
                  pl.BlockSpec((d, tn), lambda i, j: (0, j)),
                  pl.BlockSpec((CONV_W, tn), lambda i, j: (0, j)),
                  pl.BlockSpec((1, tn), lambda i, j: (0, j))],
        out_specs=pl.BlockSpec((tm, tn), lambda i, j: (i, j)),
        out_shape=jax.ShapeDtypeStruct((s, ffp), BF16),
        scratch_shapes=[pltpu.VMEM((tm + 2 * HALO, d), BF16)],
        compiler_params=_params(("arbitrary", "arbitrary"),
                                (tm + 2 * HALO) * d * 2 + 2 * d * tn * 2 + tm * tn * 2,
                                scratch_bytes=(tm + 2 * HALO) * d * 2 + 6 * (tm + 2 * HALO) * tn * 4),
        name="ffn_up",
    )(h, h, h, w_gate, w_lin, conv_w, conv_b)


def _split_cast_kernel(wg_ref, wl_ref, og_ref, ol_ref, *, n_valid):
    j = pl.program_id(0)

    @pl.when(j < n_valid)
    def _():
        og_ref[...] = wg_ref[...].astype(BF16)
        ol_ref[...] = wl_ref[...].astype(BF16)

    @pl.when(j >= n_valid)
    def _():
        og_ref[...] = jnp.zeros_like(og_ref)
        ol_ref[...] = jnp.zeros_like(ol_ref)


def _split_cast_pad(w, d_ff, ffp):
    d = w.shape[0]
    assert d_ff % LANES == 0 and ffp % LANES == 0
    tw = _pick(math.gcd(d_ff, ffp), 256, LANES)
    n_valid, n_tiles = d_ff // tw, ffp // tw
    out = jax.ShapeDtypeStruct((d, ffp), BF16)
    return pl.pallas_call(
        functools.partial(_split_cast_kernel, n_valid=n_valid),
        grid=(n_tiles,),
        in_specs=[pl.BlockSpec((d, tw), lambda j: (0, jnp.minimum(j, n_valid - 1))),
                  pl.BlockSpec((d, tw), lambda j: (0, n_valid + jnp.minimum(j, n_valid - 1)))],
        out_specs=[pl.BlockSpec((d, tw), lambda j: (0, j)), pl.BlockSpec((d, tw), lambda j: (0, j))],
        out_shape=[out, out],
        compiler_params=_params(("arbitrary",), 2 * d * tw * 4 + 2 * d * tw * 2),
        name="w_up_cast",
    )(w, w)


def _cast_pad_rows_kernel(w_ref, o_ref, *, n_valid):
    i = pl.program_id(0)

    @pl.when(i < n_valid)
    def _():
        o_ref[...] = w_ref[...].astype(BF16)

    @pl.when(i >= n_valid)
    def _():
        o_ref[...] = jnp.zeros_like(o_ref)


def _cast_pad_rows(w, rows_p):
    k, n = w.shape
    assert k % LANES == 0 and rows_p % LANES == 0
    tr = _pick(math.gcd(k, rows_p), 256, LANES)
    n_valid = k // tr
    return pl.pallas_call(
        functools.partial(_cast_pad_rows_kernel, n_valid=n_valid),
        grid=(rows_p // tr,),
        in_specs=[pl.BlockSpec((tr, n), lambda i: (jnp.minimum(i, n_valid - 1), 0))],
        out_specs=pl.BlockSpec((tr, n), lambda i: (i, 0)),
        out_shape=jax.ShapeDtypeStruct((rows_p, n), BF16),
        compiler_params=_params(("arbitrary",), tr * n * 6),
        name="w_down_cast",
    )(w)


def kernel(x, c, ctx, c_ctx, w_ada, b_ada, attn_norm_g, w_in, b_gate, da_qn_g, da_kn_g, da_lambda_q1, da_lambda_k1, da_lambda_q2, da_lambda_k2, da_subln_g, wa_qn_g, wa_kn_g, wa_sink, w_o_da, w_o_wa, w_out, ffn_norm_g, w_ffn_up, ffn_conv_w, ffn_conv_b, w_ffn_down):
    b, s, d = x.shape
    cl = ctx.shape[1]
    assert b == 1 and w_in.shape[0] == 1, "one batch element, one layer"
    da_heads = w_o_da.shape[1] // DA_V_DIM
    wa_q_heads = w_o_wa.shape[1] // WA_HEAD_DIM
    wa_kv_heads = wa_q_heads // WA_GROUP
    d_ff = ffn_conv_b.shape[1]
    gate_off = (3 * da_heads + wa_q_heads + 2 * wa_kv_heads) * LANES
    assert w_in.shape[2] == gate_off + N_BRANCH * d

    x2 = x.reshape(s, d)
    ctx2 = ctx.reshape(cl, d)

    ffp = -(-d_ff // 1024) * 1024 if d_ff > 1024 else d_ff
    padf = ffp - d_ff
    w_in_b = w_in[0].astype(BF16)
    w_o_da_b = w_o_da[0].astype(BF16)
    w_o_wa_b = w_o_wa[0].astype(BF16)
    w_out_b = w_out[0].astype(BF16)
    w_gate_b, w_lin_b = _split_cast_pad(w_ffn_up[0], d_ff, ffp)
    w_down_b = _cast_pad_rows(w_ffn_down[0], ffp)
    conv_w = jnp.pad(ffn_conv_w[0], ((0, 0), (0, padf)))
    conv_b = jnp.pad(ffn_conv_b[0], (0, padf)).reshape(1, ffp)

    cc = jnp.concatenate([c, c_ctx[None, :], jnp.zeros((6, d), F32)], axis=0)
    mod = _adaln(cc, w_ada[0], b_ada[0])
    sh1, sc1, g1 = mod[0:1, 0:d], mod[0:1, d:2 * d], mod[0:1, 2 * d:3 * d]
    sh2, sc2, g2 = mod[0:1, 3 * d:4 * d], mod[0:1, 4 * d:5 * d], mod[0:1, 5 * d:6 * d]
    mod_x1 = jnp.concatenate([sh1, sc1], axis=0)
    mod_c1 = jnp.concatenate([mod[1:2, 0:d], mod[1:2, d:2 * d]], axis=0)
    mod_x2 = jnp.concatenate([sh2, sc2], axis=0)

    h = _normmod(x2, attn_norm_g[0], mod_x1, ctx2, mod_c1)
    pr = _matmul(h, w_in_b, 1280, 512)
    tabs = _rope_tables(s, cl)
    gains = jnp.stack([jnp.tile(da_qn_g[0] * (DA_SCALE * LOG2_E), 2), jnp.tile(da_kn_g[0], 2),
                       wa_qn_g[0] * (WA_SCALE * LOG2_E), wa_kn_g[0]], axis=0)
    qkv = _headprep(pr, tabs, gains, da_heads, wa_q_heads)
    lam_vecs = jnp.stack([da_lambda_q1[0], da_lambda_k1[0], da_lambda_q2[0], da_lambda_k2[0]], axis=0)
    y_da = _diffattn(qkv, lam_vecs, da_subln_g[0], s, da_heads)
    y_wa = _winattn(qkv, wa_sink[0], s, cl, da_heads, wa_q_heads)
    u = _merge(y_da, y_wa, w_o_da_b, w_o_wa_b, pr, gate_off, b_gate[0])
    x1 = _matmul_residual(u, w_out_b, x2, g1, 1024, 1024, d, "out_proj")

    h2 = _normmod(x1, ffn_norm_g[0], mod_x2)
    act = _ffn_up(h2, w_gate_b, w_lin_b, conv_w, conv_b)
    out = _matmul_residual(act, w_down_b, x1, g2, 1024, 1024, 2816, "ffn_down")
    return out.reshape(b, s, d)
```

```python
import functools
import math

import jax
import jax.numpy as jnp
from jax import lax
from jax.experimental import pallas as pl
from jax.experimental.pallas import tpu as pltpu

F32 = jnp.float32
BF16 = jnp.bfloat16

GRID_W = 64
DA_QK_DIM = 64
DA_V_DIM = 2 * DA_QK_DIM
WA_HEAD_DIM = 128
WA_GROUP = 4
WINDOW = 128
N_BRANCH = 2
CONV_W = 3
ROPE_BASE = 10000.0
EPS = 1e-6
DA_SCALE = DA_QK_DIM ** -0.5
WA_SCALE = WA_HEAD_DIM ** -0.5
NEG_INF = -1e30
LAM_INIT = 0.8 - 0.6 * math.exp(-0.3 * 0)

LANES = 128
BF16_SUBLANES = 16
V7X_VMEM_CAP_BYTES = 58 * 1024 * 1024
VMEM_SLACK_BYTES = 6 * 1024 * 1024


def _pick(dim, pref, unit):
    best = None
    t = unit
    while t <= min(dim, pref):
        if dim % t == 0:
            best = t
        t += unit
    if best is None:
        raise ValueError(f"no tile for dim={dim} unit={unit}")
    return best


def _params(semantics, block_bytes, scratch_bytes=0):
    est = 2 * block_bytes + scratch_bytes + VMEM_SLACK_BYTES
    return pltpu.CompilerParams(
        dimension_semantics=semantics,
        vmem_limit_bytes=min(max(est, 16 * 1024 * 1024), V7X_VMEM_CAP_BYTES))


def _sigmoid(x):
    return 1.0 / (1.0 + jnp.exp(-x))


def _adaln_kernel(c_ref, w_ref, b_ref, o_ref):
    a = c_ref[...]
    a = a * _sigmoid(a)
    o_ref[...] = jnp.dot(a.astype(BF16), w_ref[...].astype(BF16),
                         preferred_element_type=F32) + b_ref[...]


def _adaln(cc, w, b):
    rows, d = cc.shape
    n = w.shape[1]
    tn = _pick(n, 512, LANES)
    return pl.pallas_call(
        _adaln_kernel,
        grid=(n // tn,),
        in_specs=[pl.BlockSpec((rows, d), lambda j: (0, 0)),
                  pl.BlockSpec((d, tn), lambda j: (0, j)),
                  pl.BlockSpec((1, tn), lambda j: (0, j))],
        out_specs=pl.BlockSpec((rows, tn), lambda j: (0, j)),
        out_shape=jax.ShapeDtypeStruct((rows, n), F32),
        compiler_params=_params(("arbitrary",), d * tn * 4 + d * tn * 2),
        name="adaln",
    )(cc, w, b.reshape(1, n))


def _normmod_rows(x, g, mod):
    ms = jnp.mean(x * x, axis=-1, keepdims=True)
    y = x * lax.rsqrt(ms + EPS) * g
    return (y * (1.0 + mod[1:2, :]) + mod[0:1, :]).astype(BF16)


def _normmod2_kernel(x_ref, c_ref, g_ref, mx_ref, mc_ref, o_ref, *, nx):
    i = pl.program_id(0)

    @pl.when(i < nx)
    def _():
        o_ref[...] = _normmod_rows(x_ref[...], g_ref[...], mx_ref[...])

    @pl.when(i >= nx)
    def _():
        o_ref[...] = _normmod_rows(c_ref[...], g_ref[...], mc_ref[...])


def _normmod1_kernel(x_ref, g_ref, mx_ref, o_ref):
    o_ref[...] = _normmod_rows(x_ref[...], g_ref[...], mx_ref[...])


def _normmod(x, g, mod_x, ctx=None, mod_c=None):
    s, d = x.shape
    g = g.reshape(1, d)
    if ctx is None:
        tm = _pick(s, 256, BF16_SUBLANES)
        return pl.pallas_call(
            _normmod1_kernel,
            grid=(s // tm,),
            in_specs=[pl.BlockSpec((tm, d), lambda i: (i, 0)),
                      pl.BlockSpec((1, d), lambda i: (0, 0)),
                      pl.BlockSpec((2, d), lambda i: (0, 0))],
            out_specs=pl.BlockSpec((tm, d), lambda i: (i, 0)),
            out_shape=jax.ShapeDtypeStruct((s, d), BF16),
            compiler_params=_params(("arbitrary",), tm * d * 6),
            name="normmod",
        )(x, g, mod_x)
    c = ctx.shape[0]
    tm = _pick(math.gcd(s, c), 256, BF16_SUBLANES)
    nx, nc = s // tm, c // tm
    return pl.pallas_call(
        functools.partial(_normmod2_kernel, nx=nx),
        grid=(nx + nc,),
        in_specs=[pl.BlockSpec((tm, d), lambda i: (jnp.minimum(i, nx - 1), 0)),
                  pl.BlockSpec((tm, d), lambda i: (jnp.maximum(i - nx, 0), 0)),
                  pl.BlockSpec((1, d), lambda i: (0, 0)),
                  pl.BlockSpec((2, d), lambda i: (0, 0)),
                  pl.BlockSpec((2, d), lambda i: (0, 0))],
        out_specs=pl.BlockSpec((tm, d), lambda i: (i, 0)),
        out_shape=jax.ShapeDtypeStruct((s + c, d), BF16),
        compiler_params=_params(("arbitrary",), tm * d * 10),
        name="normmod_xc",
    )(x, ctx, g, mod_x, mod_c)


def _mm_kernel(a_ref, w_ref, o_ref):
    o_ref[...] = jnp.dot(a_ref[...], w_ref[...], preferred_element_type=F32)


def _matmul(a, w, tm_pref, tn_pref):
    m, k = a.shape
    n = w.shape[1]
    tm = _pick(m, tm_pref, BF16_SUBLANES)
    tn = _pick(n, tn_pref, LANES)
    return pl.pallas_call(
        _mm_kernel,
        grid=(m // tm, n // tn),
        in_specs=[pl.BlockSpec((tm, k), lambda i, j: (i, 0)),
                  pl.BlockSpec((k, tn), lambda i, j: (0, j))],
        out_specs=pl.BlockSpec((tm, tn), lambda i, j: (i, j)),
        out_shape=jax.ShapeDtypeStruct((m, n), F32),
        compiler_params=_params(("arbitrary", "arbitrary"),
                                tm * k * 2 + k * tn * 2 + tm * tn * 4, scratch_bytes=tm * tn * 4),
        name="in_proj",
    )(a, w)


def _rope_tables(s, c):
    pos = jnp.arange(s, dtype=jnp.int32)
    r = (pos // GRID_W).astype(F32)[:, None]
    col = (pos % GRID_W).astype(F32)[:, None]
    lane = jnp.arange(LANES, dtype=jnp.int32)

    def table(head_dim):
        axis_dim = head_dim // 2
        pair = axis_dim // 2
        within = lane % head_dim
        f = (within % pair).astype(F32)
        freq = ROPE_BASE ** (-(2.0 * f) / axis_dim)
        use_row = (within // axis_dim) == 0
        ang = jnp.where(use_row[None, :], r * freq[None, :], col * freq[None, :])
        first = (within % axis_dim) < pair
        cos = jnp.cos(ang)
        sin = jnp.where(first[None, :], -jnp.sin(ang), jnp.sin(ang))
        cos = jnp.concatenate([cos, jnp.ones((c, LANES), F32)], axis=0)
        sin = jnp.concatenate([sin, jnp.zeros((c, LANES), F32)], axis=0)
        return cos, sin

    cd, sd = table(DA_QK_DIM)
    cw, sw = table(WA_HEAD_DIM)
    return jnp.stack([cd, sd, cw, sw], axis=0)


def _headprep_kernel(pr_ref, tab_ref, gain_ref, ones_da_ref, ones_wa_ref, o_ref,
                     *, n_qk_da, n_v_da, n_q_wa, n_k_wa, n_v_wa):
    tm = pr_ref.shape[0]
    lane = lax.broadcasted_iota(jnp.int32, (tm, LANES), 1)
    first_da = (lane % (DA_QK_DIM // 2)) < (DA_QK_DIM // 4)
    first_wa = (lane % (WA_HEAD_DIM // 2)) < (WA_HEAD_DIM // 4)
    cos_da, sin_da = tab_ref[0], tab_ref[1]
    cos_wa, sin_wa = tab_ref[2], tab_ref[3]

    def rope(y, cos, sin, first, shift):
        partner = jnp.where(first, pltpu.roll(y, LANES - shift, 1), pltpu.roll(y, shift, 1))
        return y * cos + partner * sin

    def head_sum_sq(x, ones_ref):
        x2 = x * x
        hi = x2.astype(BF16)
        mid = (x2 - hi.astype(F32)).astype(BF16)
        return jnp.dot(jnp.concatenate([hi, mid], axis=1), ones_ref[...], preferred_element_type=F32)

    def da_group(gidx, gain):
        x = pr_ref[:, gidx * LANES:(gidx + 1) * LANES]
        ms = head_sum_sq(x, ones_da_ref) * (1.0 / DA_QK_DIM)
        y = x * lax.rsqrt(ms + EPS) * gain
        y = rope(y, cos_da, sin_da, first_da, DA_QK_DIM // 4)
        o_ref[:, gidx * LANES:(gidx + 1) * LANES] = y.astype(BF16)

    def wa_group(gidx, gain):
        x = pr_ref[:, gidx * LANES:(gidx + 1) * LANES]
        ms = head_sum_sq(x, ones_wa_ref) * (1.0 / WA_HEAD_DIM)
        y = x * lax.rsqrt(ms + EPS) * gain
        y = rope(y, cos_wa, sin_wa, first_wa, WA_HEAD_DIM // 4)
        o_ref[:, gidx * LANES:(gidx + 1) * LANES] = y.astype(BF16)

    def copy_group(gidx):
        o_ref[:, gidx * LANES:(gidx + 1) * LANES] = pr_ref[:, gidx * LANES:(gidx + 1) * LANES].astype(BF16)

    g = 0
    for _ in range(n_qk_da):
        da_group(g, gain_ref[0:1, :]); g += 1
    for _ in range(n_qk_da):
        da_group(g, gain_ref[1:2, :]); g += 1
    for _ in range(n_v_da):
        copy_group(g); g += 1
    for _ in range(n_q_wa):
        wa_group(g, gain_ref[2:3, :]); g += 1
    for _ in range(n_k_wa):
        wa_group(g, gain_ref[3:4, :]); g += 1
    for _ in range(n_v_wa):
        copy_group(g); g += 1


def _headprep(pr, tabs, gains, da_heads, wa_q_heads, tm_pref=256):
    rows = pr.shape[0]
    wa_kv_heads = wa_q_heads // WA_GROUP
    groups = dict(n_qk_da=da_heads, n_v_da=da_heads, n_q_wa=wa_q_heads,
                  n_k_wa=wa_kv_heads, n_v_wa=wa_kv_heads)
    width = (3 * da_heads + wa_q_heads + 2 * wa_kv_heads) * LANES
    tm = _pick(rows, tm_pref, BF16_SUBLANES)
    lane = jnp.arange(LANES)
    same_da_head = (lane[:, None] // DA_QK_DIM) == (lane[None, :] // DA_QK_DIM)
    ones_da = jnp.tile(same_da_head.astype(BF16), (2, 1))
    ones_wa = jnp.ones((2 * LANES, LANES), BF16)
    return pl.pallas_call(
        functools.partial(_headprep_kernel, **groups),
        grid=(rows // tm,),
        in_specs=[pl.BlockSpec((tm, width), lambda i: (i, 0)),
                  pl.BlockSpec((4, tm, LANES), lambda i: (0, i, 0)),
                  pl.BlockSpec((4, LANES), lambda i: (0, 0)),
                  pl.BlockSpec((2 * LANES, LANES), lambda i: (0, 0)),
                  pl.BlockSpec((2 * LANES, LANES), lambda i: (0, 0))],
        out_specs=pl.BlockSpec((tm, width), lambda i: (i, 0)),
        out_shape=jax.ShapeDtypeStruct((rows, width), BF16),
        compiler_params=_params(("arbitrary",), tm * width * 6 + 4 * tm * LANES * 4),
        name="headprep",
    )(pr, tabs, gains, ones_da, ones_wa)


def _nt_dot(a, b):
    return lax.dot_general(a, b, (((1,), (1,)), ((), ())), preferred_element_type=F32)


STEPS_PER_TRIP = 4
LOG2_E = math.log2(math.e)


def _diffattn_kernel(lam_ref, q_ref, k_ref, v_ref, g_ref, o_ref,
                     s_ref, p_ref, m_ref, al_ref, acc_ref, *, tk, rb):
    tq = q_ref.shape[0]
    nk = k_ref.shape[0] // tk
    q = q_ref[...]
    lane = lax.broadcasted_iota(jnp.int32, (tq, LANES), 1)
    zero = jnp.zeros_like(q)
    qs = (jnp.where(lane < DA_QK_DIM, q, zero), jnp.where(lane < DA_QK_DIM, zero, q))
    ones = jnp.ones((tk, LANES), BF16)

    def scores(c, slot):
        k = k_ref[pl.ds(pl.multiple_of(c * tk, tk), tk), :]
        for h in range(2):
            s_ref[slot, h] = _nt_dot(qs[h], k).astype(BF16)

    def softmax(slot):
        for h in range(2):
            for r in range(0, tq, rb):
                s = s_ref[slot, h, r:r + rb, :]
                m_old = m_ref[h, r:r + rb, :]
                m_new = jnp.maximum(m_old, jnp.max(s, axis=-1, keepdims=True).astype(F32))
                p_ref[slot, h, r:r + rb, :] = jnp.exp2(s - m_new.astype(BF16))
                m_ref[h, r:r + rb, :] = m_new
                al_ref[slot, h, r:r + rb, :] = jnp.exp2(m_old - m_new)

    def accumulate(c, slot):
        v = v_ref[pl.ds(pl.multiple_of(c * tk, tk), tk), :]
        v1 = jnp.concatenate([v, ones], axis=1)
        for h in range(2):
            acc_ref[h] = (al_ref[slot, h] * acc_ref[h]
                          + jnp.dot(p_ref[slot, h], v1, preferred_element_type=F32))

    def step(c, slot, with_softmax=True, with_scores=True):
        if with_softmax:
            softmax(1 - slot)
        accumulate(c, slot)
        if with_scores:
            scores(c + 2, slot)

    m_ref[...] = jnp.full(m_ref.shape, NEG_INF, F32)
    acc_ref[...] = jnp.zeros(acc_ref.shape, F32)
    scores(0, 0)
    scores(1, 1)
    softmax(0)

    def group(i, carry):
        for t in range(STEPS_PER_TRIP):
            step(STEPS_PER_TRIP * i + t, t % 2)
        return carry

    n_full = nk - 2
    lax.fori_loop(0, n_full // STEPS_PER_TRIP, group, 0)
    c = STEPS_PER_TRIP * (n_full // STEPS_PER_TRIP)
    while c < n_full:
        step(c, c % 2)
        c += 1
    step(c, c % 2, with_scores=False)
    step(c + 1, (c + 1) % 2, with_softmax=False, with_scores=False)
    a1, a2 = acc_ref[0], acc_ref[1]

    lam = (jnp.exp(jnp.sum(lam_ref[0:1, :] * lam_ref[1:2, :], axis=-1, keepdims=True))
           - jnp.exp(jnp.sum(lam_ref[2:3, :] * lam_ref[3:4, :], axis=-1, keepdims=True))
           + LAM_INIT)
    o = (a1[:, :DA_V_DIM] / a1[:, DA_V_DIM:]) - lam * (a2[:, :DA_V_DIM] / a2[:, DA_V_DIM:])
    ms = jnp.mean(o * o, axis=-1, keepdims=True)
    o = o * lax.rsqrt(ms + EPS) * g_ref[...] * (1.0 - LAM_INIT)
    o_ref[...] = o.astype(BF16)


def _diffattn_values_kernel(lam_ref, q_ref, k_ref, v_ref, g_ref, o_ref, *, tk, rb):
    tq = q_ref.shape[0]
    nk = k_ref.shape[0] // tk
    q = q_ref[...]
    lane = lax.broadcasted_iota(jnp.int32, (tq, LANES), 1)
    zero = jnp.zeros_like(q)
    qs = (jnp.where(lane < DA_QK_DIM, q, zero), jnp.where(lane < DA_QK_DIM, zero, q))
    ones = jnp.ones((tk, LANES), BF16)

    def scores(c):
        k = k_ref[c * tk:(c + 1) * tk, :]
        return [_nt_dot(qs[h], k).astype(BF16) for h in range(2)]

    def softmax(ss, ms):
        ps, new_ms, alphas = [], [], []
        for h in range(2):
            p_blocks, m_blocks, a_blocks = [], [], []
            for r in range(0, tq, rb):
                s = ss[h][r:r + rb]
                m_old = ms[h][r:r + rb]
                m_new = jnp.maximum(m_old, jnp.max(s, axis=-1, keepdims=True).astype(F32))
                p_blocks.append(jnp.exp2(s - m_new.astype(BF16)))
                m_blocks.append(m_new)
                a_blocks.append(jnp.exp2(m_old - m_new))
            ps.append(jnp.concatenate(p_blocks, axis=0))
            new_ms.append(jnp.concatenate(m_blocks, axis=0))
            alphas.append(jnp.concatenate(a_blocks, axis=0))
        return ps, new_ms, alphas

    def accumulate(c, ps, alphas, accs):
        v1 = jnp.concatenate([v_ref[c * tk:(c + 1) * tk, :], ones], axis=1)
        return [alphas[h] * accs[h] + jnp.dot(ps[h], v1, preferred_element_type=F32)
                for h in range(2)]

    m0 = jnp.full((tq, 1), NEG_INF, F32)
    a0 = jnp.zeros((tq, 2 * DA_V_DIM), F32)
    ss = {0: scores(0), 1: scores(1)}
    p_cur, ms, al_cur = softmax(ss.pop(0), [m0, m0])
    accs = [a0, a0]
    for c in range(nk):
        if c + 1 < nk:
            p_next, ms, al_next = softmax(ss.pop(c + 1), ms)
        accs = accumulate(c, p_cur, al_cur, accs)
        if c + 2 < nk:
            ss[c + 2] = scores(c + 2)
        if c + 1 < nk:
            p_cur, al_cur = p_next, al_next
    a1, a2 = accs

    lam = (jnp.exp(jnp.sum(lam_ref[0:1, :] * lam_ref[1:2, :], axis=-1, keepdims=True))
           - jnp.exp(jnp.sum(lam_ref[2:3, :] * lam_ref[3:4, :], axis=-1, keepdims=True))
           + LAM_INIT)
    o = (a1[:, :DA_V_DIM] / a1[:, DA_V_DIM:]) - lam * (a2[:, :DA_V_DIM] / a2[:, DA_V_DIM:])
    ms_o = jnp.mean(o * o, axis=-1, keepdims=True)
    o = o * lax.rsqrt(ms_o + EPS) * g_ref[...] * (1.0 - LAM_INIT)
    o_ref[...] = o.astype(BF16)


def _diffattn(qkv, lam_vecs, subln_g, s, da_heads, tq_pref=512, tk_pref=1280):
    rows = qkv.shape[0]
    tq = _pick(s, tq_pref, BF16_SUBLANES)
    tk = _pick(rows, min(tk_pref, rows // 2), LANES)
    kb, vb = da_heads, 2 * da_heads
    return pl.pallas_call(
        functools.partial(_diffattn_values_kernel, tk=tk, rb=2 * BF16_SUBLANES),
        grid=(da_heads, s // tq),
        in_specs=[pl.BlockSpec((4, DA_QK_DIM), lambda h, i: (0, 0)),
                  pl.BlockSpec((tq, LANES), lambda h, i: (i, h)),
                  pl.BlockSpec((rows, LANES), lambda h, i: (0, kb + h)),
                  pl.BlockSpec((rows, LANES), lambda h, i: (0, vb + h)),
                  pl.BlockSpec((1, DA_V_DIM), lambda h, i: (0, 0))],
        out_specs=pl.BlockSpec((tq, DA_V_DIM), lambda h, i: (i, h)),
        out_shape=jax.ShapeDtypeStruct((s, da_heads * DA_V_DIM), BF16),
        compiler_params=_params(("arbitrary", "arbitrary"),
                                2 * rows * LANES * 2 + 2 * tq * LANES * 2,
                                scratch_bytes=4 * tq * tk * 6 + 4 * tq * tk * 4),
        name="diff_attn",
    )(lam_vecs, qkv, qkv, qkv, subln_g.reshape(1, DA_V_DIM))


def _winattn_kernel(sink_ref, q_ref, k_ref, v_ref, o_ref, *, s, c):
    g = pl.program_id(0)
    i = pl.program_id(1)
    tq = q_ref.shape[0]
    band = 3 * WINDOW
    rows = WA_GROUP * WINDOW
    kc = k_ref[s:s + c, :]
    vc = v_ref[s:s + c, :]
    row = lax.broadcasted_iota(jnp.int32, (rows, band), 0)
    col = lax.broadcasted_iota(jnp.int32, (rows, band), 1)
    head = lax.broadcasted_iota(jnp.int32, (rows, 1), 0) // WINDOW
    sink = jnp.zeros((rows, 1), F32)
    for r in range(WA_GROUP):
        sink = jnp.where(head == r, sink_ref[g * WA_GROUP + r] * LOG2_E, sink)
    for b in range(tq // WINDOW):
        q0 = i * tq + b * WINDOW
        start = pl.multiple_of(jnp.clip(q0 - WINDOW, 0, s - band), WINDOW)
        kb = k_ref[pl.ds(start, band), :]
        vb = v_ref[pl.ds(start, band), :]
        valid = jnp.abs(q0 + row % WINDOW - (start + col)) <= WINDOW
        q = jnp.concatenate(
            [q_ref[b * WINDOW:(b + 1) * WINDOW, r * WA_HEAD_DIM:(r + 1) * WA_HEAD_DIM]
             for r in range(WA_GROUP)], axis=0)
        sb = jnp.where(valid, _nt_dot(q, kb), NEG_INF)
        sc = _nt_dot(q, kc)
        m = jnp.maximum(jnp.max(sb, axis=-1, keepdims=True), jnp.max(sc, axis=-1, keepdims=True))
        m = jnp.maximum(m, sink)
        pb = jnp.exp2(sb - m)
        pc = jnp.exp2(sc - m)
        l = (jnp.sum(pb, axis=-1, keepdims=True) + jnp.sum(pc, axis=-1, keepdims=True)
             + jnp.exp2(sink - m))
        o = (jnp.dot(pb.astype(BF16), vb, preferred_element_type=F32)
             + jnp.dot(pc.astype(BF16), vc, preferred_element_type=F32)) * (1.0 / l)
        for r in range(WA_GROUP):
            o_ref[b * WINDOW:(b + 1) * WINDOW, r * WA_HEAD_DIM:(r + 1) * WA_HEAD_DIM] = (
                o[r * WINDOW:(r + 1) * WINDOW].astype(BF16))


def _winattn(qkv, sink, s, c, da_heads, wa_q_heads, tq_pref=512):
    rows = qkv.shape[0]
    kvh = wa_q_heads // WA_GROUP
    tq = _pick(s, tq_pref, WINDOW)
    qb = 3 * da_heads // WA_GROUP
    kb = 3 * da_heads + wa_q_heads
    vb = kb + kvh
    gw = WA_GROUP * WA_HEAD_DIM
    return pl.pallas_call(
        functools.partial(_winattn_kernel, s=s, c=c),
        grid=(kvh, s // tq),
        in_specs=[pl.BlockSpec(memory_space=pltpu.SMEM),
                  pl.BlockSpec((tq, gw), lambda g, i: (i, qb + g)),
                  pl.BlockSpec((rows, LANES), lambda g, i: (0, kb + g)),
                  pl.BlockSpec((rows, LANES), lambda g, i: (0, vb + g))],
        out_specs=pl.BlockSpec((tq, gw), lambda g, i: (i, g)),
        out_shape=jax.ShapeDtypeStruct((s, wa_q_heads * WA_HEAD_DIM), BF16),
        compiler_params=_params(("arbitrary", "arbitrary"),
                                2 * rows * LANES * 2 + 2 * tq * gw * 2,
                                scratch_bytes=8 * tq * (tq + 2 * WINDOW + c) * 4),
        name="win_attn",
    )(sink, qkv, qkv, qkv)


def _merge_kernel(yd_ref, yw_ref, wd_ref, ww_ref, gd_ref, gw_ref, bd_ref, bw_ref, o_ref):
    pd = jnp.dot(yd_ref[...], wd_ref[...], preferred_element_type=F32)
    pw = jnp.dot(yw_ref[...], ww_ref[...], preferred_element_type=F32)
    gd = _sigmoid(gd_ref[...] + bd_ref[...])
    gw = _sigmoid(gw_ref[...] + bw_ref[...])
    o_ref[...] = (gd * pd + gw * pw).astype(BF16)


def _merge(y_da, y_wa, w_o_da, w_o_wa, pr, gate_off, b_gate, tm_pref=512, tn_pref=1024):
    s, kd = y_da.shape
    kw = y_wa.shape[1]
    d = w_o_da.shape[1]
    tm = _pick(s, tm_pref, BF16_SUBLANES)
    tn = _pick(math.gcd(d, gate_off), tn_pref, LANES)
    od, ow = gate_off // tn, (gate_off + d) // tn
    nb = d // tn
    bg = b_gate.reshape(1, N_BRANCH * d)
    return pl.pallas_call(
        _merge_kernel,
        grid=(s // tm, nb),
        in_specs=[pl.BlockSpec((tm, kd), lambda i, j: (i, 0)),
                  pl.BlockSpec((tm, kw), lambda i, j: (i, 0)),
                  pl.BlockSpec((kd, tn), lambda i, j: (0, j)),
                  pl.BlockSpec((kw, tn), lambda i, j: (0, j)),
                  pl.BlockSpec((tm, tn), lambda i, j: (i, od + j)),
                  pl.BlockSpec((tm, tn), lambda i, j: (i, ow + j)),
                  pl.BlockSpec((1, tn), lambda i, j: (0, j)),
                  pl.BlockSpec((1, tn), lambda i, j: (0, nb + j))],
        out_specs=pl.BlockSpec((tm, tn), lambda i, j: (i, j)),
        out_shape=jax.ShapeDtypeStruct((s, d), BF16),
        compiler_params=_params(("arbitrary", "arbitrary"),
                                tm * (kd + kw) * 2 + (kd + kw) * tn * 2 + 2 * tm * tn * 4 + tm * tn * 2,
                                scratch_bytes=4 * tm * tn * 4),
        name="merge",
    )(y_da, y_wa, w_o_da, w_o_wa, pr, pr, bg, bg)


def _mm_res_kernel(a_ref, w_ref, x_ref, g_ref, o_ref, acc_ref):
    kk = pl.program_id(2)

    @pl.when(kk == 0)
    def _():
        acc_ref[...] = jnp.zeros_like(acc_ref)

    acc_ref[...] += jnp.dot(a_ref[...], w_ref[...], preferred_element_type=F32)

    @pl.when(kk == pl.num_programs(2) - 1)
    def _():
        o_ref[...] = x_ref[...] + g_ref[...] * acc_ref[...]


def _matmul_residual(a, w, x, gate, tm_pref, tn_pref, tk_pref, name):
    m, k = a.shape
    n = w.shape[1]
    tm = _pick(m, tm_pref, BF16_SUBLANES)
    tn = _pick(n, tn_pref, LANES)
    tk = _pick(k, tk_pref, LANES)
    return pl.pallas_call(
        _mm_res_kernel,
        grid=(m // tm, n // tn, k // tk),
        in_specs=[pl.BlockSpec((tm, tk), lambda i, j, kk: (i, kk)),
                  pl.BlockSpec((tk, tn), lambda i, j, kk: (kk, j)),
                  pl.BlockSpec((tm, tn), lambda i, j, kk: (i, j)),
                  pl.BlockSpec((1, tn), lambda i, j, kk: (0, j))],
        out_specs=pl.BlockSpec((tm, tn), lambda i, j, kk: (i, j)),
        out_shape=jax.ShapeDtypeStruct((m, n), F32),
        scratch_shapes=[pltpu.VMEM((tm, tn), F32)],
        compiler_params=_params(("arbitrary", "arbitrary", "arbitrary"),
                                tm * tk * 2 + tk * tn * 2 + 2 * tm * tn * 4,
                                scratch_bytes=2 * tm * tn * 4),
        name=name,
    )(a, w, x, gate)


HALO = BF16_SUBLANES


def _ffn_up_kernel(h_ref, hp_ref, hn_ref, wa_ref, wu_ref, cw_ref, cb_ref, o_ref, lhs_ref):
    i = pl.program_id(0)
    j = pl.program_id(1)
    tm = h_ref.shape[0]

    @pl.when(j == 0)
    def _():
        prev = jnp.where(i > 0, hp_ref[...], jnp.zeros_like(hp_ref))
        nxt = jnp.where(i < pl.num_programs(0) - 1, hn_ref[...], jnp.zeros_like(hn_ref))
        lhs_ref[0:HALO, :] = prev
        lhs_ref[HALO:HALO + tm, :] = h_ref[...]
        lhs_ref[HALO + tm:, :] = nxt

    a = jnp.dot(lhs_ref[...], wa_ref[...], preferred_element_type=F32)
    u = jnp.dot(lhs_ref[HALO:HALO + tm, :], wu_ref[...], preferred_element_type=F32)
    rows = tm + 2 * HALO
    a_prev = pltpu.roll(a, 1, 0)[HALO:HALO + tm, :]
    a_next = pltpu.roll(a, rows - 1, 0)[HALO:HALO + tm, :]
    a_mid = a[HALO:HALO + tm, :]
    conv = cb_ref[...] + a_prev * cw_ref[0:1, :] + a_mid * cw_ref[1:2, :] + a_next * cw_ref[2:3, :]
    o_ref[...] = (conv * _sigmoid(conv) * u).astype(BF16)


def _ffn_up(h, w_gate, w_lin, conv_w, conv_b, tm_pref=1024, tn_pref=512):
    s, d = h.shape
    ffp = w_gate.shape[1]
    tm = _pick(s, tm_pref, HALO)
    tn = _pick(ffp, tn_pref, LANES)
    nf = ffp // tn
    per = tm // HALO
    last = s // HALO - 1
    return pl.pallas_call(
        _ffn_up_kernel,
        grid=(s // tm, nf),
        in_specs=[pl.BlockSpec((tm, d), lambda i, j: (i, 0)),
                  pl.BlockSpec((HALO, d), lambda i, j: (jnp.maximum(i * per - 1, 0), 0)),
                  pl.BlockSpec((HALO, d), lambda i, j: (jnp.minimum((i + 1) * per, last), 0)),
                  pl.BlockSpec((d, tn), lambda i, j: (0, j)),
                  pl.BlockSpec((d, tn), lambda i, j: (0, j)),
                  pl.BlockSpec((CONV_W, tn), lambda i, j: (0, j)),
                  pl.BlockSpec((1, tn), lambda i, j: (0, j))],
        out_specs=pl.BlockSpec((tm, tn), lambda i, j: (i, j)),
        out_shape=jax.ShapeDtypeStruct((s, ffp), BF16),
        scratch_shapes=[pltpu.VMEM((tm + 2 * HALO, d), BF16)],
        compiler_params=_params(("arbitrary", "arbitrary"),
                                (tm + 2 * HALO) * d * 2 + 2 * d * tn * 2 + tm * tn * 2,
                                scratch_bytes=(tm + 2 * HALO) * d * 2 + 6 * (tm + 2 * HALO) * tn * 4),
        name="ffn_up",
    )(h, h, h, w_gate, w_lin, conv_w, conv_b)


def _split_cast_kernel(wg_ref, wl_ref, og_ref, ol_ref, *, n_valid):
    j = pl.program_id(0)

    @pl.when(j < n_valid)
    def _():
        og_ref[...] = wg_ref[...].astype(BF16)
        ol_ref[...] = wl_ref[...].astype(BF16)

    @pl.when(j >= n_valid)
    def _():
        og_ref[...] = jnp.zeros_like(og_ref)
        ol_ref[...] = jnp.zeros_like(ol_ref)


def _split_cast_pad(w, d_ff, ffp):
    d = w.shape[0]
    assert d_ff % LANES == 0 and ffp % LANES == 0
    tw = _pick(math.gcd(d_ff, ffp), 256, LANES)
    n_valid, n_tiles = d_ff // tw, ffp // tw
    out = jax.ShapeDtypeStruct((d, ffp), BF16)
    return pl.pallas_call(
        functools.partial(_split_cast_kernel, n_valid=n_valid),
        grid=(n_tiles,),
        in_specs=[pl.BlockSpec((d, tw), lambda j: (0, jnp.minimum(j, n_valid - 1))),
                  pl.BlockSpec((d, tw), lambda j: (0, n_valid + jnp.minimum(j, n_valid - 1)))],
        out_specs=[pl.BlockSpec((d, tw), lambda j: (0, j)), pl.BlockSpec((d, tw), lambda j: (0, j))],
        out_shape=[out, out],
        compiler_params=_params(("arbitrary",), 2 * d * tw * 4 + 2 * d * tw * 2),
        name="w_up_cast",
    )(w, w)


def _cast_pad_rows_kernel(w_ref, o_ref, *, n_valid):
    i = pl.program_id(0)

    @pl.when(i < n_valid)
    def _():
        o_ref[...] = w_ref[...].astype(BF16)

    @pl.when(i >= n_valid)
    def _():
        o_ref[...] = jnp.zeros_like(o_ref)


def _cast_pad_rows(w, rows_p):
    k, n = w.shape
    assert k % LANES == 0 and rows_p % LANES == 0
    tr = _pick(math.gcd(k, rows_p), 256, LANES)
    n_valid = k // tr
    return pl.pallas_call(
        functools.partial(_cast_pad_rows_kernel, n_valid=n_valid),
        grid=(rows_p // tr,),
        in_specs=[pl.BlockSpec((tr, n), lambda i: (jnp.minimum(i, n_valid - 1), 0))],
        out_specs=pl.BlockSpec((tr, n), lambda i: (i, 0)),
        out_shape=jax.ShapeDtypeStruct((rows_p, n), BF16),
        compiler_params=_params(("arbitrary",), tr * n * 6),
        name="w_down_cast",
    )(w)


def kernel(x, c, ctx, c_ctx, w_ada, b_ada, attn_norm_g, w_in, b_gate, da_qn_g, da_kn_g, da_lambda_q1, da_lambda_k1, da_lambda_q2, da_lambda_k2, da_subln_g, wa_qn_g, wa_kn_g, wa_sink, w_o_da, w_o_wa, w_out, ffn_norm_g, w_ffn_up, ffn_conv_w, ffn_conv_b, w_ffn_down):
    b, s, d = x.shape
    cl = ctx.shape[1]
    assert b == 1 and w_in.shape[0] == 1, "one batch element, one layer"
    da_heads = w_o_da.shape[1] // DA_V_DIM
    wa_q_heads = w_o_wa.shape[1] // WA_HEAD_DIM
    wa_kv_heads = wa_q_heads // WA_GROUP
    d_ff = ffn_conv_b.shape[1]
    gate_off = (3 * da_heads + wa_q_heads + 2 * wa_kv_heads) * LANES
    assert w_in.shape[2] == gate_off + N_BRANCH * d

    x2 = x.reshape(s, d)
    ctx2 = ctx.reshape(cl, d)

    ffp = -(-d_ff // 1024) * 1024 if d_ff > 1024 else d_ff
    padf = ffp - d_ff
    w_in_b = w_in[0].astype(BF16)
    w_o_da_b = w_o_da[0].astype(BF16)
    w_o_wa_b = w_o_wa[0].astype(BF16)
    w_out_b = w_out[0].astype(BF16)
    w_gate_b, w_lin_b = _split_cast_pad(w_ffn_up[0], d_ff, ffp)
    w_down_b = _cast_pad_rows(w_ffn_down[0], ffp)
    conv_w = jnp.pad(ffn_conv_w[0], ((0, 0), (0, padf)))
    conv_b = jnp.pad(ffn_conv_b[0], (0, padf)).reshape(1, ffp)

    cc = jnp.concatenate([c, c_ctx[None, :], jnp.zeros((6, d), F32)], axis=0)
    mod = _adaln(cc, w_ada[0], b_ada[0])
    sh1, sc1, g1 = mod[0:1, 0:d], mod[0:1, d:2 * d], mod[0:1, 2 * d:3 * d]
    sh2, sc2, g2 = mod[0:1, 3 * d:4 * d], mod[0:1, 4 * d:5 * d], mod[0:1, 5 * d:6 * d]
    mod_x1 = jnp.concatenate([sh1, sc1], axis=0)
    mod_c1 = jnp.concatenate([mod[1:2, 0:d], mod[1:2, d:2 * d]], axis=0)
    mod_x2 = jnp.concatenate([sh2, sc2], axis=0)

    h = _normmod(x2, attn_norm_g[0], mod_x1, ctx2, mod_c1)
    pr = _matmul(h, w_in_b, 1280, 512)
    tabs = _rope_tables(s, cl)
    gains = jnp.stack([jnp.tile(da_qn_g[0] * (DA_SCALE * LOG2_E), 2), jnp.tile(da_kn_g[0], 2),
                       wa_qn_g[0] * (WA_SCALE * LOG2_E), wa_kn_g[0]], axis=0)
    qkv = _headprep(pr, tabs, gains, da_heads, wa_q_heads)
    lam_vecs = jnp.stack([da_lambda_q1[0], da_lambda_k1[0], da_lambda_q2[0], da_lambda_k2[0]], axis=0)
    y_da = _diffattn(qkv, lam_vecs, da_subln_g[0], s, da_heads)
    y_wa = _winattn(qkv, wa_sink[0], s, cl, da_heads, wa_q_heads)
    u = _merge(y_da, y_wa, w_o_da_b, w_o_wa_b, pr, gate_off, b_gate[0])
    x1 = _matmul_residual(u, w_out_b, x2, g1, 1024, 1024, d, "out_proj")

    h2 = _normmod(x1, ffn_norm_g[0], mod_x2)
    act = _ffn_up(h2, w_gate_b, w_lin_b, conv_w, conv_b)
    out = _matmul_residual(act, w_down_b, x1, g2, 1024, 1024, 2816, "ffn_down")
    return out.reshape(b, s, d)
```

```python
import functools
import math

import jax
import jax.numpy as jnp
from jax import lax
from jax.experimental import pallas as pl
from jax.experimental.pallas import tpu as pltpu

F32 = jnp.float32
BF16 = jnp.bfloat16

GRID_W = 64
DA_QK_DIM = 64
DA_V_DIM = 2 * DA_QK_DIM
WA_HEAD_DIM = 128
WA_GROUP = 4
WINDOW = 128
N_BRANCH = 2
CONV_W = 3
ROPE_BASE = 10000.0
EPS = 1e-6
DA_SCALE = DA_QK_DIM ** -0.5
WA_SCALE = WA_HEAD_DIM ** -0.5
NEG_INF = -1e30
LAM_INIT = 0.8 - 0.6 * math.exp(-0.3 * 0)

LANES = 128
BF16_SUBLANES = 16
V7X_VMEM_CAP_BYTES = 58 * 1024 * 1024
VMEM_SLACK_BYTES = 6 * 1024 * 1024


def _pick(dim, pref, unit):
    best = None
    t = unit
    while t <= min(dim, pref):
        if dim % t == 0:
            best = t
        t += unit
    if best is None:
        raise ValueError(f"no tile for dim={dim} unit={unit}")
    return best


def _params(semantics, block_bytes, scratch_bytes=0):
    est = 2 * block_bytes + scratch_bytes + VMEM_SLACK_BYTES
    return pltpu.CompilerParams(
        dimension_semantics=semantics,
        vmem_limit_bytes=min(max(est, 16 * 1024 * 1024), V7X_VMEM_CAP_BYTES))


def _sigmoid(x):
    return 1.0 / (1.0 + jnp.exp(-x))


def _adaln_kernel(c_ref, w_ref, b_ref, o_ref):
    a = c_ref[...]
    a = a * _sigmoid(a)
    o_ref[...] = jnp.dot(a.astype(BF16), w_ref[...].astype(BF16),
                         preferred_element_type=F32) + b_ref[...]


def _adaln(cc, w, b):
    rows, d = cc.shape
    n = w.shape[1]
    tn = _pick(n, 512, LANES)
    return pl.pallas_call(
        _adaln_kernel,
        grid=(n // tn,),
        in_specs=[pl.BlockSpec((rows, d), lambda j: (0, 0)),
                  pl.BlockSpec((d, tn), lambda j: (0, j)),
                  pl.BlockSpec((1, tn), lambda j: (0, j))],
        out_specs=pl.BlockSpec((rows, tn), lambda j: (0, j)),
        out_shape=jax.ShapeDtypeStruct((rows, n), F32),
        compiler_params=_params(("arbitrary",), d * tn * 4 + d * tn * 2),
        name="adaln",
    )(cc, w, b.reshape(1, n))


def _normmod_rows(x, g, mod):
    ms = jnp.mean(x * x, axis=-1, keepdims=True)
    y = x * lax.rsqrt(ms + EPS) * g
    return (y * (1.0 + mod[1:2, :]) + mod[0:1, :]).astype(BF16)


def _normmod2_kernel(x_ref, c_ref, g_ref, mx_ref, mc_ref, o_ref, *, nx):
    i = pl.program_id(0)

    @pl.when(i < nx)
    def _():
        o_ref[...] = _normmod_rows(x_ref[...], g_ref[...], mx_ref[...])

    @pl.when(i >= nx)
    def _():
        o_ref[...] = _normmod_rows(c_ref[...], g_ref[...], mc_ref[...])


def _normmod1_kernel(x_ref, g_ref, mx_ref, o_ref):
    o_ref[...] = _normmod_rows(x_ref[...], g_ref[...], mx_ref[...])


def _normmod(x, g, mod_x, ctx=None, mod_c=None):
    s, d = x.shape
    g = g.reshape(1, d)
    if ctx is None:
        tm = _pick(s, 256, BF16_SUBLANES)
        return pl.pallas_call(
            _normmod1_kernel,
            grid=(s // tm,),
            in_specs=[pl.BlockSpec((tm, d), lambda i: (i, 0)),
                      pl.BlockSpec((1, d), lambda i: (0, 0)),
                      pl.BlockSpec((2, d), lambda i: (0, 0))],
            out_specs=pl.BlockSpec((tm, d), lambda i: (i, 0)),
            out_shape=jax.ShapeDtypeStruct((s, d), BF16),
            compiler_params=_params(("arbitrary",), tm * d * 6),
            name="normmod",
        )(x, g, mod_x)
    c = ctx.shape[0]
    tm = _pick(math.gcd(s, c), 256, BF16_SUBLANES)
    nx, nc = s // tm, c // tm
    return pl.pallas_call(
        functools.partial(_normmod2_kernel, nx=nx),
        grid=(nx + nc,),
        in_specs=[pl.BlockSpec((tm, d), lambda i: (jnp.minimum(i, nx - 1), 0)),
                  pl.BlockSpec((tm, d), lambda i: (jnp.maximum(i - nx, 0), 0)),
                  pl.BlockSpec((1, d), lambda i: (0, 0)),
                  pl.BlockSpec((2, d), lambda i: (0, 0)),
                  pl.BlockSpec((2, d), lambda i: (0, 0))],
        out_specs=pl.BlockSpec((tm, d), lambda i: (i, 0)),
        out_shape=jax.ShapeDtypeStruct((s + c, d), BF16),
        compiler_params=_params(("arbitrary",), tm * d * 10),
        name="normmod_xc",
    )(x, ctx, g, mod_x, mod_c)


def _mm_kernel(a_ref, w_ref, o_ref):
    o_ref[...] = jnp.dot(a_ref[...], w_ref[...], preferred_element_type=F32)


def _matmul(a, w, tm_pref, tn_pref):
    m, k = a.shape
    n = w.shape[1]
    tm = _pick(m, tm_pref, BF16_SUBLANES)
    tn = _pick(n, tn_pref, LANES)
    return pl.pallas_call(
        _mm_kernel,
        grid=(m // tm, n // tn),
        in_specs=[pl.BlockSpec((tm, k), lambda i, j: (i, 0)),
                  pl.BlockSpec((k, tn), lambda i, j: (0, j))],
        out_specs=pl.BlockSpec((tm, tn), lambda i, j: (i, j)),
        out_shape=jax.ShapeDtypeStruct((m, n), F32),
        compiler_params=_params(("arbitrary", "arbitrary"),
                                tm * k * 2 + k * tn * 2 + tm * tn * 4, scratch_bytes=tm * tn * 4),
        name="in_proj",
    )(a, w)


def _rope_tables(s, c):
    pos = jnp.arange(s, dtype=jnp.int32)
    r = (pos // GRID_W).astype(F32)[:, None]
    col = (pos % GRID_W).astype(F32)[:, None]
    lane = jnp.arange(LANES, dtype=jnp.int32)

    def table(head_dim):
        axis_dim = head_dim // 2
        pair = axis_dim // 2
        within = lane % head_dim
        f = (within % pair).astype(F32)
        freq = ROPE_BASE ** (-(2.0 * f) / axis_dim)
        use_row = (within // axis_dim) == 0
        ang = jnp.where(use_row[None, :], r * freq[None, :], col * freq[None, :])
        first = (within % axis_dim) < pair
        cos = jnp.cos(ang)
        sin = jnp.where(first[None, :], -jnp.sin(ang), jnp.sin(ang))
        cos = jnp.concatenate([cos, jnp.ones((c, LANES), F32)], axis=0)
        sin = jnp.concatenate([sin, jnp.zeros((c, LANES), F32)], axis=0)
        return cos, sin

    cd, sd = table(DA_QK_DIM)
    cw, sw = table(WA_HEAD_DIM)
    return jnp.stack([cd, sd, cw, sw], axis=0)


def _headprep_kernel(pr_ref, tab_ref, gain_ref, ones_da_ref, ones_wa_ref, o_ref,
                     *, n_qk_da, n_v_da, n_q_wa, n_k_wa, n_v_wa):
    tm = pr_ref.shape[0]
    lane = lax.broadcasted_iota(jnp.int32, (tm, LANES), 1)
    first_da = (lane % (DA_QK_DIM // 2)) < (DA_QK_DIM // 4)
    first_wa = (lane % (WA_HEAD_DIM // 2)) < (WA_HEAD_DIM // 4)
    cos_da, sin_da = tab_ref[0], tab_ref[1]
    cos_wa, sin_wa = tab_ref[2], tab_ref[3]

    def rope(y, cos, sin, first, shift):
        partner = jnp.where(first, pltpu.roll(y, LANES - shift, 1), pltpu.roll(y, shift, 1))
        return y * cos + partner * sin

    def head_sum_sq(x, ones_ref):
        x2 = x * x
        hi = x2.astype(BF16)
        mid = (x2 - hi.astype(F32)).astype(BF16)
        return jnp.dot(jnp.concatenate([hi, mid], axis=1), ones_ref[...], preferred_element_type=F32)

    def da_group(gidx, gain):
        x = pr_ref[:, gidx * LANES:(gidx + 1) * LANES]
        ms = head_sum_sq(x, ones_da_ref) * (1.0 / DA_QK_DIM)
        y = x * lax.rsqrt(ms + EPS) * gain
        y = rope(y, cos_da, sin_da, first_da, DA_QK_DIM // 4)
        o_ref[:, gidx * LANES:(gidx + 1) * LANES] = y.astype(BF16)

    def wa_group(gidx, gain):
        x = pr_ref[:, gidx * LANES:(gidx + 1) * LANES]
        ms = head_sum_sq(x, ones_wa_ref) * (1.0 / WA_HEAD_DIM)
        y = x * lax.rsqrt(ms + EPS) * gain
        y = rope(y, cos_wa, sin_wa, first_wa, WA_HEAD_DIM // 4)
        o_ref[:, gidx * LANES:(gidx + 1) * LANES] = y.astype(BF16)

    def copy_group(gidx):
        o_ref[:, gidx * LANES:(gidx + 1) * LANES] = pr_ref[:, gidx * LANES:(gidx + 1) * LANES].astype(BF16)

    g = 0
    for _ in range(n_qk_da):
        da_group(g, gain_ref[0:1, :]); g += 1
    for _ in range(n_qk_da):
        da_group(g, gain_ref[1:2, :]); g += 1
    for _ in range(n_v_da):
        copy_group(g); g += 1
    for _ in range(n_q_wa):
        wa_group(g, gain_ref[2:3, :]); g += 1
    for _ in range(n_k_wa):
        wa_group(g, gain_ref[3:4, :]); g += 1
    for _ in range(n_v_wa):
        copy_group(g); g += 1


def _headprep(pr, tabs, gains, da_heads, wa_q_heads, tm_pref=256):
    rows = pr.shape[0]
    wa_kv_heads = wa_q_heads // WA_GROUP
    groups = dict(n_qk_da=da_heads, n_v_da=da_heads, n_q_wa=wa_q_heads,
                  n_k_wa=wa_kv_heads, n_v_wa=wa_kv_heads)
    width = (3 * da_heads + wa_q_heads + 2 * wa_kv_heads) * LANES
    tm = _pick(rows, tm_pref, BF16_SUBLANES)
    lane = jnp.arange(LANES)
    same_da_head = (lane[:, None] // DA_QK_DIM) == (lane[None, :] // DA_QK_DIM)
    ones_da = jnp.tile(same_da_head.astype(BF16), (2, 1))
    ones_wa = jnp.ones((2 * LANES, LANES), BF16)
    return pl.pallas_call(
        functools.partial(_headprep_kernel, **groups),
        grid=(rows // tm,),
        in_specs=[pl.BlockSpec((tm, width), lambda i: (i, 0)),
                  pl.BlockSpec((4, tm, LANES), lambda i: (0, i, 0)),
                  pl.BlockSpec((4, LANES), lambda i: (0, 0)),
                  pl.BlockSpec((2 * LANES, LANES), lambda i: (0, 0)),
                  pl.BlockSpec((2 * LANES, LANES), lambda i: (0, 0))],
        out_specs=pl.BlockSpec((tm, width), lambda i: (i, 0)),
        out_shape=jax.ShapeDtypeStruct((rows, width), BF16),
        compiler_params=_params(("arbitrary",), tm * width * 6 + 4 * tm * LANES * 4),
        name="headprep",
    )(pr, tabs, gains, ones_da, ones_wa)


def _nt_dot(a, b):
    return lax.dot_general(a, b, (((1,), (1,)), ((), ())), preferred_element_type=F32)


LOG2_E = math.log2(math.e)


def _diffattn_kernel(lam_ref, q_ref, k_ref, v_ref, g_ref, o_ref, *, tk, rb):
    tq = q_ref.shape[0]
    nk = k_ref.shape[0] // tk
    q = q_ref[...]
    lane = lax.broadcasted_iota(jnp.int32, (tq, LANES), 1)
    zero = jnp.zeros_like(q)
    qs = (jnp.where(lane < DA_QK_DIM, q, zero), jnp.where(lane < DA_QK_DIM, zero, q))
    ones = jnp.ones((tk, LANES), BF16)

    def scores(c):
        k = k_ref[c * tk:(c + 1) * tk, :]
        return [_nt_dot(qs[h], k).astype(BF16) for h in range(2)]

    def softmax(ss, ms):
        ps, new_ms, alphas = [], [], []
        for h in range(2):
            p_blocks, m_blocks, a_blocks = [], [], []
            for r in range(0, tq, rb):
                s = ss[h][r:r + rb]
                m_old = ms[h][r:r + rb]
                m_new = jnp.maximum(m_old, jnp.max(s, axis=-1, keepdims=True).astype(F32))
                p_blocks.append(jnp.exp2(s - m_new.astype(BF16)))
                m_blocks.append(m_new)
                a_blocks.append(jnp.exp2(m_old - m_new))
            ps.append(jnp.concatenate(p_blocks, axis=0))
            new_ms.append(jnp.concatenate(m_blocks, axis=0))
            alphas.append(jnp.concatenate(a_blocks, axis=0))
        return ps, new_ms, alphas

    def accumulate(c, ps, alphas, accs):
        v1 = jnp.concatenate([v_ref[c * tk:(c + 1) * tk, :], ones], axis=1)
        return [alphas[h] * accs[h] + jnp.dot(ps[h], v1, preferred_element_type=F32)
                for h in range(2)]

    m0 = jnp.full((tq, 1), NEG_INF, F32)
    a0 = jnp.zeros((tq, 2 * DA_V_DIM), F32)
    ss = {0: scores(0), 1: scores(1)}
    p_cur, ms, al_cur = softmax(ss.pop(0), [m0, m0])
    accs = [a0, a0]
    for c in range(nk):
        if c + 1 < nk:
            p_next, ms, al_next = softmax(ss.pop(c + 1), ms)
        accs = accumulate(c, p_cur, al_cur, accs)
        if c + 2 < nk:
            ss[c + 2] = scores(c + 2)
        if c + 1 < nk:
            p_cur, al_cur = p_next, al_next
    a1, a2 = accs

    lam = (jnp.exp(jnp.sum(lam_ref[0:1, :] * lam_ref[1:2, :], axis=-1, keepdims=True))
           - jnp.exp(jnp.sum(lam_ref[2:3, :] * lam_ref[3:4, :], axis=-1, keepdims=True))
           + LAM_INIT)
    o = (a1[:, :DA_V_DIM] / a1[:, DA_V_DIM:]) - lam * (a2[:, :DA_V_DIM] / a2[:, DA_V_DIM:])
    ms_o = jnp.mean(o * o, axis=-1, keepdims=True)
    o = o * lax.rsqrt(ms_o + EPS) * g_ref[...] * (1.0 - LAM_INIT)
    o_ref[...] = o.astype(BF16)


def _diffattn(qkv, lam_vecs, subln_g, s, da_heads, tq_pref=512, tk_pref=1280):
    rows = qkv.shape[0]
    tq = _pick(s, tq_pref, BF16_SUBLANES)
    tk = _pick(rows, min(tk_pref, rows // 2), LANES)
    kb, vb = da_heads, 2 * da_heads
    return pl.pallas_call(
        functools.partial(_diffattn_kernel, tk=tk, rb=2 * BF16_SUBLANES),
        grid=(da_heads, s // tq),
        in_specs=[pl.BlockSpec((4, DA_QK_DIM), lambda h, i: (0, 0)),
                  pl.BlockSpec((tq, LANES), lambda h, i: (i, h)),
                  pl.BlockSpec((rows, LANES), lambda h, i: (0, kb + h)),
                  pl.BlockSpec((rows, LANES), lambda h, i: (0, vb + h)),
                  pl.BlockSpec((1, DA_V_DIM), lambda h, i: (0, 0))],
        out_specs=pl.BlockSpec((tq, DA_V_DIM), lambda h, i: (i, h)),
        out_shape=jax.ShapeDtypeStruct((s, da_heads * DA_V_DIM), BF16),
        compiler_params=_params(("arbitrary", "arbitrary"),
                                2 * rows * LANES * 2 + 2 * tq * LANES * 2,
                                scratch_bytes=4 * tq * tk * 6 + 4 * tq * tk * 4),
        name="diff_attn",
    )(lam_vecs, qkv, qkv, qkv, subln_g.reshape(1, DA_V_DIM))


def _winattn_kernel(sink_ref, q_ref, k_ref, v_ref, o_ref, *, s, c):
    g = pl.program_id(0)
    i = pl.program_id(1)
    tq = q_ref.shape[0]
    band = 3 * WINDOW
    rows = WA_GROUP * WINDOW
    kc = k_ref[s:s + c, :]
    vc = v_ref[s:s + c, :]
    row = lax.broadcasted_iota(jnp.int32, (rows, band), 0)
    col = lax.broadcasted_iota(jnp.int32, (rows, band), 1)
    head = lax.broadcasted_iota(jnp.int32, (rows, 1), 0) // WINDOW
    sink = jnp.zeros((rows, 1), F32)
    for r in range(WA_GROUP):
        sink = jnp.where(head == r, sink_ref[g * WA_GROUP + r] * LOG2_E, sink)
    for b in range(tq // WINDOW):
        q0 = i * tq + b * WINDOW
        start = pl.multiple_of(jnp.clip(q0 - WINDOW, 0, s - band), WINDOW)
        kb = k_ref[pl.ds(start, band), :]
        vb = v_ref[pl.ds(start, band), :]
        valid = jnp.abs(q0 + row % WINDOW - (start + col)) <= WINDOW
        q = jnp.concatenate(
            [q_ref[b * WINDOW:(b + 1) * WINDOW, r * WA_HEAD_DIM:(r + 1) * WA_HEAD_DIM]
             for r in range(WA_GROUP)], axis=0)
        sb = jnp.where(valid, _nt_dot(q, kb), NEG_INF)
        sc = _nt_dot(q, kc)
        m = jnp.maximum(jnp.max(sb, axis=-1, keepdims=True), jnp.max(sc, axis=-1, keepdims=True))
        m = jnp.maximum(m, sink)
        pb = jnp.exp2(sb - m)
        pc = jnp.exp2(sc - m)
        l = (jnp.sum(pb, axis=-1, keepdims=True) + jnp.sum(pc, axis=-1, keepdims=True)
             + jnp.exp2(sink - m))
        o = (jnp.dot(pb.astype(BF16), vb, preferred_element_type=F32)
             + jnp.dot(pc.astype(BF16), vc, preferred_element_type=F32)) * (1.0 / l)
        for r in range(WA_GROUP):
            o_ref[b * WINDOW:(b + 1) * WINDOW, r * WA_HEAD_DIM:(r + 1) * WA_HEAD_DIM] = (
                o[r * WINDOW:(r + 1) * WINDOW].astype(BF16))


def _winattn(qkv, sink, s, c, da_heads, wa_q_heads, tq_pref=512):
    rows = qkv.shape[0]
    kvh = wa_q_heads // WA_GROUP
    tq = _pick(s, tq_pref, WINDOW)
    qb = 3 * da_heads // WA_GROUP
    kb = 3 * da_heads + wa_q_heads
    vb = kb + kvh
    gw = WA_GROUP * WA_HEAD_DIM
    return pl.pallas_call(
        functools.partial(_winattn_kernel, s=s, c=c),
        grid=(kvh, s // tq),
        in_specs=[pl.BlockSpec(memory_space=pltpu.SMEM),
                  pl.BlockSpec((tq, gw), lambda g, i: (i, qb + g)),
                  pl.BlockSpec((rows, LANES), lambda g, i: (0, kb + g)),
                  pl.BlockSpec((rows, LANES), lambda g, i: (0, vb + g))],
        out_specs=pl.BlockSpec((tq, gw), lambda g, i: (i, g)),
        out_shape=jax.ShapeDtypeStruct((s, wa_q_heads * WA_HEAD_DIM), BF16),
        compiler_params=_params(("arbitrary", "arbitrary"),
                                2 * rows * LANES * 2 + 2 * tq * gw * 2,
                                scratch_bytes=8 * tq * (tq + 2 * WINDOW + c) * 4),
        name="win_attn",
    )(sink, qkv, qkv, qkv)


def _merge_kernel(yd_ref, yw_ref, wd_ref, ww_ref, gd_ref, gw_ref, bd_ref, bw_ref, o_ref):
    pd = jnp.dot(yd_ref[...], wd_ref[...], preferred_element_type=F32)
    pw = jnp.dot(yw_ref[...], ww_ref[...], preferred_element_type=F32)
    gd = _sigmoid(gd_ref[...] + bd_ref[...])
    gw = _sigmoid(gw_ref[...] + bw_ref[...])
    o_ref[...] = (gd * pd + gw * pw).astype(BF16)


def _merge(y_da, y_wa, w_o_da, w_o_wa, pr, gate_off, b_gate, tm_pref=512, tn_pref=1024):
    s, kd = y_da.shape
    kw = y_wa.shape[1]
    d = w_o_da.shape[1]
    tm = _pick(s, tm_pref, BF16_SUBLANES)
    tn = _pick(math.gcd(d, gate_off), tn_pref, LANES)
    od, ow = gate_off // tn, (gate_off + d) // tn
    nb = d // tn
    bg = b_gate.reshape(1, N_BRANCH * d)
    return pl.pallas_call(
        _merge_kernel,
        grid=(s // tm, nb),
        in_specs=[pl.BlockSpec((tm, kd), lambda i, j: (i, 0)),
                  pl.BlockSpec((tm, kw), lambda i, j: (i, 0)),
                  pl.BlockSpec((kd, tn), lambda i, j: (0, j)),
                  pl.BlockSpec((kw, tn), lambda i, j: (0, j)),
                  pl.BlockSpec((tm, tn), lambda i, j: (i, od + j)),
                  pl.BlockSpec((tm, tn), lambda i, j: (i, ow + j)),
                  pl.BlockSpec((1, tn), lambda i, j: (0, j)),
                  pl.BlockSpec((1, tn), lambda i, j: (0, nb + j))],
        out_specs=pl.BlockSpec((tm, tn), lambda i, j: (i, j)),
        out_shape=jax.ShapeDtypeStruct((s, d), BF16),
        compiler_params=_params(("arbitrary", "arbitrary"),
                                tm * (kd + kw) * 2 + (kd + kw) * tn * 2 + 2 * tm * tn * 4 + tm * tn * 2,
                                scratch_bytes=4 * tm * tn * 4),
        name="merge",
    )(y_da, y_wa, w_o_da, w_o_wa, pr, pr, bg, bg)


def _mm_res_kernel(a_ref, w_ref, x_ref, g_ref, o_ref, acc_ref):
    kk = pl.program_id(2)

    @pl.when(kk == 0)
    def _():
        acc_ref[...] = jnp.zeros_like(acc_ref)

    acc_ref[...] += jnp.dot(a_ref[...], w_ref[...], preferred_element_type=F32)

    @pl.when(kk == pl.num_programs(2) - 1)
    def _():
        o_ref[...] = x_ref[...] + g_ref[...] * acc_ref[...]


def _matmul_residual(a, w, x, gate, tm_pref, tn_pref, tk_pref, name):
    m, k = a.shape
    n = w.shape[1]
    tm = _pick(m, tm_pref, BF16_SUBLANES)
    tn = _pick(n, tn_pref, LANES)
    tk = _pick(k, tk_pref, LANES)
    return pl.pallas_call(
        _mm_res_kernel,
        grid=(m // tm, n // tn, k // tk),
        in_specs=[pl.BlockSpec((tm, tk), lambda i, j, kk: (i, kk)),
                  pl.BlockSpec((tk, tn), lambda i, j, kk: (kk, j)),
                  pl.BlockSpec((tm, tn), lambda i, j, kk: (i, j)),
                  pl.BlockSpec((1, tn), lambda i, j, kk: (0, j))],
        out_specs=pl.BlockSpec((tm, tn), lambda i, j, kk: (i, j)),
        out_shape=jax.ShapeDtypeStruct((m, n), F32),
        scratch_shapes=[pltpu.VMEM((tm, tn), F32)],
        compiler_params=_params(("arbitrary", "arbitrary", "arbitrary"),
                                tm * tk * 2 + tk * tn * 2 + 2 * tm * tn * 4,
                                scratch_bytes=2 * tm * tn * 4),
        name=name,
    )(a, w, x, gate)


HALO = BF16_SUBLANES


def _ffn_up_kernel(h_ref, hp_ref, hn_ref, wa_ref, wu_ref, cw_ref, cb_ref, o_ref, lhs_ref):
    i = pl.program_id(0)
    j = pl.program_id(1)
    tm = h_ref.shape[0]

    @pl.when(j == 0)
    def _():
        prev = jnp.where(i > 0, hp_ref[...], jnp.zeros_like(hp_ref))
        nxt = jnp.where(i < pl.num_programs(0) - 1, hn_ref[...], jnp.zeros_like(hn_ref))
        lhs_ref[0:HALO, :] = prev
        lhs_ref[HALO:HALO + tm, :] = h_ref[...]
        lhs_ref[HALO + tm:, :] = nxt

    a = jnp.dot(lhs_ref[...], wa_ref[...], preferred_element_type=F32)
    u = jnp.dot(lhs_ref[HALO:HALO + tm, :], wu_ref[...], preferred_element_type=F32)
    rows = tm + 2 * HALO
    a_prev = pltpu.roll(a, 1, 0)[HALO:HALO + tm, :]
    a_next = pltpu.roll(a, rows - 1, 0)[HALO:HALO + tm, :]
    a_mid = a[HALO:HALO + tm, :]
    conv = cb_ref[...] + a_prev * cw_ref[0:1, :] + a_mid * cw_ref[1:2, :] + a_next * cw_ref[2:3, :]
    o_ref[...] = (conv * _sigmoid(conv) * u).astype(BF16)


def _ffn_up(h, w_gate, w_lin, conv_w, conv_b, tm_pref=1024, tn_pref=512):
    s, d = h.shape
    ffp = w_gate.shape[1]
    tm = _pick(s, tm_pref, HALO)
    tn = _pick(ffp, tn_pref, LANES)
    nf = ffp // tn
    per = tm // HALO
    last = s // HALO - 1
    return pl.pallas_call(
        _ffn_up_kernel,
        grid=(s // tm, nf),
        in_specs=[pl.BlockSpec((tm, d), lambda i, j: (i, 0)),
                  pl.BlockSpec((HALO, d), lambda i, j: (jnp.maximum(i * per - 1, 0), 0)),
                  pl.BlockSpec((HALO, d), lambda i, j: (jnp.minimum((i + 1) * per, last), 0)),
                  pl.BlockSpec((d, tn), lambda i, j: (0, j)),
                  pl.BlockSpec((d, tn), lambda i, j: (0, j)),
                  pl.BlockSpec((CONV_W, tn), lambda i, j: (0, j)),
                  pl.BlockSpec((1, tn), lambda i, j: (0, j))],
        out_specs=pl.BlockSpec((tm, tn), lambda i, j: (i, j)),
        out_shape=jax.ShapeDtypeStruct((s, ffp), BF16),
        scratch_shapes=[pltpu.VMEM((tm + 2 * HALO, d), BF16)],
        compiler_params=_params(("arbitrary", "arbitrary"),
                                (tm + 2 * HALO) * d * 2 + 2 * d * tn * 2 + tm * tn * 2,
                                scratch_bytes=(tm + 2 * HALO) * d * 2 + 6 * (tm + 2 * HALO) * tn * 4),
        name="ffn_up",
    )(h, h, h, w_gate, w_lin, conv_w, conv_b)


def _split_cast_kernel(wg_ref, wl_ref, og_ref, ol_ref, *, n_valid):
    j = pl.program_id(0)

    @pl.when(j < n_valid)
    def _():
        og_ref[...] = wg_ref[...].astype(BF16)
        ol_ref[...] = wl_ref[...].astype(BF16)

    @pl.when(j >= n_valid)
    def _():
        og_ref[...] = jnp.zeros_like(og_ref)
        ol_ref[...] = jnp.zeros_like(ol_ref)


def _split_cast_pad(w, d_ff, ffp):
    d = w.shape[0]
    assert d_ff % LANES == 0 and ffp % LANES == 0
    tw = _pick(math.gcd(d_ff, ffp), 256, LANES)
    n_valid, n_tiles = d_ff // tw, ffp // tw
    out = jax.ShapeDtypeStruct((d, ffp), BF16)
    return pl.pallas_call(
        functools.partial(_split_cast_kernel, n_valid=n_valid),
        grid=(n_tiles,),
        in_specs=[pl.BlockSpec((d, tw), lambda j: (0, jnp.minimum(j, n_valid - 1))),
                  pl.BlockSpec((d, tw), lambda j: (0, n_valid + jnp.minimum(j, n_valid - 1)))],
        out_specs=[pl.BlockSpec((d, tw), lambda j: (0, j)), pl.BlockSpec((d, tw), lambda j: (0, j))],
        out_shape=[out, out],
        compiler_params=_params(("arbitrary",), 2 * d * tw * 4 + 2 * d * tw * 2),
        name="w_up_cast",
    )(w, w)


def _cast_pad_rows_kernel(w_ref, o_ref, *, n_valid):
    i = pl.program_id(0)

    @pl.when(i < n_valid)
    def _():
        o_ref[...] = w_ref[...].astype(BF16)

    @pl.when(i >= n_valid)
    def _():
        o_ref[...] = jnp.zeros_like(o_ref)


def _cast_pad_rows(w, rows_p):
    k, n = w.shape
    assert k % LANES == 0 and rows_p % LANES == 0
    tr = _pick(math.gcd(k, rows_p), 256, LANES)
    n_valid = k // tr
    return pl.pallas_call(
        functools.partial(_cast_pad_rows_kernel, n_valid=n_valid),
        grid=(rows_p // tr,),
        in_specs=[pl.BlockSpec((tr, n), lambda i: (jnp.minimum(i, n_valid - 1), 0))],
        out_specs=pl.BlockSpec((tr, n), lambda i: (i, 0)),
        out_shape=jax.ShapeDtypeStruct((rows_p, n), BF16),
        compiler_params=_params(("arbitrary",), tr * n * 6),
        name="w_down_cast",
    )(w)


def kernel(x, c, ctx, c_ctx, w_ada, b_ada, attn_norm_g, w_in, b_gate, da_qn_g, da_kn_g, da_lambda_q1, da_lambda_k1, da_lambda_q2, da_lambda_k2, da_subln_g, wa_qn_g, wa_kn_g, wa_sink, w_o_da, w_o_wa, w_out, ffn_norm_g, w_ffn_up, ffn_conv_w, ffn_conv_b, w_ffn_down):
    b, s, d = x.shape
    cl = ctx.shape[1]
    assert b == 1 and w_in.shape[0] == 1, "one batch element, one layer"
    da_heads = w_o_da.shape[1] // DA_V_DIM
    wa_q_heads = w_o_wa.shape[1] // WA_HEAD_DIM
    wa_kv_heads = wa_q_heads // WA_GROUP
    d_ff = ffn_conv_b.shape[1]
    gate_off = (3 * da_heads + wa_q_heads + 2 * wa_kv_heads) * LANES
    assert w_in.shape[2] == gate_off + N_BRANCH * d

    x2 = x.reshape(s, d)
    ctx2 = ctx.reshape(cl, d)

    ffp = -(-d_ff // 1024) * 1024 if d_ff > 1024 else d_ff
    padf = ffp - d_ff
    w_in_b = w_in[0].astype(BF16)
    w_o_da_b = w_o_da[0].astype(BF16)
    w_o_wa_b = w_o_wa[0].astype(BF16)
    w_out_b = w_out[0].astype(BF16)
    w_gate_b, w_lin_b = _split_cast_pad(w_ffn_up[0], d_ff, ffp)
    w_down_b = _cast_pad_rows(w_ffn_down[0], ffp)
    conv_w = jnp.pad(ffn_conv_w[0], ((0, 0), (0, padf)))
    conv_b = jnp.pad(ffn_conv_b[0], (0, padf)).reshape(1, ffp)

    cc = jnp.concatenate([c, c_ctx[None, :], jnp.zeros((6, d), F32)], axis=0)
    mod = _adaln(cc, w_ada[0], b_ada[0])
    sh1, sc1, g1 = mod[0:1, 0:d], mod[0:1, d:2 * d], mod[0:1, 2 * d:3 * d]
    sh2, sc2, g2 = mod[0:1, 3 * d:4 * d], mod[0:1, 4 * d:5 * d], mod[0:1, 5 * d:6 * d]
    mod_x1 = jnp.concatenate([sh1, sc1], axis=0)
    mod_c1 = jnp.concatenate([mod[1:2, 0:d], mod[1:2, d:2 * d]], axis=0)
    mod_x2 = jnp.concatenate([sh2, sc2], axis=0)

    h = _normmod(x2, attn_norm_g[0], mod_x1, ctx2, mod_c1)
    pr = _matmul(h, w_in_b, 1280, 512)
    tabs = _rope_tables(s, cl)
    gains = jnp.stack([jnp.tile(da_qn_g[0] * (DA_SCALE * LOG2_E), 2), jnp.tile(da_kn_g[0], 2),
                       wa_qn_g[0] * (WA_SCALE * LOG2_E), wa_kn_g[0]], axis=0)
    qkv = _headprep(pr, tabs, gains, da_heads, wa_q_heads)
    lam_vecs = jnp.stack([da_lambda_q1[0], da_lambda_k1[0], da_lambda_q2[0], da_lambda_k2[0]], axis=0)
    y_da = _diffattn(qkv, lam_vecs, da_subln_g[0], s, da_heads)
    y_wa = _winattn(qkv, wa_sink[0], s, cl, da_heads, wa_q_heads)
    u = _merge(y_da, y_wa, w_o_da_b, w_o_wa_b, pr, gate_off, b_gate[0])
    x1 = _matmul_residual(u, w_out_b, x2, g1, 1024, 1024, d, "out_proj")

    h2 = _normmod(x1, ffn_norm_g[0], mod_x2)
    act = _ffn_up(h2, w_gate_b, w_lin_b, conv_w, conv_b)
    out = _matmul_residual(act, w_down_b, x1, g2, 1024, 1024, 2816, "ffn_down")
    return out.reshape(b, s, d)
```

```python
import functools
import math

import jax
import jax.numpy as jnp
from jax import lax
from jax.experimental import pallas as pl
from jax.experimental.pallas import tpu as pltpu

F32 = jnp.float32
BF16 = jnp.bfloat16

GRID_W = 64
DA_QK_DIM = 64
DA_V_DIM = 2 * DA_QK_DIM
WA_HEAD_DIM = 128
WA_GROUP = 4
WINDOW = 128
N_BRANCH = 2
CONV_W = 3
ROPE_BASE = 10000.0
EPS = 1e-6
DA_SCALE = DA_QK_DIM ** -0.5
WA_SCALE = WA_HEAD_DIM ** -0.5
NEG_INF = -1e30
LAM_INIT = 0.8 - 0.6 * math.exp(-0.3 * 0)

LANES = 128
BF16_SUBLANES = 16
V7X_VMEM_CAP_BYTES = 58 * 1024 * 1024
VMEM_SLACK_BYTES = 6 * 1024 * 1024


def _pick(dim, pref, unit):
    best = None
    t = unit
    while t <= min(dim, pref):
        if dim % t == 0:
            best = t
        t += unit
    if best is None:
        raise ValueError(f"no tile for dim={dim} unit={unit}")
    return best


def _params(semantics, block_bytes, scratch_bytes=0):
    est = 2 * block_bytes + scratch_bytes + VMEM_SLACK_BYTES
    return pltpu.CompilerParams(
        dimension_semantics=semantics,
        vmem_limit_bytes=min(max(est, 16 * 1024 * 1024), V7X_VMEM_CAP_BYTES))


def _sigmoid(x):
    return 1.0 / (1.0 + jnp.exp(-x))


def _adaln_kernel(c_ref, w_ref, b_ref, o_ref):
    a = c_ref[...]
    a = a * _sigmoid(a)
    o_ref[...] = jnp.dot(a.astype(BF16), w_ref[...].astype(BF16),
                         preferred_element_type=F32) + b_ref[...]


def _adaln(cc, w, b):
    rows, d = cc.shape
    n = w.shape[1]
    tn = _pick(n, 512, LANES)
    return pl.pallas_call(
        _adaln_kernel,
        grid=(n // tn,),
        in_specs=[pl.BlockSpec((rows, d), lambda j: (0, 0)),
                  pl.BlockSpec((d, tn), lambda j: (0, j)),
                  pl.BlockSpec((1, tn), lambda j: (0, j))],
        out_specs=pl.BlockSpec((rows, tn), lambda j: (0, j)),
        out_shape=jax.ShapeDtypeStruct((rows, n), F32),
        compiler_params=_params(("arbitrary",), d * tn * 4 + d * tn * 2),
        name="adaln",
    )(cc, w, b.reshape(1, n))


def _normmod_rows(x, g, mod):
    ms = jnp.mean(x * x, axis=-1, keepdims=True)
    y = x * lax.rsqrt(ms + EPS) * g
    return (y * (1.0 + mod[1:2, :]) + mod[0:1, :]).astype(BF16)


def _normmod2_kernel(x_ref, c_ref, g_ref, mx_ref, mc_ref, o_ref, *, nx):
    i = pl.program_id(0)

    @pl.when(i < nx)
    def _():
        o_ref[...] = _normmod_rows(x_ref[...], g_ref[...], mx_ref[...])

    @pl.when(i >= nx)
    def _():
        o_ref[...] = _normmod_rows(c_ref[...], g_ref[...], mc_ref[...])


def _normmod1_kernel(x_ref, g_ref, mx_ref, o_ref):
    o_ref[...] = _normmod_rows(x_ref[...], g_ref[...], mx_ref[...])


def _normmod(x, g, mod_x, ctx=None, mod_c=None):
    s, d = x.shape
    g = g.reshape(1, d)
    if ctx is None:
        tm = _pick(s, 256, BF16_SUBLANES)
        return pl.pallas_call(
            _normmod1_kernel,
            grid=(s // tm,),
            in_specs=[pl.BlockSpec((tm, d), lambda i: (i, 0)),
                      pl.BlockSpec((1, d), lambda i: (0, 0)),
                      pl.BlockSpec((2, d), lambda i: (0, 0))],
            out_specs=pl.BlockSpec((tm, d), lambda i: (i, 0)),
            out_shape=jax.ShapeDtypeStruct((s, d), BF16),
            compiler_params=_params(("arbitrary",), tm * d * 6),
            name="normmod",
        )(x, g, mod_x)
    c = ctx.shape[0]
    tm = _pick(math.gcd(s, c), 256, BF16_SUBLANES)
    nx, nc = s // tm, c // tm
    return pl.pallas_call(
        functools.partial(_normmod2_kernel, nx=nx),
        grid=(nx + nc,),
        in_specs=[pl.BlockSpec((tm, d), lambda i: (jnp.minimum(i, nx - 1), 0)),
                  pl.BlockSpec((tm, d), lambda i: (jnp.maximum(i - nx, 0), 0)),
                  pl.BlockSpec((1, d), lambda i: (0, 0)),
                  pl.BlockSpec((2, d), lambda i: (0, 0)),
                  pl.BlockSpec((2, d), lambda i: (0, 0))],
        out_specs=pl.BlockSpec((tm, d), lambda i: (i, 0)),
        out_shape=jax.ShapeDtypeStruct((s + c, d), BF16),
        compiler_params=_params(("arbitrary",), tm * d * 10),
        name="normmod_xc",
    )(x, ctx, g, mod_x, mod_c)


def _mm_kernel(a_ref, w_ref, o_ref):
    o_ref[...] = jnp.dot(a_ref[...], w_ref[...], preferred_element_type=F32)


def _matmul(a, w, tm_pref, tn_pref):
    m, k = a.shape
    n = w.shape[1]
    tm = _pick(m, tm_pref, BF16_SUBLANES)
    tn = _pick(n, tn_pref, LANES)
    return pl.pallas_call(
        _mm_kernel,
        grid=(m // tm, n // tn),
        in_specs=[pl.BlockSpec((tm, k), lambda i, j: (i, 0)),
                  pl.BlockSpec((k, tn), lambda i, j: (0, j))],
        out_specs=pl.BlockSpec((tm, tn), lambda i, j: (i, j)),
        out_shape=jax.ShapeDtypeStruct((m, n), F32),
        compiler_params=_params(("arbitrary", "arbitrary"),
                                tm * k * 2 + k * tn * 2 + tm * tn * 4, scratch_bytes=tm * tn * 4),
        name="in_proj",
    )(a, w)


def _rope_tables(s, c):
    pos = jnp.arange(s, dtype=jnp.int32)
    r = (pos // GRID_W).astype(F32)[:, None]
    col = (pos % GRID_W).astype(F32)[:, None]
    lane = jnp.arange(LANES, dtype=jnp.int32)

    def table(head_dim):
        axis_dim = head_dim // 2
        pair = axis_dim // 2
        within = lane % head_dim
        f = (within % pair).astype(F32)
        freq = ROPE_BASE ** (-(2.0 * f) / axis_dim)
        use_row = (within // axis_dim) == 0
        ang = jnp.where(use_row[None, :], r * freq[None, :], col * freq[None, :])
        first = (within % axis_dim) < pair
        cos = jnp.cos(ang)
        sin = jnp.where(first[None, :], -jnp.sin(ang), jnp.sin(ang))
        cos = jnp.concatenate([cos, jnp.ones((c, LANES), F32)], axis=0)
        sin = jnp.concatenate([sin, jnp.zeros((c, LANES), F32)], axis=0)
        return cos, sin

    cd, sd = table(DA_QK_DIM)
    cw, sw = table(WA_HEAD_DIM)
    return jnp.stack([cd, sd, cw, sw], axis=0)


def _headprep_kernel(pr_ref, tab_ref, gain_ref, ones_da_ref, ones_wa_ref, o_ref,
                     *, n_qk_da, n_v_da, n_q_wa, n_k_wa, n_v_wa):
    tm = pr_ref.shape[0]
    lane = lax.broadcasted_iota(jnp.int32, (tm, LANES), 1)
    first_da = (lane % (DA_QK_DIM // 2)) < (DA_QK_DIM // 4)
    first_wa = (lane % (WA_HEAD_DIM // 2)) < (WA_HEAD_DIM // 4)
    cos_da, sin_da = tab_ref[0], tab_ref[1]
    cos_wa, sin_wa = tab_ref[2], tab_ref[3]

    def rope(y, cos, sin, first, shift):
        partner = jnp.where(first, pltpu.roll(y, LANES - shift, 1), pltpu.roll(y, shift, 1))
        return y * cos + partner * sin

    def head_sum_sq(x, ones_ref):
        x2 = x * x
        hi = x2.astype(BF16)
        mid = (x2 - hi.astype(F32)).astype(BF16)
        return jnp.dot(jnp.concatenate([hi, mid], axis=1), ones_ref[...], preferred_element_type=F32)

    def da_group(gidx, gain):
        x = pr_ref[:, gidx * LANES:(gidx + 1) * LANES]
        ms = head_sum_sq(x, ones_da_ref) * (1.0 / DA_QK_DIM)
        y = x * lax.rsqrt(ms + EPS) * gain
        y = rope(y, cos_da, sin_da, first_da, DA_QK_DIM // 4)
        o_ref[:, gidx * LANES:(gidx + 1) * LANES] = y.astype(BF16)

    def wa_group(gidx, gain):
        x = pr_ref[:, gidx * LANES:(gidx + 1) * LANES]
        ms = head_sum_sq(x, ones_wa_ref) * (1.0 / WA_HEAD_DIM)
        y = x * lax.rsqrt(ms + EPS) * gain
        y = rope(y, cos_wa, sin_wa, first_wa, WA_HEAD_DIM // 4)
        o_ref[:, gidx * LANES:(gidx + 1) * LANES] = y.astype(BF16)

    def copy_group(gidx):
        o_ref[:, gidx * LANES:(gidx + 1) * LANES] = pr_ref[:, gidx * LANES:(gidx + 1) * LANES].astype(BF16)

    g = 0
    for _ in range(n_qk_da):
        da_group(g, gain_ref[0:1, :]); g += 1
    for _ in range(n_qk_da):
        da_group(g, gain_ref[1:2, :]); g += 1
    for _ in range(n_v_da):
        copy_group(g); g += 1
    for _ in range(n_q_wa):
        wa_group(g, gain_ref[2:3, :]); g += 1
    for _ in range(n_k_wa):
        wa_group(g, gain_ref[3:4, :]); g += 1
    for _ in range(n_v_wa):
        copy_group(g); g += 1


def _headprep(pr, tabs, gains, da_heads, wa_q_heads, tm_pref=256):
    rows = pr.shape[0]
    wa_kv_heads = wa_q_heads // WA_GROUP
    groups = dict(n_qk_da=da_heads, n_v_da=da_heads, n_q_wa=wa_q_heads,
                  n_k_wa=wa_kv_heads, n_v_wa=wa_kv_heads)
    width = (3 * da_heads + wa_q_heads + 2 * wa_kv_heads) * LANES
    tm = _pick(rows, tm_pref, BF16_SUBLANES)
    lane = jnp.arange(LANES)
    same_da_head = (lane[:, None] // DA_QK_DIM) == (lane[None, :] // DA_QK_DIM)
    ones_da = jnp.tile(same_da_head.astype(BF16), (2, 1))
    ones_wa = jnp.ones((2 * LANES, LANES), BF16)
    return pl.pallas_call(
        functools.partial(_headprep_kernel, **groups),
        grid=(rows // tm,),
        in_specs=[pl.BlockSpec((tm, width), lambda i: (i, 0)),
                  pl.BlockSpec((4, tm, LANES), lambda i: (0, i, 0)),
                  pl.BlockSpec((4, LANES), lambda i: (0, 0)),
                  pl.BlockSpec((2 * LANES, LANES), lambda i: (0, 0)),
                  pl.BlockSpec((2 * LANES, LANES), lambda i: (0, 0))],
        out_specs=pl.BlockSpec((tm, width), lambda i: (i, 0)),
        out_shape=jax.ShapeDtypeStruct((rows, width), BF16),
        compiler_params=_params(("arbitrary",), tm * width * 6 + 4 * tm * LANES * 4),
        name="headprep",
    )(pr, tabs, gains, ones_da, ones_wa)


def _nt_dot(a, b):
    return lax.dot_general(a, b, (((1,), (1,)), ((), ())), preferred_element_type=F32)


LOG2_E = math.log2(math.e)


def _diffattn_kernel(lam_ref, q_ref, k_ref, v_ref, g_ref, o_ref, *, tk, rb):
    tq = q_ref.shape[0]
    nk = k_ref.shape[0] // tk
    q = q_ref[...]
    lane = lax.broadcasted_iota(jnp.int32, (tq, LANES), 1)
    zero = jnp.zeros_like(q)
    qs = (jnp.where(lane < DA_QK_DIM, q, zero), jnp.where(lane < DA_QK_DIM, zero, q))
    ones = jnp.ones((tk, LANES), BF16)

    def scores(c):
        k = k_ref[c * tk:(c + 1) * tk, :]
        return [_nt_dot(qs[h], k).astype(BF16) for h in range(2)]

    def softmax(ss, ms):
        ps, new_ms, alphas = [], [], []
        for h in range(2):
            p_blocks, m_blocks, a_blocks = [], [], []
            for r in range(0, tq, rb):
                s = ss[h][r:r + rb]
                m_old = ms[h][r:r + rb]
                m_new = jnp.maximum(m_old, jnp.max(s, axis=-1, keepdims=True).astype(F32))
                p_blocks.append(jnp.exp2(s - m_new.astype(BF16)))
                m_blocks.append(m_new)
                a_blocks.append(jnp.exp2(m_old - m_new))
            ps.append(jnp.concatenate(p_blocks, axis=0))
            new_ms.append(jnp.concatenate(m_blocks, axis=0))
            alphas.append(jnp.concatenate(a_blocks, axis=0))
        return ps, new_ms, alphas

    def accumulate(c, ps, alphas, accs):
        v1 = jnp.concatenate([v_ref[c * tk:(c + 1) * tk, :], ones], axis=1)
        return [alphas[h] * accs[h] + jnp.dot(ps[h], v1, preferred_element_type=F32)
                for h in range(2)]

    m0 = jnp.full((tq, 1), NEG_INF, F32)
    a0 = jnp.zeros((tq, 2 * DA_V_DIM), F32)
    ss = {0: scores(0), 1: scores(1)}
    p_cur, ms, al_cur = softmax(ss.pop(0), [m0, m0])
    accs = [a0, a0]
    for c in range(nk):
        if c + 1 < nk:
            p_next, ms, al_next = softmax(ss.pop(c + 1), ms)
        accs = accumulate(c, p_cur, al_cur, accs)
        if c + 2 < nk:
            ss[c + 2] = scores(c + 2)
        if c + 1 < nk:
            p_cur, al_cur = p_next, al_next
    a1, a2 = accs

    lam = (jnp.exp(jnp.sum(lam_ref[0:1, :] * lam_ref[1:2, :], axis=-1, keepdims=True))
           - jnp.exp(jnp.sum(lam_ref[2:3, :] * lam_ref[3:4, :], axis=-1, keepdims=True))
           + LAM_INIT)
    o = (a1[:, :DA_V_DIM] / a1[:, DA_V_DIM:]) - lam * (a2[:, :DA_V_DIM] / a2[:, DA_V_DIM:])
    ms_o = jnp.mean(o * o, axis=-1, keepdims=True)
    o = o * lax.rsqrt(ms_o + EPS) * g_ref[...] * (1.0 - LAM_INIT)
    o_ref[...] = o.astype(BF16)


def _diffattn(qkv, lam_vecs, subln_g, s, da_heads, tq_pref=512, tk_pref=1280):
    rows = qkv.shape[0]
    tq = _pick(s, tq_pref, BF16_SUBLANES)
    tk = _pick(rows, min(tk_pref, rows // 2), LANES)
    kb, vb = da_heads, 2 * da_heads
    return pl.pallas_call(
        functools.partial(_diffattn_kernel, tk=tk, rb=BF16_SUBLANES),
        grid=(da_heads, s // tq),
        in_specs=[pl.BlockSpec((4, DA_QK_DIM), lambda h, i: (0, 0)),
                  pl.BlockSpec((tq, LANES), lambda h, i: (i, h)),
                  pl.BlockSpec((rows, LANES), lambda h, i: (0, kb + h)),
                  pl.BlockSpec((rows, LANES), lambda h, i: (0, vb + h)),
                  pl.BlockSpec((1, DA_V_DIM), lambda h, i: (0, 0))],
        out_specs=pl.BlockSpec((tq, DA_V_DIM), lambda h, i: (i, h)),
        out_shape=jax.ShapeDtypeStruct((s, da_heads * DA_V_DIM), BF16),
        compiler_params=_params(("arbitrary", "arbitrary"),
                                2 * rows * LANES * 2 + 2 * tq * LANES * 2,
                                scratch_bytes=4 * tq * tk * 6 + 4 * tq * tk * 4),
        name="diff_attn",
    )(lam_vecs, qkv, qkv, qkv, subln_g.reshape(1, DA_V_DIM))


def _winattn_kernel(sink_ref, q_ref, k_ref, v_ref, o_ref, *, s, c):
    g = pl.program_id(0)
    i = pl.program_id(1)
    tq = q_ref.shape[0]
    band = 3 * WINDOW
    rows = WA_GROUP * WINDOW
    kc = k_ref[s:s + c, :]
    vc = v_ref[s:s + c, :]
    row = lax.broadcasted_iota(jnp.int32, (rows, band), 0)
    col = lax.broadcasted_iota(jnp.int32, (rows, band), 1)
    head = lax.broadcasted_iota(jnp.int32, (rows, 1), 0) // WINDOW
    sink = jnp.zeros((rows, 1), F32)
    for r in range(WA_GROUP):
        sink = jnp.where(head == r, sink_ref[g * WA_GROUP + r] * LOG2_E, sink)
    for b in range(tq // WINDOW):
        q0 = i * tq + b * WINDOW
        start = pl.multiple_of(jnp.clip(q0 - WINDOW, 0, s - band), WINDOW)
        kb = k_ref[pl.ds(start, band), :]
        vb = v_ref[pl.ds(start, band), :]
        valid = jnp.abs(q0 + row % WINDOW - (start + col)) <= WINDOW
        q = jnp.concatenate(
            [q_ref[b * WINDOW:(b + 1) * WINDOW, r * WA_HEAD_DIM:(r + 1) * WA_HEAD_DIM]
             for r in range(WA_GROUP)], axis=0)
        sb = jnp.where(valid, _nt_dot(q, kb), NEG_INF)
        sc = _nt_dot(q, kc)
        m = jnp.maximum(jnp.max(sb, axis=-1, keepdims=True), jnp.max(sc, axis=-1, keepdims=True))
        m = jnp.maximum(m, sink)
        pb = jnp.exp2(sb - m)
        pc = jnp.exp2(sc - m)
        l = (jnp.sum(pb, axis=-1, keepdims=True) + jnp.sum(pc, axis=-1, keepdims=True)
             + jnp.exp2(sink - m))
        o = (jnp.dot(pb.astype(BF16), vb, preferred_element_type=F32)
             + jnp.dot(pc.astype(BF16), vc, preferred_element_type=F32)) * (1.0 / l)
        for r in range(WA_GROUP):
            o_ref[b * WINDOW:(b + 1) * WINDOW, r * WA_HEAD_DIM:(r + 1) * WA_HEAD_DIM] = (
                o[r * WINDOW:(r + 1) * WINDOW].astype(BF16))


def _winattn(qkv, sink, s, c, da_heads, wa_q_heads, tq_pref=512):
    rows = qkv.shape[0]
    kvh = wa_q_heads // WA_GROUP
    tq = _pick(s, tq_pref, WINDOW)
    qb = 3 * da_heads // WA_GROUP
    kb = 3 * da_heads + wa_q_heads
    vb = kb + kvh
    gw = WA_GROUP * WA_HEAD_DIM
    return pl.pallas_call(
        functools.partial(_winattn_kernel, s=s, c=c),
        grid=(kvh, s // tq),
        in_specs=[pl.BlockSpec(memory_space=pltpu.SMEM),
                  pl.BlockSpec((tq, gw), lambda g, i: (i, qb + g)),
                  pl.BlockSpec((rows, LANES), lambda g, i: (0, kb + g)),
                  pl.BlockSpec((rows, LANES), lambda g, i: (0, vb + g))],
        out_specs=pl.BlockSpec((tq, gw), lambda g, i: (i, g)),
        out_shape=jax.ShapeDtypeStruct((s, wa_q_heads * WA_HEAD_DIM), BF16),
        compiler_params=_params(("arbitrary", "arbitrary"),
                                2 * rows * LANES * 2 + 2 * tq * gw * 2,
                                scratch_bytes=8 * tq * (tq + 2 * WINDOW + c) * 4),
        name="win_attn",
    )(sink, qkv, qkv, qkv)


def _merge_kernel(yd_ref, yw_ref, wd_ref, ww_ref, gd_ref, gw_ref, bd_ref, bw_ref, o_ref):
    pd = jnp.dot(yd_ref[...], wd_ref[...], preferred_element_type=F32)
    pw = jnp.dot(yw_ref[...], ww_ref[...], preferred_element_type=F32)
    gd = _sigmoid(gd_ref[...] + bd_ref[...])
    gw = _sigmoid(gw_ref[...] + bw_ref[...])
    o_ref[...] = (gd * pd + gw * pw).astype(BF16)


def _merge(y_da, y_wa, w_o_da, w_o_wa, pr, gate_off, b_gate, tm_pref=512, tn_pref=1024):
    s, kd = y_da.shape
    kw = y_wa.shape[1]
    d = w_o_da.shape[1]
    tm = _pick(s, tm_pref, BF16_SUBLANES)
    tn = _pick(math.gcd(d, gate_off), tn_pref, LANES)
    od, ow = gate_off // tn, (gate_off + d) // tn
    nb = d // tn
    bg = b_gate.reshape(1, N_BRANCH * d)
    return pl.pallas_call(
        _merge_kernel,
        grid=(s // tm, nb),
        in_specs=[pl.BlockSpec((tm, kd), lambda i, j: (i, 0)),
                  pl.BlockSpec((tm, kw), lambda i, j: (i, 0)),
                  pl.BlockSpec((kd, tn), lambda i, j: (0, j)),
                  pl.BlockSpec((kw, tn), lambda i, j: (0, j)),
                  pl.BlockSpec((tm, tn), lambda i, j: (i, od + j)),
                  pl.BlockSpec((tm, tn), lambda i, j: (i, ow + j)),
                  pl.BlockSpec((1, tn), lambda i, j: (0, j)),
                  pl.BlockSpec((1, tn), lambda i, j: (0, nb + j))],
        out_specs=pl.BlockSpec((tm, tn), lambda i, j: (i, j)),
        out_shape=jax.ShapeDtypeStruct((s, d), BF16),
        compiler_params=_params(("arbitrary", "arbitrary"),
                                tm * (kd + kw) * 2 + (kd + kw) * tn * 2 + 2 * tm * tn * 4 + tm * tn * 2,
                                scratch_bytes=4 * tm * tn * 4),
        name="merge",
    )(y_da, y_wa, w_o_da, w_o_wa, pr, pr, bg, bg)


def _mm_res_kernel(a_ref, w_ref, x_ref, g_ref, o_ref, acc_ref):
    kk = pl.program_id(2)

    @pl.when(kk == 0)
    def _():
        acc_ref[...] = jnp.zeros_like(acc_ref)

    acc_ref[...] += jnp.dot(a_ref[...], w_ref[...], preferred_element_type=F32)

    @pl.when(kk == pl.num_programs(2) - 1)
    def _():
        o_ref[...] = x_ref[...] + g_ref[...] * acc_ref[...]


def _matmul_residual(a, w, x, gate, tm_pref, tn_pref, tk_pref, name):
    m, k = a.shape
    n = w.shape[1]
    tm = _pick(m, tm_pref, BF16_SUBLANES)
    tn = _pick(n, tn_pref, LANES)
    tk = _pick(k, tk_pref, LANES)
    return pl.pallas_call(
        _mm_res_kernel,
        grid=(m // tm, n // tn, k // tk),
        in_specs=[pl.BlockSpec((tm, tk), lambda i, j, kk: (i, kk)),
                  pl.BlockSpec((tk, tn), lambda i, j, kk: (kk, j)),
                  pl.BlockSpec((tm, tn), lambda i, j, kk: (i, j)),
                  pl.BlockSpec((1, tn), lambda i, j, kk: (0, j))],
        out_specs=pl.BlockSpec((tm, tn), lambda i, j, kk: (i, j)),
        out_shape=jax.ShapeDtypeStruct((m, n), F32),
        scratch_shapes=[pltpu.VMEM((tm, tn), F32)],
        compiler_params=_params(("arbitrary", "arbitrary", "arbitrary"),
                                tm * tk * 2 + tk * tn * 2 + 2 * tm * tn * 4,
                                scratch_bytes=2 * tm * tn * 4),
        name=name,
    )(a, w, x, gate)


HALO = BF16_SUBLANES


def _ffn_up_kernel(h_ref, hp_ref, hn_ref, wa_ref, wu_ref, cw_ref, cb_ref, o_ref, lhs_ref):
    i = pl.program_id(0)
    j = pl.program_id(1)
    tm = h_ref.shape[0]

    @pl.when(j == 0)
    def _():
        prev = jnp.where(i > 0, hp_ref[...], jnp.zeros_like(hp_ref))
        nxt = jnp.where(i < pl.num_programs(0) - 1, hn_ref[...], jnp.zeros_like(hn_ref))
        lhs_ref[0:HALO, :] = prev
        lhs_ref[HALO:HALO + tm, :] = h_ref[...]
        lhs_ref[HALO + tm:, :] = nxt

    a = jnp.dot(lhs_ref[...], wa_ref[...], preferred_element_type=F32)
    u = jnp.dot(lhs_ref[HALO:HALO + tm, :], wu_ref[...], preferred_element_type=F32)
    rows = tm + 2 * HALO
    a_prev = pltpu.roll(a, 1, 0)[HALO:HALO + tm, :]
    a_next = pltpu.roll(a, rows - 1, 0)[HALO:HALO + tm, :]
    a_mid = a[HALO:HALO + tm, :]
    conv = cb_ref[...] + a_prev * cw_ref[0:1, :] + a_mid * cw_ref[1:2, :] + a_next * cw_ref[2:3, :]
    o_ref[...] = (conv * _sigmoid(conv) * u).astype(BF16)


def _ffn_up(h, w_gate, w_lin, conv_w, conv_b, tm_pref=1024, tn_pref=512):
    s, d = h.shape
    ffp = w_gate.shape[1]
    tm = _pick(s, tm_pref, HALO)
    tn = _pick(ffp, tn_pref, LANES)
    nf = ffp // tn
    per = tm // HALO
    last = s // HALO - 1
    return pl.pallas_call(
        _ffn_up_kernel,
        grid=(s // tm, nf),
        in_specs=[pl.BlockSpec((tm, d), lambda i, j: (i, 0)),
                  pl.BlockSpec((HALO, d), lambda i, j: (jnp.maximum(i * per - 1, 0), 0)),
                  pl.BlockSpec((HALO, d), lambda i, j: (jnp.minimum((i + 1) * per, last), 0)),
                  pl.BlockSpec((d, tn), lambda i, j: (0, j)),
                  pl.BlockSpec((d, tn), lambda i, j: (0, j)),
                  pl.BlockSpec((CONV_W, tn), lambda i, j: (0, j)),
                  pl.BlockSpec((1, tn), lambda i, j: (0, j))],
        out_specs=pl.BlockSpec((tm, tn), lambda i, j: (i, j)),
        out_shape=jax.ShapeDtypeStruct((s, ffp), BF16),
        scratch_shapes=[pltpu.VMEM((tm + 2 * HALO, d), BF16)],
        compiler_params=_params(("arbitrary", "arbitrary"),
                                (tm + 2 * HALO) * d * 2 + 2 * d * tn * 2 + tm * tn * 2,
                                scratch_bytes=(tm + 2 * HALO) * d * 2 + 6 * (tm + 2 * HALO) * tn * 4),
        name="ffn_up",
    )(h, h, h, w_gate, w_lin, conv_w, conv_b)


def _split_cast_kernel(wg_ref, wl_ref, og_ref, ol_ref, *, n_valid):
    j = pl.program_id(0)

    @pl.when(j < n_valid)
    def _():
        og_ref[...] = wg_ref[...].astype(BF16)
        ol_ref[...] = wl_ref[...].astype(BF16)

    @pl.when(j >= n_valid)
    def _():
        og_ref[...] = jnp.zeros_like(og_ref)
        ol_ref[...] = jnp.zeros_like(ol_ref)


def _split_cast_pad(w, d_ff, ffp):
    d = w.shape[0]
    assert d_ff % LANES == 0 and ffp % LANES == 0
    tw = _pick(math.gcd(d_ff, ffp), 256, LANES)
    n_valid, n_tiles = d_ff // tw, ffp // tw
    out = jax.ShapeDtypeStruct((d, ffp), BF16)
    return pl.pallas_call(
        functools.partial(_split_cast_kernel, n_valid=n_valid),
        grid=(n_tiles,),
        in_specs=[pl.BlockSpec((d, tw), lambda j: (0, jnp.minimum(j, n_valid - 1))),
                  pl.BlockSpec((d, tw), lambda j: (0, n_valid + jnp.minimum(j, n_valid - 1)))],
        out_specs=[pl.BlockSpec((d, tw), lambda j: (0, j)), pl.BlockSpec((d, tw), lambda j: (0, j))],
        out_shape=[out, out],
        compiler_params=_params(("arbitrary",), 2 * d * tw * 4 + 2 * d * tw * 2),
        name="w_up_cast",
    )(w, w)


def _cast_pad_rows_kernel(w_ref, o_ref, *, n_valid):
    i = pl.program_id(0)

    @pl.when(i < n_valid)
    def _():
        o_ref[...] = w_ref[...].astype(BF16)

    @pl.when(i >= n_valid)
    def _():
        o_ref[...] = jnp.zeros_like(o_ref)


def _cast_pad_rows(w, rows_p):
    k, n = w.shape
    assert k % LANES == 0 and rows_p % LANES == 0
    tr = _pick(math.gcd(k, rows_p), 256, LANES)
    n_valid = k // tr
    return pl.pallas_call(
        functools.partial(_cast_pad_rows_kernel, n_valid=n_valid),
        grid=(rows_p // tr,),
        in_specs=[pl.BlockSpec((tr, n), lambda i: (jnp.minimum(i, n_valid - 1), 0))],
        out_specs=pl.BlockSpec((tr, n), lambda i: (i, 0)),
        out_shape=jax.ShapeDtypeStruct((rows_p, n), BF16),
        compiler_params=_params(("arbitrary",), tr * n * 6),
        name="w_down_cast",
    )(w)


def kernel(x, c, ctx, c_ctx, w_ada, b_ada, attn_norm_g, w_in, b_gate, da_qn_g, da_kn_g, da_lambda_q1, da_lambda_k1, da_lambda_q2, da_lambda_k2, da_subln_g, wa_qn_g, wa_kn_g, wa_sink, w_o_da, w_o_wa, w_out, ffn_norm_g, w_ffn_up, ffn_conv_w, ffn_conv_b, w_ffn_down):
    b, s, d = x.shape
    cl = ctx.shape[1]
    assert b == 1 and w_in.shape[0] == 1, "one batch element, one layer"
    da_heads = w_o_da.shape[1] // DA_V_DIM
    wa_q_heads = w_o_wa.shape[1] // WA_HEAD_DIM
    wa_kv_heads = wa_q_heads // WA_GROUP
    d_ff = ffn_conv_b.shape[1]
    gate_off = (3 * da_heads + wa_q_heads + 2 * wa_kv_heads) * LANES
    assert w_in.shape[2] == gate_off + N_BRANCH * d

    x2 = x.reshape(s, d)
    ctx2 = ctx.reshape(cl, d)

    ffp = -(-d_ff // 1024) * 1024 if d_ff > 1024 else d_ff
    padf = ffp - d_ff
    w_in_b = w_in[0].astype(BF16)
    w_o_da_b = w_o_da[0].astype(BF16)
    w_o_wa_b = w_o_wa[0].astype(BF16)
    w_out_b = w_out[0].astype(BF16)
    w_gate_b, w_lin_b = _split_cast_pad(w_ffn_up[0], d_ff, ffp)
    w_down_b = _cast_pad_rows(w_ffn_down[0], ffp)
    conv_w = jnp.pad(ffn_conv_w[0], ((0, 0), (0, padf)))
    conv_b = jnp.pad(ffn_conv_b[0], (0, padf)).reshape(1, ffp)

    cc = jnp.concatenate([c, c_ctx[None, :], jnp.zeros((6, d), F32)], axis=0)
    mod = _adaln(cc, w_ada[0], b_ada[0])
    sh1, sc1, g1 = mod[0:1, 0:d], mod[0:1, d:2 * d], mod[0:1, 2 * d:3 * d]
    sh2, sc2, g2 = mod[0:1, 3 * d:4 * d], mod[0:1, 4 * d:5 * d], mod[0:1, 5 * d:6 * d]
    mod_x1 = jnp.concatenate([sh1, sc1], axis=0)
    mod_c1 = jnp.concatenate([mod[1:2, 0:d], mod[1:2, d:2 * d]], axis=0)
    mod_x2 = jnp.concatenate([sh2, sc2], axis=0)

    h = _normmod(x2, attn_norm_g[0], mod_x1, ctx2, mod_c1)
    pr = _matmul(h, w_in_b, 1280, 512)
    tabs = _rope_tables(s, cl)
    gains = jnp.stack([jnp.tile(da_qn_g[0] * (DA_SCALE * LOG2_E), 2), jnp.tile(da_kn_g[0], 2),
                       wa_qn_g[0] * (WA_SCALE * LOG2_E), wa_kn_g[0]], axis=0)
    qkv = _headprep(pr, tabs, gains, da_heads, wa_q_heads)
    lam_vecs = jnp.stack([da_lambda_q1[0], da_lambda_k1[0], da_lambda_q2[0], da_lambda_k2[0]], axis=0)
    y_da = _diffattn(qkv, lam_vecs, da_subln_g[0], s, da_heads)
    y_wa = _winattn(qkv, wa_sink[0], s, cl, da_heads, wa_q_heads)
    u = _merge(y_da, y_wa, w_o_da_b, w_o_wa_b, pr, gate_off, b_gate[0])
    x1 = _matmul_residual(u, w_out_b, x2, g1, 1024, 1024, d, "out_proj")

    h2 = _normmod(x1, ffn_norm_g[0], mod_x2)
    act = _ffn_up(h2, w_gate_b, w_lin_b, conv_w, conv_b)
    out = _matmul_residual(act, w_down_b, x1, g2, 1024, 1024, 2816, "ffn_down")
    return out.reshape(b, s, d)
```
